```python
import jax, jax.numpy as jnp
from jax import lax
import numpy as np

D_MODEL = 1024
BATCH = 1
SEQ = 16384
DEPTH = 4

GRID_W = 64
CTX_LEN = 256
NORM_EPS = 1e-6

RW_HEADS = 6
RW_HEAD_DIM = 64
RW_WIDTH = RW_HEADS * RW_HEAD_DIM
DECAY_LORA = 64
ICLR_LORA = 64
GATE_LORA = 128
RW_LN_EPS = 64e-5
MLA_HEADS = 6
Q_LORA = 384
KV_LORA = 256
NOPE_DIM = 64
ROPE_DIM = 32
V_DIM = 64
QK_DIM = NOPE_DIM + ROPE_DIM
MLA_WIDTH = MLA_HEADS * V_DIM
ROPE_BASE = 10000.0
Q_BLOCK = 128
FT_GROUPS = 4
FT_GROUP_DIM = 64
FT_WIDTH = FT_GROUPS * FT_GROUP_DIM
D_FF = 2816
N_BRANCH = 3
N_MOD = 6

RW_COLS = 3 * RW_WIDTH + 2 * DECAY_LORA + 2 * ICLR_LORA + GATE_LORA
RW_SPLITS = (RW_WIDTH, 2 * RW_WIDTH, 3 * RW_WIDTH,
             3 * RW_WIDTH + DECAY_LORA, 3 * RW_WIDTH + 2 * DECAY_LORA,
             3 * RW_WIDTH + 2 * DECAY_LORA + ICLR_LORA, 3 * RW_WIDTH + 2 * DECAY_LORA + 2 * ICLR_LORA)
MLA_COLS = Q_LORA + KV_LORA + ROPE_DIM
IN_COLS = RW_COLS + MLA_COLS + FT_WIDTH + N_BRANCH * D_MODEL

kernel_name = 'hybrid_rwkv7_mla_fnet_dit'


def rms(x, eps=NORM_EPS):
    xf = x.astype(jnp.float32)
    return (xf * lax.rsqrt(jnp.mean(xf * xf, axis=-1, keepdims=True) + eps)).astype(x.dtype)


def modnorm(x, gain, shift, scale):
    return rms(x) * gain * (1.0 + scale) + shift


def dwconv3(u, w):
    up = jnp.pad(u, ((0, 0), (1, 1), (0, 0)))
    return w[0] * up[:, :-2] + w[1] * up[:, 1:-1] + w[2] * up[:, 2:]


def axial_rope(t, row, col):
    half = ROPE_DIM // 2
    inv = ROPE_BASE ** (-jnp.arange(0, half, 2, dtype=jnp.float32) / half)

    def rot(ta, pos):
        ang = pos.astype(jnp.float32)[:, None] * inv
        cos = jnp.cos(ang)[None, :, None, :]
        sin = jnp.sin(ang)[None, :, None, :]
        x1, x2 = jnp.split(ta.astype(jnp.float32), 2, axis=-1)
        return jnp.concatenate([x1 * cos - x2 * sin, x2 * cos + x1 * sin], axis=-1)

    out = jnp.concatenate([rot(t[..., :half], row), rot(t[..., half:], col)], axis=-1)
    return out.astype(t.dtype)


def rope_tail(t, row, col):
    return jnp.concatenate([t[..., :NOPE_DIM], axial_rope(t[..., NOPE_DIM:], row, col)], axis=-1)


def rwkv_prep(u, p):
    B, L, _ = u.shape
    u = dwconv3(u, p['rw_conv'])
    r, k, v, wd_f, wd_b, ad_f, ad_b, gd = jnp.split(u, RW_SPLITS, axis=-1)
    heads = lambda t: t.reshape(B, L, RW_HEADS, RW_HEAD_DIM)
    kk = heads((k * p['rw_k_k']).astype(jnp.float32))
    kk = kk * lax.rsqrt(jnp.sum(kk * kk, axis=-1, keepdims=True) + 1e-12)
    g = jax.nn.sigmoid(gd) @ p['rw_g_up']
    dirs = []
    for d, (wd, ad) in enumerate(((wd_f, ad_f), (wd_b, ad_b))):
        w_log = -jax.nn.softplus(-(p['rw_w0'][d] + jnp.tanh(wd) @ p['rw_w_up'][d])) - 0.5
        decay = jnp.exp(-jnp.exp(w_log.astype(jnp.float32)))
        a = jax.nn.sigmoid(p['rw_a0'][d] + ad @ p['rw_a_up'][d])
        kd = k * (1.0 + (a - 1.0) * p['rw_k_a'])
        dirs.append((heads(decay), heads(kd), heads(a)))
    return heads(r), heads(v), kk, g, dirs


def _rwkv_step(S, inp):
    r, w, k, v, a, b = inp
    sa = jnp.einsum('bhij,bhj->bhi', S, a)
    S = S * w[:, :, None, :] + sa[..., None] * b[:, :, None, :] + v[..., None] * k[:, :, None, :]
    return S, jnp.einsum('bhij,bhj->bhi', S, r)


def rwkv_scan(S0, seqs, reverse):
    xs = tuple(jnp.moveaxis(s.astype(jnp.float32), 1, 0) for s in seqs)
    S, y = lax.scan(_rwkv_step, S0, xs, reverse=reverse)
    return S, jnp.moveaxis(y, 0, 1)


def rwkv_output(prep, ys, p):
    r, v, kk, g, dirs = prep
    B, L = r.shape[:2]
    y = ys[0] + ys[1]
    mu = jnp.mean(y, axis=-1, keepdims=True)
    var = jnp.mean(jnp.square(y - mu), axis=-1, keepdims=True)
    y = ((y - mu) * lax.rsqrt(var + RW_LN_EPS)).reshape(B, L, RW_WIDTH).astype(r.dtype)
    y = y * p['rw_ln_w'] + p['rw_ln_b']
    bonus = sum(jnp.sum(r * dirs[d][1] * p['rw_r_k'], axis=-1, keepdims=True) * v for d in range(2))
    return (y + bonus.reshape(B, L, RW_WIDTH)) * g


def rwkv_mixer(u_lat, u_ctx, p, need_ctx):
    prep_c = rwkv_prep(u_ctx, p)
    prep_l = rwkv_prep(u_lat, p)
    S0 = jnp.zeros((u_lat.shape[0], RW_HEADS, RW_HEAD_DIM, RW_HEAD_DIM), jnp.float32)
    ys_c, ys_l = [], []
    for d in range(2):
        def scan_inputs(prep):
            r, v, kk, g, dirs = prep
            decay, kd, a = dirs[d]
            return (r, decay, kd, v, -kk, kk * a)
        S_c, y_c = rwkv_scan(S0, scan_inputs(prep_c), d == 1)
        _, y_l = rwkv_scan(S_c, scan_inputs(prep_l), d == 1)
        ys_c.append(y_c)
        ys_l.append(y_l)
    out_l = rwkv_output(prep_l, ys_l, p)
    out_c = rwkv_output(prep_c, ys_c, p) if need_ctx else None
    return out_l, out_c


def mla_q(u, p, row, col):
    B, L, _ = u.shape
    cq = rms(u[..., :Q_LORA]) * p['mla_q_norm']
    q = (cq @ p['w_uq']).reshape(B, L, MLA_HEADS, QK_DIM)
    q = rms(q) * p['q_gain']
    return q if row is None else rope_tail(q, row, col)


def mla_kv(u, p, row, col):
    B, L, _ = u.shape
    ckv = rms(u[..., Q_LORA:Q_LORA + KV_LORA]) * p['mla_kv_norm']
    kv = (ckv @ p['w_ukv']).reshape(B, L, MLA_HEADS, NOPE_DIM + V_DIM)
    k_rope = jnp.broadcast_to(u[..., None, Q_LORA + KV_LORA:], (B, L, MLA_HEADS, ROPE_DIM))
    k = rms(jnp.concatenate([kv[..., :NOPE_DIM], k_rope], axis=-1)) * p['k_gain']
    if row is not None:
        k = rope_tail(k, row, col)
    return k, kv[..., NOPE_DIM:]


def attend(q, k, v):
    s = jnp.einsum('bqhd,bkhd->bhqk', q, k).astype(jnp.float32) * (QK_DIM ** -0.5)
    pr = jax.nn.softmax(s, axis=-1).astype(v.dtype)
    return jnp.einsum('bhqk,bkhd->bqhd', pr, v)


def blocked_attend(q, k, v):
    B, L, H, Dk = q.shape
    nb = L // Q_BLOCK
    qb = jnp.moveaxis(q.reshape(B, nb, Q_BLOCK, H, Dk), 1, 0)
    o = lax.map(lambda qq: attend(qq, k, v), qb)
    return jnp.moveaxis(o, 0, 1).reshape(B, L, H * V_DIM)


def fourier_mix(u):
    B, L, _ = u.shape
    z = jnp.fft.fft2(u.astype(jnp.float32).reshape(B, L, FT_GROUPS, FT_GROUP_DIM), axes=(1, 3), norm='ortho')
    return jnp.real(z).reshape(B, L, FT_WIDTH).astype(u.dtype)


def split_cols(u):
    o1 = RW_COLS
    o2 = o1 + MLA_COLS
    o3 = o2 + FT_WIDTH
    return u[..., :o1], u[..., o1:o2], u[..., o2:o3], u[..., o3:]


def merge(gate_u, yA, yB, yC, p):
    gA, gB, gC = jnp.split(jax.nn.sigmoid(gate_u), N_BRANCH, axis=-1)
    m = gA * (yA @ p['w_branch_a']) + gB * (yB @ p['w_branch_b']) + gC * (yC @ p['w_branch_c'])
    return m @ p['w_out']


def token_mixing(h_lat, h_ctx, p, row, col, last):
    B, L, _ = h_lat.shape
    a_lat, b_lat, f_lat, gate_lat = split_cols(h_lat @ p['w_in'])
    a_ctx, b_ctx, f_ctx, gate_ctx = split_cols(h_ctx @ p['w_in'])
    yA_lat, yA_ctx = rwkv_mixer(a_lat, a_ctx, p, not last)
    k_c, v_c = mla_kv(b_ctx, p, None, None)
    k_l, v_l = mla_kv(b_lat, p, row, col)
    q_l = mla_q(b_lat, p, row, col)
    yB_lat = blocked_attend(q_l, jnp.concatenate([k_c, k_l], axis=1), jnp.concatenate([v_c, v_l], axis=1))
    out_lat = merge(gate_lat, yA_lat, yB_lat, fourier_mix(f_lat), p)
    if last:
        return out_lat, None
    q_c = mla_q(b_ctx, p, None, None)
    yB_ctx = attend(q_c, k_c, v_c).reshape(B, h_ctx.shape[1], MLA_WIDTH)
    out_ctx = merge(gate_ctx, yA_ctx, yB_ctx, fourier_mix(f_ctx), p)
    return out_lat, out_ctx


def conv_ffn(h, p):
    u = dwconv3(h @ p['w_up'], p['ffn_conv'])
    a, b = jnp.split(u, 2, axis=-1)
    return (jax.nn.silu(a) * b) @ p['w_down']


def layer(x_lat, x_ctx, mod_lat, mod_ctx, p, row, col, last):
    sh1_l, sc1_l, gt1_l, sh2_l, sc2_l, gt2_l = jnp.split(mod_lat, N_MOD, axis=-1)
    sh1_c, sc1_c, gt1_c, sh2_c, sc2_c, gt2_c = jnp.split(mod_ctx, N_MOD, axis=-1)
    h_lat = modnorm(x_lat, p['g_norm1'], sh1_l, sc1_l)
    h_ctx = modnorm(x_ctx, p['g_norm1'], sh1_c, sc1_c)
    m_lat, m_ctx = token_mixing(h_lat, h_ctx, p, row, col, last)
    x_lat = x_lat + gt1_l * m_lat
    x_lat = x_lat + gt2_l * conv_ffn(modnorm(x_lat, p['g_norm2'], sh2_l, sc2_l), p)
    if not last:
        x_ctx = x_ctx + gt1_c * m_ctx
        x_ctx = x_ctx + gt2_c * conv_ffn(modnorm(x_ctx, p['g_norm2'], sh2_c, sc2_c), p)
    return x_lat, x_ctx


def setup_inputs(seed: int = 0) -> dict:
    key = jax.random.key(seed)
    ks = iter(jax.random.split(key, 40))
    nrm = lambda shape, s: s * jax.random.normal(next(ks), shape, jnp.float32)
    conv_base = jnp.array([0.25, 0.5, 0.25], jnp.float32)[:, None]
    return {
        'x': nrm((BATCH, SEQ, D_MODEL), 1.0),
        'c': nrm((BATCH, D_MODEL), 1.0),
        'ctx': nrm((BATCH, CTX_LEN, D_MODEL), 1.0),
        'c_ctx': nrm((D_MODEL,), 1.0),
        'w_ada': nrm((DEPTH, D_MODEL, N_MOD * D_MODEL), 0.5 * D_MODEL ** -0.5),
        'b_ada': nrm((DEPTH, N_MOD * D_MODEL), 0.01),
        'g_norm1': 1.0 + nrm((DEPTH, D_MODEL), 0.02),
        'g_norm2': 1.0 + nrm((DEPTH, D_MODEL), 0.02),
        'w_in': nrm((DEPTH, D_MODEL, IN_COLS), D_MODEL ** -0.5),
        'rw_conv': conv_base[None] + nrm((DEPTH, 3, RW_COLS), 0.1),
        'rw_w0': nrm((DEPTH, 2, RW_WIDTH), 0.5),
        'rw_w_up': nrm((DEPTH, 2, DECAY_LORA, RW_WIDTH), 0.3 * DECAY_LORA ** -0.5),
        'rw_a0': nrm((DEPTH, 2, RW_WIDTH), 0.5),
        'rw_a_up': nrm((DEPTH, 2, ICLR_LORA, RW_WIDTH), 0.5 * ICLR_LORA ** -0.5),
        'rw_g_up': nrm((DEPTH, GATE_LORA, RW_WIDTH), GATE_LORA ** -0.5),
        'rw_k_k': 1.0 + nrm((DEPTH, RW_WIDTH), 0.1),
        'rw_k_a': 1.0 + nrm((DEPTH, RW_WIDTH), 0.1),
        'rw_r_k': nrm((DEPTH, RW_HEADS, RW_HEAD_DIM), 0.1),
        'rw_ln_w': 1.0 + nrm((DEPTH, RW_WIDTH), 0.02),
        'rw_ln_b': nrm((DEPTH, RW_WIDTH), 0.02),
        'mla_q_norm': 1.0 + nrm((DEPTH, Q_LORA), 0.02),
        'w_uq': nrm((DEPTH, Q_LORA, MLA_HEADS * QK_DIM), Q_LORA ** -0.5),
        'mla_kv_norm': 1.0 + nrm((DEPTH, KV_LORA), 0.02),
        'w_ukv': nrm((DEPTH, KV_LORA, MLA_HEADS * (NOPE_DIM + V_DIM)), KV_LORA ** -0.5),
        'q_gain': 1.0 + nrm((DEPTH, QK_DIM), 0.02),
        'k_gain': 1.0 + nrm((DEPTH, QK_DIM), 0.02),
        'w_branch_a': nrm((DEPTH, RW_WIDTH, D_MODEL), RW_WIDTH ** -0.5),
        'w_branch_b': nrm((DEPTH, MLA_WIDTH, D_MODEL), MLA_WIDTH ** -0.5),
        'w_branch_c': nrm((DEPTH, FT_WIDTH, D_MODEL), FT_WIDTH ** -0.5),
        'w_out': nrm((DEPTH, D_MODEL, D_MODEL), D_MODEL ** -0.5),
        'ffn_conv': conv_base[None] + nrm((DEPTH, 3, 2 * D_FF), 0.1),
        'w_up': nrm((DEPTH, D_MODEL, 2 * D_FF), D_MODEL ** -0.5),
        'w_down': nrm((DEPTH, D_FF, D_MODEL), D_FF ** -0.5),
    }


def reference(x, c, ctx, c_ctx, w_ada, b_ada, g_norm1, g_norm2, w_in, rw_conv, rw_w0, rw_w_up, rw_a0,
              rw_a_up, rw_g_up, rw_k_k, rw_k_a, rw_r_k, rw_ln_w, rw_ln_b, mla_q_norm, w_uq, mla_kv_norm,
              w_ukv, q_gain, k_gain, w_branch_a, w_branch_b, w_branch_c, w_out, ffn_conv, w_up, w_down):
    L = x.shape[1]
    rows = L // GRID_W
    row = jnp.repeat(jnp.arange(rows, dtype=jnp.int32), GRID_W)
    col = jnp.tile(jnp.arange(GRID_W, dtype=jnp.int32), rows)
    x_lat, x_ctx = x, ctx
    for l in range(DEPTH):
        p = dict(g_norm1=g_norm1[l], g_norm2=g_norm2[l], w_in=w_in[l], rw_conv=rw_conv[l],
                 rw_w0=rw_w0[l], rw_w_up=rw_w_up[l], rw_a0=rw_a0[l], rw_a_up=rw_a_up[l],
                 rw_g_up=rw_g_up[l], rw_k_k=rw_k_k[l], rw_k_a=rw_k_a[l], rw_r_k=rw_r_k[l],
                 rw_ln_w=rw_ln_w[l], rw_ln_b=rw_ln_b[l], mla_q_norm=mla_q_norm[l], w_uq=w_uq[l],
                 mla_kv_norm=mla_kv_norm[l], w_ukv=w_ukv[l], q_gain=q_gain[l], k_gain=k_gain[l],
                 w_branch_a=w_branch_a[l], w_branch_b=w_branch_b[l], w_branch_c=w_branch_c[l],
                 w_out=w_out[l], ffn_conv=ffn_conv[l], w_up=w_up[l], w_down=w_down[l])
        mod_lat = (jax.nn.silu(c) @ w_ada[l] + b_ada[l])[:, None, :]
        mod_ctx = (jax.nn.silu(c_ctx) @ w_ada[l] + b_ada[l])[None, None, :]
        x_lat, x_ctx = layer(x_lat, x_ctx, mod_lat, mod_ctx, p, row, col, l == DEPTH - 1)
    return x_lat
```

```python
import functools
import math

import numpy as np
import jax
import jax.numpy as jnp
from jax import lax
from jax.experimental import pallas as pl
from jax.experimental.pallas import tpu as pltpu

F32 = jnp.float32
BF16 = jnp.bfloat16

D_MODEL = 1024
GRID_W = 64
NORM_EPS = 1e-6
RW_HEADS = 6
RW_HEAD_DIM = 64
RW_WIDTH = RW_HEADS * RW_HEAD_DIM
DECAY_LORA = 64
ICLR_LORA = 64
GATE_LORA = 128
RW_LN_EPS = 64e-5
MLA_HEADS = 6
Q_LORA = 384
KV_LORA = 256
NOPE_DIM = 64
ROPE_DIM = 32
V_DIM = 64
QK_DIM = NOPE_DIM + ROPE_DIM
ROPE_BASE = 10000.0
FT_GROUPS = 4
FT_GROUP_DIM = 64
FT_WIDTH = FT_GROUPS * FT_GROUP_DIM
D_FF = 2816
N_MOD = 6
RW_COLS = 3 * RW_WIDTH + 2 * DECAY_LORA + 2 * ICLR_LORA + GATE_LORA
MLA_COLS = Q_LORA + KV_LORA + ROPE_DIM

LANES = 128
HEAD_PAD = 128
MLA_PAD = 768
GATE_OFF = 0
RW_OFF = 3 * D_MODEL
MLA_OFF = RW_OFF + RW_COLS
FT_OFF = MLA_OFF + MLA_PAD
IN_PAD = FT_OFF + FT_WIDTH
HALO = 16
CHUNK = 128
VMEM_LIMIT = 48 * 1024 * 1024

NN = (((1,), (0,)), ((), ()))
NT = (((1,), (1,)), ((), ()))


def _pieces(x, n):
    if x.dtype == BF16:
        return [x]
    out = []
    rem = x
    for i in range(n):
        p = rem.astype(BF16)
        out.append(p)
        if i + 1 < n:
            rem = rem - p.astype(F32)
    return out


def _mm(a, b, pa=1, pb=1, dims=NN):
    ap = _pieces(a, pa)
    bp = _pieces(b, pb)
    n = max(len(ap), len(bp))
    acc = None
    for i in reversed(range(len(ap))):
        for j in reversed(range(len(bp))):
            if i + j < n:
                t = lax.dot_general(ap[i], bp[j], dims, preferred_element_type=F32)
                acc = t if acc is None else acc + t
    return acc


def _sigmoid(x):
    return 1.0 / (1.0 + jnp.exp(-x))


def _cparams(sem):
    return pltpu.CompilerParams(dimension_semantics=sem, vmem_limit_bytes=VMEM_LIMIT)


def _row_tile(n, pref):
    t = min(n, pref)
    while n % t:
        t //= 2
    return t


def _mods_kernel(cc_ref, w_ref, b_ref, o_ref):
    cc = cc_ref[...]
    s = cc * _sigmoid(cc)
    o_ref[0] = _mm(s, w_ref[0], 2, 2) + b_ref[0]


def _mods(cc, w_ada, b_ada):
    depth, d, n = w_ada.shape
    tn = 1536
    return pl.pallas_call(
        _mods_kernel,
        grid=(depth, n // tn),
        in_specs=[pl.BlockSpec((8, d), lambda l, j: (0, 0)),
                  pl.BlockSpec((1, d, tn), lambda l, j: (l, 0, j)),
                  pl.BlockSpec((1, 1, tn), lambda l, j: (l, 0, j))],
        out_specs=pl.BlockSpec((1, 8, tn), lambda l, j: (l, 0, j)),
        out_shape=jax.ShapeDtypeStruct((depth, 8, n), F32),
        compiler_params=_cparams(("parallel", "parallel")),
        name="mods",
    )(cc, w_ada, b_ada.reshape(depth, 1, n))


def _modnorm(xv, g, s):
    ms = jnp.mean(xv * xv, axis=-1, keepdims=True)
    return (xv * lax.rsqrt(ms + NORM_EPS)) * g + s


def _fill_normed(h_scr, x_ref, xp_ref, xn_ref, g, s, i, nrt, tm):
    h_scr[HALO:HALO + tm] = _modnorm(x_ref[...], g, s).astype(BF16)
    hp = jnp.where(i > 0, _modnorm(xp_ref[...], g, s), 0.0)
    h_scr[0:HALO] = hp.astype(BF16)
    hn = jnp.where(i < nrt - 1, _modnorm(xn_ref[...], g, s), 0.0)
    h_scr[HALO + tm:2 * HALO + tm] = hn.astype(BF16)


def _conv3(u, cw, tm):
    rows = u.shape[0]
    up = pltpu.roll(u, 1, 0)
    un = pltpu.roll(u, rows - 1, 0)
    out = cw[0:1] * up + cw[1:2] * u + cw[2:3] * un
    return out[HALO:HALO + tm]


def _halo_specs(tm, d, n_rows):
    r = tm // HALO
    last = n_rows // HALO - 1
    return [pl.BlockSpec((tm, d), lambda i, j: (i, 0)),
            pl.BlockSpec((HALO, d), lambda i, j: (jnp.maximum(i * r - 1, 0), 0)),
            pl.BlockSpec((HALO, d), lambda i, j: (jnp.minimum((i + 1) * r, last), 0))]


def _in_kernel(x_ref, xp_ref, xn_ref, g_ref, s_ref, w_ref, cw_ref, o_ref, h_scr, *, tm, nrt):
    i = pl.program_id(0)

    @pl.when(pl.program_id(1) == 0)
    def _():
        _fill_normed(h_scr, x_ref, xp_ref, xn_ref, g_ref[...], s_ref[...], i, nrt, tm)

    u = jnp.dot(h_scr[...], w_ref[...], preferred_element_type=F32)
    o_ref[...] = _conv3(u, cw_ref[...], tm)


def _in_proj(x, gain, shift, w, cw):
    n_rows, d = x.shape
    n = w.shape[1]
    tm = _row_tile(n_rows, 1024)
    tn = 512
    nrt = n_rows // tm
    return pl.pallas_call(
        functools.partial(_in_kernel, tm=tm, nrt=nrt),
        grid=(nrt, n // tn),
        in_specs=_halo_specs(tm, d, n_rows) + [
            pl.BlockSpec((1, d), lambda i, j: (0, 0)),
            pl.BlockSpec((1, d), lambda i, j: (0, 0)),
            pl.BlockSpec((d, tn), lambda i, j: (0, j)),
            pl.BlockSpec((3, tn), lambda i, j: (0, j))],
        out_specs=pl.BlockSpec((tm, tn), lambda i, j: (i, j)),
        out_shape=jax.ShapeDtypeStruct((n_rows, n), F32),
        scratch_shapes=[pltpu.VMEM((tm + 2 * HALO, d), BF16)],
        compiler_params=_cparams(("parallel", "arbitrary")),
        name="in_proj",
    )(x, x, x, gain, shift, w, cw)


def _ffn_kernel(x_ref, xp_ref, xn_ref, g_ref, s_ref, gt_ref, wa_ref, wb_ref, cwa_ref, cwb_ref,
                wd_ref, o_ref, h_scr, acc_scr, *, tm, nrt, nj):
    i = pl.program_id(0)
    j = pl.program_id(1)

    @pl.when(j == 0)
    def _():
        _fill_normed(h_scr, x_ref, xp_ref, xn_ref, g_ref[...], s_ref[...], i, nrt, tm)
        acc_scr[...] = jnp.zeros_like(acc_scr)

    h = h_scr[...]
    a = _conv3(jnp.dot(h, wa_ref[...], preferred_element_type=F32), cwa_ref[...], tm)
    b = _conv3(jnp.dot(h, wb_ref[...], preferred_element_type=F32), cwb_ref[...], tm)
    gated = (a * _sigmoid(a) * b).astype(BF16)
    acc_scr[...] += jnp.dot(gated, wd_ref[...], preferred_element_type=F32)

    @pl.when(j == nj - 1)
    def _():
        o_ref[...] = x_ref[...] + gt_ref[...] * acc_scr[...]


def _ffn(x, gain, shift, gate, w_up, cw, w_down):
    n_rows, d = x.shape
    tm = _row_tile(n_rows, 1024)
    tf = 256
    nj = D_FF // tf
    nrt = n_rows // tm
    vec = pl.BlockSpec((1, d), lambda i, j: (0, 0))
    return pl.pallas_call(
        functools.partial(_ffn_kernel, tm=tm, nrt=nrt, nj=nj),
        grid=(nrt, nj),
        in_specs=_halo_specs(tm, d, n_rows) + [
            vec, vec, vec,
            pl.BlockSpec((d, tf), lambda i, j: (0, j)),
            pl.BlockSpec((d, tf), lambda i, j: (0, j + nj)),
            pl.BlockSpec((3, tf), lambda i, j: (0, j)),
            pl.BlockSpec((3, tf), lambda i, j: (0, j + nj)),
            pl.BlockSpec((tf, d), lambda i, j: (j, 0))],
        out_specs=pl.BlockSpec((tm, d), lambda i, j: (i, 0)),
        out_shape=jax.ShapeDtypeStruct((n_rows, d), F32),
        scratch_shapes=[pltpu.VMEM((tm + 2 * HALO, d), BF16), pltpu.VMEM((tm, d), F32)],
        compiler_params=_cparams(("parallel", "arbitrary")),
        name="ffn",
    )(x, x, x, gain, shift, gate, w_up, w_up, cw, cw, w_down)


def _seg_sum(x, ones_bd):
    return _mm(x, ones_bd, 3, 1)


def _rw_prep_kernel(u_ref, wlora_ref, alora_ref, gup_ref, ones_ref, vec_ref,
                    com_ref, dir_ref, post_ref):
    u = u_ref[...]
    w = RW_WIDTH
    r = u[:, 0:w]
    k = u[:, w:2 * w]
    v = u[:, 2 * w:3 * w]
    wd = u[:, 3 * w:3 * w + 2 * DECAY_LORA]
    ad = u[:, 3 * w + 2 * DECAY_LORA:3 * w + 2 * DECAY_LORA + 2 * ICLR_LORA]
    gd = u[:, 3 * w + 2 * DECAY_LORA + 2 * ICLR_LORA:]
    vec = vec_ref[...]
    k_k, k_a, r_k = vec[0:1], vec[1:2], vec[2:3]
    ones_bd = ones_ref[...]

    kk = k * k_k
    kk = kk * lax.rsqrt(_seg_sum(kk * kk, ones_bd) + 1e-12)
    g = _mm(_sigmoid(gd), gup_ref[...], 2, 2)
    wl = _mm(jnp.tanh(wd), wlora_ref[...], 2, 2)
    al = _mm(ad, alora_ref[...], 2, 2)
    com_ref[0] = r
    com_ref[1] = v
    com_ref[2] = -kk
    kd_sum = None
    for d in range(2):
        w0 = vec[3 + d:4 + d]
        a0 = vec[5 + d:6 + d]
        lw = -math.exp(-0.5) * _sigmoid(w0 + wl[:, d * w:(d + 1) * w])
        a = _sigmoid(a0 + al[:, d * w:(d + 1) * w])
        kd = k * (1.0 + (a - 1.0) * k_a)
        dir_ref[d, 0] = lw
        dir_ref[d, 1] = kd
        dir_ref[d, 2] = kk * a
        kd_sum = kd if kd_sum is None else kd_sum + kd
    post_ref[0] = g
    post_ref[1] = _seg_sum(r * kd_sum * r_k, ones_bd) * v


def _rw_prep(u, col_block, wlora, alora, gup, ones_bd, vec):
    n_rows = u.shape[0]
    tm = _row_tile(n_rows, 512)
    w = RW_WIDTH
    full = lambda a: pl.BlockSpec(a.shape, lambda i: (0,) * a.ndim)
    return pl.pallas_call(
        _rw_prep_kernel,
        grid=(n_rows // tm,),
        in_specs=[pl.BlockSpec((tm, RW_COLS), lambda i: (i, col_block)),
                  full(wlora), full(alora), full(gup), full(ones_bd), full(vec)],
        out_specs=[pl.BlockSpec((3, tm, w), lambda i: (0, i, 0)),
                   pl.BlockSpec((2, 3, tm, w), lambda i: (0, 0, i, 0)),
                   pl.BlockSpec((2, tm, w), lambda i: (0, i, 0))],
        out_shape=[jax.ShapeDtypeStruct((3, n_rows, w), F32),
                   jax.ShapeDtypeStruct((2, 3, n_rows, w), F32),
                   jax.ShapeDtypeStruct((2, n_rows, w), F32)],
        compiler_params=_cparams(("parallel",)),
        name="rw_prep",
    )(u, wlora, alora, gup, ones_bd, vec)


SCAN_PASSES = 2


def _smm(a, b, dims=NN):
    return _mm(a, b, SCAN_PASSES, SCAN_PASSES, dims)


def _scan_dir(com, dr, tri, s_ref, slot, reverse):
    c = CHUNK
    r, v, a = com[0], com[1], com[2]
    lw, kd, b = dr[0], dr[1], dr[2]
    cs = _mm(tri, lw, 1, 3)
    tot = cs[0:1] if reverse else cs[c - 1:c]
    e_pos = jnp.exp(cs)
    e_neg = jnp.exp(-cs)
    e_end = jnp.exp(tot - cs)
    at = a * jnp.exp(cs - lw)
    rt = r * e_pos
    bt = b * e_neg
    kt = kd * e_neg
    bh = b * e_end
    kh = kd * e_end
    p_end = jnp.exp(tot)

    row = lax.broadcasted_iota(jnp.int32, (c, c), 0)
    col = lax.broadcasted_iota(jnp.int32, (c, c), 1)
    before = (col > row) if reverse else (col < row)
    upto = (col >= row) if reverse else (col <= row)
    eye = row == col
    lane = lax.broadcasted_iota(jnp.int32, (c, LANES), 1)
    same_head = (row // RW_HEAD_DIM) == (col // RW_HEAD_DIM)

    ys = []
    for p in range(RW_HEADS // 2):
        sl = slice(p * LANES, (p + 1) * LANES)
        at_p, rt_p, bt_p, kt_p, v_p = at[:, sl], rt[:, sl], bt[:, sl], kt[:, sl], v[:, sl]
        s0 = s_ref[slot, p]
        ah = _smm(at_p, s0, NT)
        rh = _smm(rt_p, s0, NT)
        bk = jnp.concatenate([bt_p, kt_p], axis=0)
        u_parts = []
        y_parts = []
        mats = []
        for h in range(2):
            mh = (lane // RW_HEAD_DIM) == h
            ar = jnp.concatenate([jnp.where(mh, at_p, 0.0), jnp.where(mh, rt_p, 0.0)], axis=0)
            gm = _smm(ar, bk, NT)
            a_ab = jnp.where(before, gm[:c, :c], 0.0)
            a_ak = jnp.where(before, gm[:c, c:], 0.0)
            a_rb = jnp.where(upto, gm[c:, :c], 0.0)
            a_rk = jnp.where(upto, gm[c:, c:], 0.0)
            t_inv = jnp.where(eye, 1.0, a_ab)
            apow = a_ab
            for _ in range(int(math.log2(c)) - 1):
                apow = _smm(apow, apow)
                t_inv = t_inv + _smm(t_inv, apow)
            mats.append((a_ak, a_rb, a_rk, t_inv, mh))
        for a_ak, a_rb, a_rk, t_inv, mh in mats:
            x = _smm(a_ak, v_p)
            u_parts.append(jnp.where(mh, _smm(t_inv, ah + x), 0.0))
        u_p = u_parts[0] + u_parts[1]
        for a_ak, a_rb, a_rk, t_inv, mh in mats:
            y_h = _smm(a_rb, u_p) + _smm(a_rk, v_p)
            y_parts.append(jnp.where(mh, y_h, 0.0))
        ys.append(rh + y_parts[0] + y_parts[1])
        upd = _smm(u_p.T, bh[:, sl]) + _smm(v_p.T, kh[:, sl])
        s_ref[slot, p] = s0 * p_end[:, sl] + jnp.where(same_head, upd, 0.0)
    return jnp.concatenate(ys, axis=1)


def _rw_scan_kernel(comf_ref, dirf_ref, comb_ref, dirb_ref, tri_ref, yf_ref, yb_ref, s_ref):
    @pl.when(pl.program_id(0) == 0)
    def _():
        s_ref[...] = jnp.zeros_like(s_ref)

    yf_ref[...] = _scan_dir(comf_ref[...], dirf_ref[0], tri_ref[0], s_ref, 0, False)
    yb_ref[...] = _scan_dir(comb_ref[...], dirb_ref[0], tri_ref[1], s_ref, 1, True)


def _rw_scan(com, dirs, tri, n_ctx_chunks):
    n_rows = com.shape[1]
    w = RW_WIDTH
    c = CHUNK
    assert c == LANES
    n_chunks = n_rows // c
    nc = n_ctx_chunks
    nl = n_chunks - nc

    def fwd_block(i):
        return jnp.where(i < nc, nl + i, i - nc)

    def bwd_block(i):
        return jnp.where(i < nc, nl + nc - 1 - i, nl - 1 - (i - nc))

    return pl.pallas_call(
        _rw_scan_kernel,
        grid=(n_chunks,),
        in_specs=[pl.BlockSpec((3, c, w), lambda i: (0, fwd_block(i), 0)),
                  pl.BlockSpec((1, 3, c, w), lambda i: (0, 0, fwd_block(i), 0)),
                  pl.BlockSpec((3, c, w), lambda i: (0, bwd_block(i), 0)),
                  pl.BlockSpec((1, 3, c, w), lambda i: (1, 0, bwd_block(i), 0)),
                  pl.BlockSpec((2, c, c), lambda i: (0, 0, 0))],
        out_specs=[pl.BlockSpec((c, w), lambda i: (fwd_block(i), 0)),
                   pl.BlockSpec((c, w), lambda i: (bwd_block(i), 0))],
        out_shape=[jax.ShapeDtypeStruct((n_rows, w), F32)] * 2,
        scratch_shapes=[pltpu.VMEM((2, RW_HEADS // 2, LANES, LANES), F32)],
        compiler_params=_cparams(("arbitrary",)),
        name="rw_scan",
    )(com, dirs, com, dirs, tri)


def _rw_post_kernel(yf_ref, yb_ref, post_ref, avg_ref, vec_ref, o_ref):
    y = yf_ref[...] + yb_ref[...]
    avg = avg_ref[...]
    mu = _mm(y, avg, 3, 1)
    yc = y - mu
    var = _mm(yc * yc, avg, 3, 1)
    yn = yc * lax.rsqrt(var + RW_LN_EPS)
    vec = vec_ref[...]
    o_ref[...] = (yn * vec[0:1] + vec[1:2] + post_ref[1]) * post_ref[0]


def _rw_post(yf, yb, post, avg_bd, vec, row_off):
    n_rows = post.shape[1]
    w = RW_WIDTH
    tm = _row_tile(n_rows, 512)
    assert row_off % tm == 0
    ob = row_off // tm
    full = lambda a: pl.BlockSpec(a.shape, lambda i: (0,) * a.ndim)
    return pl.pallas_call(
        _rw_post_kernel,
        grid=(n_rows // tm,),
        in_specs=[pl.BlockSpec((tm, w), lambda i: (i + ob, 0)),
                  pl.BlockSpec((tm, w), lambda i: (i + ob, 0)),
                  pl.BlockSpec((2, tm, w), lambda i: (0, i, 0)),
                  full(avg_bd), full(vec)],
        out_specs=pl.BlockSpec((tm, w), lambda i: (i, 0)),
        out_shape=jax.ShapeDtypeStruct((n_rows, w), F32),
        compiler_params=_cparams(("parallel",)),
        name="rw_post",
    )(yf, yb, post, avg_bd, vec)


def _rms(x, n):
    return x * lax.rsqrt(jnp.sum(x * x, axis=-1, keepdims=True) * (1.0 / n) + NORM_EPS)


def _mla_prep_kernel(*refs, rope):
    if rope:
        u_ref, wq_ref, wk_ref, wv_ref, vq_ref, vkv_ref, gains_ref, rope_ref, q_ref, k_ref, v_ref = refs
    else:
        u_ref, wq_ref, wk_ref, wv_ref, vq_ref, vkv_ref, gains_ref, q_ref, k_ref, v_ref = refs
    u = u_ref[...]
    cq = _rms(u[:, :Q_LORA], Q_LORA) * vq_ref[...]
    q = jnp.dot(cq.astype(BF16), wq_ref[...], preferred_element_type=F32)
    ckv = _rms(u[:, Q_LORA:Q_LORA + KV_LORA], KV_LORA) * vkv_ref[...]
    ckv = ckv.astype(BF16)
    kn = jnp.dot(ckv, wk_ref[...], preferred_element_type=F32)
    vv = jnp.dot(ckv, wv_ref[...], preferred_element_type=F32)
    kr = pltpu.roll(u[:, Q_LORA + KV_LORA:], NOPE_DIM, 1)
    gains = gains_ref[...]
    lane = lax.broadcasted_iota(jnp.int32, (1, HEAD_PAD), 1)
    ones_col = (lane == V_DIM).astype(F32)

    def rot(t):
        if not rope:
            return t
        tab = rope_ref[...]
        half = ROPE_DIM // 4
        return (t * tab[0] + pltpu.roll(t, HEAD_PAD - half, 1) * tab[1]
                + pltpu.roll(t, half, 1) * tab[2])

    for h in range(MLA_HEADS):
        sl = slice(h * HEAD_PAD, (h + 1) * HEAD_PAD)
        qh = _rms(q[:, sl], QK_DIM) * gains[0:1]
        kh = _rms(kn[:, sl] + kr, QK_DIM) * gains[1:2]
        q_ref[h] = rot(qh).astype(BF16)
        k_ref[h] = rot(kh).astype(BF16)
        v_ref[h] = (vv[:, sl] + ones_col).astype(BF16)


def _mla_prep(u, col_block, wq, wk, wv, vq, vkv, gains, rope_tab):
    n_rows = u.shape[0]
    tm = _row_tile(n_rows, 512)
    rope = rope_tab is not None
    full = lambda a: pl.BlockSpec(a.shape, lambda i: (0,) * a.ndim)
    in_specs = [pl.BlockSpec((tm, MLA_PAD), lambda i: (i, col_block)),
                full(wq), full(wk), full(wv), full(vq), full(vkv), full(gains)]
    args = [u, wq, wk, wv, vq, vkv, gains]
    if rope:
        in_specs.append(pl.BlockSpec((3, tm, HEAD_PAD), lambda i: (0, i, 0)))
        args.append(rope_tab)
    hs = pl.BlockSpec((MLA_HEADS, tm, HEAD_PAD), lambda i: (0, i, 0))
    shp = jax.ShapeDtypeStruct((MLA_HEADS, n_rows, HEAD_PAD), BF16)
    return pl.pallas_call(
        functools.partial(_mla_prep_kernel, rope=rope),
        grid=(n_rows // tm,),
        in_specs=in_specs,
        out_specs=[hs, hs, hs],
        out_shape=[shp, shp, shp],
        compiler_params=_cparams(("parallel",)),
        name="mla_prep",
    )(*args)


def _attn_kernel(q_ref, k_ref, v_ref, o_ref, m_scr, acc_scr, *, nk):
    kv = pl.program_id(2)

    @pl.when(kv == 0)
    def _():
        m_scr[...] = jnp.full_like(m_scr, -jnp.inf)
        acc_scr[...] = jnp.zeros_like(acc_scr)

    s = lax.dot_general(q_ref[0], k_ref[0], NT, preferred_element_type=F32)
    tk = s.shape[1]
    m_prev = m_scr[...]
    m_cur = s[:, 0:LANES]
    for cblk in range(1, tk // LANES):
        m_cur = jnp.maximum(m_cur, s[:, cblk * LANES:(cblk + 1) * LANES])
    m_new = jnp.maximum(m_prev, jnp.max(m_cur, axis=-1, keepdims=True))
    alpha = jnp.exp2(m_prev - m_new)
    p = jnp.concatenate(
        [jnp.exp2(s[:, cblk * LANES:(cblk + 1) * LANES] - m_new).astype(BF16)
         for cblk in range(tk // LANES)], axis=1)
    acc_scr[...] = alpha * acc_scr[...] + jnp.dot(p, v_ref[0], preferred_element_type=F32)
    m_scr[...] = m_new

    @pl.when(kv == nk - 1)
    def _():
        acc = acc_scr[...]
        lane = lax.broadcasted_iota(jnp.int32, acc.shape, 1)
        denom = jnp.sum(jnp.where(lane == V_DIM, acc, 0.0), axis=-1, keepdims=True)
        o_ref[...] = acc / denom


def _attention(q, k, v):
    h, lq, dp = q.shape
    lk = k.shape[1]
    tq = _row_tile(lq, 512)
    tk = lk
    for cand in (1280, 1024, 768, 512, 384, 256, 128):
        if lk % cand == 0:
            tk = cand
            break
    nk = lk // tk
    return pl.pallas_call(
        functools.partial(_attn_kernel, nk=nk),
        grid=(h, lq // tq, nk),
        in_specs=[pl.BlockSpec((1, tq, dp), lambda hh, i, j: (hh, i, 0)),
                  pl.BlockSpec((1, tk, dp), lambda hh, i, j: (hh, j, 0)),
                  pl.BlockSpec((1, tk, dp), lambda hh, i, j: (hh, j, 0))],
        out_specs=pl.BlockSpec((tq, dp), lambda hh, i, j: (i, hh)),
        out_shape=jax.ShapeDtypeStruct((lq, h * dp), F32),
        scratch_shapes=[pltpu.VMEM((tq, LANES), F32), pltpu.VMEM((tq, dp), F32)],
        compiler_params=_cparams(("parallel", "parallel", "arbitrary")),
        name="attn",
    )(q, k, v)


def _ft_chan_kernel(u_ref, m_ref, o_ref):
    o_ref[...] = _mm(u_ref[...], m_ref[...], 2, 2)


def _ft_chan(u, col_block, mat):
    n_rows = u.shape[0]
    tm = _row_tile(n_rows, 512)
    return pl.pallas_call(
        _ft_chan_kernel,
        grid=(n_rows // tm,),
        in_specs=[pl.BlockSpec((tm, FT_WIDTH), lambda i: (i, col_block)),
                  pl.BlockSpec(mat.shape, lambda i: (0, 0))],
        out_specs=pl.BlockSpec((tm, 2 * FT_WIDTH), lambda i: (i, 0)),
        out_shape=jax.ShapeDtypeStruct((n_rows, 2 * FT_WIDTH), F32),
        compiler_params=_cparams(("parallel",)),
        name="ft_chan",
    )(u, mat)


def _ft_s1_kernel(w_ref, cs_ref, tw_ref, o_ref, *, l1, g):
    z = _mm(cs_ref[...], w_ref[...], 2, 2)
    fw = FT_WIDTH
    for t in range(g):
        zr = z[:, t * 2 * fw:t * 2 * fw + fw]
        zi = z[:, t * 2 * fw + fw:(t + 1) * 2 * fw]
        a_re = zr[:l1] + zi[l1:]
        a_im = zi[:l1] - zr[l1:]
        tc = jnp.concatenate([tw_ref[0, t]] * (fw // LANES), axis=1)
        ts = jnp.concatenate([tw_ref[1, t]] * (fw // LANES), axis=1)
        o_ref[:, t * 2 * fw:t * 2 * fw + fw] = a_re * tc + a_im * ts
        o_ref[:, t * 2 * fw + fw:(t + 1) * 2 * fw] = a_im * tc - a_re * ts


def _ft_s1(w2d, cs, tw, l1, l2):
    g = min(8, l2)
    blk = g * 2 * FT_WIDTH
    return pl.pallas_call(
        functools.partial(_ft_s1_kernel, l1=l1, g=g),
        grid=(l2 // g,),
        in_specs=[pl.BlockSpec((l1, blk), lambda i: (0, i)),
                  pl.BlockSpec(cs.shape, lambda i: (0, 0)),
                  pl.BlockSpec((2, g, l1, LANES), lambda i: (0, i, 0, 0))],
        out_specs=pl.BlockSpec((l1, blk), lambda i: (0, i)),
        out_shape=jax.ShapeDtypeStruct((l1, l2 * 2 * FT_WIDTH), F32),
        compiler_params=_cparams(("parallel",)),
        name="ft_s1",
    )(w2d, cs, tw)


def _ft_s2_kernel(b_ref, cs_ref, o_ref, *, g, scale):
    fw = FT_WIDTH
    cmat = cs_ref[0]
    smat = cs_ref[1]
    for kk in range(g):
        blk = b_ref[kk]
        out = _mm(cmat, blk[:, :fw], 2, 2) + _mm(smat, blk[:, fw:], 2, 2)
        o_ref[:, kk * fw:(kk + 1) * fw] = out * scale


def _ft_s2(b3d, cs2, l1, l2, scale):
    g = min(8, l1)
    return pl.pallas_call(
        functools.partial(_ft_s2_kernel, g=g, scale=scale),
        grid=(l1 // g,),
        in_specs=[pl.BlockSpec((g, l2, 2 * FT_WIDTH), lambda i: (i, 0, 0)),
                  pl.BlockSpec(cs2.shape, lambda i: (0, 0, 0))],
        out_specs=pl.BlockSpec((l2, g * FT_WIDTH), lambda i: (0, i)),
        out_shape=jax.ShapeDtypeStruct((l2, l1 * FT_WIDTH), F32),
        compiler_params=_cparams(("parallel",)),
        name="ft_s2",
    )(b3d, cs2)


def _ft_tables(n):
    l1 = 1 << (int(math.log2(n)) // 2)
    l2 = n // l1
    assert l1 * l2 == n and l1 % 8 == 0 and l2 % 8 == 0
    k1 = np.arange(l1)
    ang1 = 2.0 * np.pi * np.outer(k1, np.arange(l1)) / l1
    cs1 = np.concatenate([np.cos(ang1), np.sin(ang1)], axis=0)
    angt = 2.0 * np.pi * np.outer(np.arange(l2), k1) / n
    tw = np.stack([np.cos(angt), np.sin(angt)], axis=0)
    tw = np.broadcast_to(tw[..., None], (2, l2, l1, LANES))
    ang2 = 2.0 * np.pi * np.outer(np.arange(l2), np.arange(l2)) / l2
    cs2 = np.stack([np.cos(ang2), np.sin(ang2)], axis=0)
    f = lambda a: jnp.asarray(np.ascontiguousarray(a), F32)
    return l1, l2, f(cs1), f(tw), f(cs2)


def _chan_dft_matrix():
    n = FT_GROUP_DIM
    ang = 2.0 * np.pi * np.outer(np.arange(n), np.arange(n)) / n
    eye = np.eye(FT_GROUPS)
    mat = np.concatenate([np.kron(eye, np.cos(ang)), -np.kron(eye, np.sin(ang))], axis=1)
    return jnp.asarray(mat, F32)


def _fourier_mix(u, col_block, chan_mat):
    n = u.shape[0]
    l1, l2, cs1, tw, cs2 = _ft_tables(n)
    wc = _ft_chan(u, col_block, chan_mat)
    b = _ft_s1(wc.reshape(l1, l2 * 2 * FT_WIDTH), cs1, tw, l1, l2)
    scale = 1.0 / math.sqrt(n * FT_GROUP_DIM)
    out = _ft_s2(b.reshape(l1, l2, 2 * FT_WIDTH), cs2, l1, l2, scale)
    return out.reshape(n, FT_WIDTH)


def _merge_kernel(ya_ref, yb_ref, yc_ref, ga_ref, gb_ref, gc_ref, x_ref, gt_ref,
                  wa_ref, wb_ref, wc_ref, wo_ref, o_ref):
    def branch(y_ref, w_ref, g_ref):
        t = jnp.dot(y_ref[...].astype(BF16), w_ref[...], preferred_element_type=F32)
        return _sigmoid(g_ref[...]) * t

    m = branch(ya_ref, wa_ref, ga_ref) + branch(yb_ref, wb_ref, gb_ref) + branch(yc_ref, wc_ref, gc_ref)
    out = jnp.dot(m.astype(BF16), wo_ref[...], preferred_element_type=F32)
    o_ref[...] = x_ref[...] + gt_ref[...] * out


def _merge(ya, yb, yc, u, x, gt, wa, wb, wc, wo):
    n_rows, d = x.shape
    tm = _row_tile(n_rows, 512)
    full = lambda a: pl.BlockSpec(a.shape, lambda i: (0,) * a.ndim)
    rows = lambda a: pl.BlockSpec((tm, a.shape[1]), lambda i: (i, 0))
    gate = lambda b: pl.BlockSpec((tm, d), lambda i: (i, b))
    return pl.pallas_call(
        _merge_kernel,
        grid=(n_rows // tm,),
        in_specs=[rows(ya), rows(yb), rows(yc), gate(0), gate(1), gate(2), rows(x), full(gt),
                  full(wa), full(wb), full(wc), full(wo)],
        out_specs=pl.BlockSpec((tm, d), lambda i: (i, 0)),
        out_shape=jax.ShapeDtypeStruct((n_rows, d), F32),
        compiler_params=_cparams(("parallel",)),
        name="merge",
    )(ya, yb, yc, u, u, u, x, gt, wa, wb, wc, wo)


def _pad_heads(w, src_cols, width):
    k = w.shape[0]
    wh = w.reshape(k, MLA_HEADS, width)[:, :, src_cols]
    wh = jnp.pad(wh, ((0, 0), (0, 0), (0, HEAD_PAD - wh.shape[2])))
    return wh.reshape(k, MLA_HEADS * HEAD_PAD)


def _block_diag2(a, b):
    z01 = jnp.zeros((a.shape[0], b.shape[1]), a.dtype)
    z10 = jnp.zeros((b.shape[0], a.shape[1]), a.dtype)
    return jnp.concatenate([jnp.concatenate([a, z01], axis=1), jnp.concatenate([z10, b], axis=1)], axis=0)


def _rope_tables(n):
    half = ROPE_DIM // 2
    inv = ROPE_BASE ** (-np.arange(0, half, 2, dtype=np.float32) / half)
    t = np.arange(n)
    row = (t // GRID_W).astype(np.float32)
    col = (t % GRID_W).astype(np.float32)
    q = half // 2
    cos = np.ones((n, HEAD_PAD), np.float32)
    sin_a = np.zeros((n, HEAD_PAD), np.float32)
    sin_b = np.zeros((n, HEAD_PAD), np.float32)
    for base, pos in ((NOPE_DIM, row), (NOPE_DIM + half, col)):
        ang = pos[:, None] * inv[None, :]
        c, s = np.cos(ang), np.sin(ang)
        cos[:, base:base + q] = c
        cos[:, base + q:base + 2 * q] = c
        sin_a[:, base:base + q] = -s
        sin_b[:, base + q:base + 2 * q] = s
    return jnp.asarray(np.stack([cos, sin_a, sin_b], axis=0))


def _layer_weights(l, w):
    w_in = w["w_in"][l]
    o1 = RW_COLS
    o2 = o1 + MLA_COLS
    o3 = o2 + FT_WIDTH
    w_in_p = jnp.concatenate(
        [w_in[:, o3:], w_in[:, :o1], w_in[:, o1:o2],
         jnp.zeros((D_MODEL, MLA_PAD - MLA_COLS), F32), w_in[:, o2:o3]], axis=1).astype(BF16)
    ident = jnp.tile(jnp.array([[0.0], [1.0], [0.0]], F32), (1, IN_PAD))
    cw_in = ident.at[:, RW_OFF:RW_OFF + RW_COLS].set(w["rw_conv"][l])
    pad_gain = lambda g: jnp.pad(g, (0, HEAD_PAD - QK_DIM))
    wb = w["w_branch_b"][l].reshape(MLA_HEADS, V_DIM, D_MODEL)
    wb = jnp.pad(wb, ((0, 0), (0, HEAD_PAD - V_DIM), (0, 0))).reshape(MLA_HEADS * HEAD_PAD, D_MODEL)
    q_scale = math.log2(math.e) * QK_DIM ** -0.5
    return dict(
        w_in=w_in_p, cw_in=cw_in,
        wlora=_block_diag2(w["rw_w_up"][l, 0], w["rw_w_up"][l, 1]),
        alora=_block_diag2(w["rw_a_up"][l, 0], w["rw_a_up"][l, 1]),
        gup=w["rw_g_up"][l],
        rw_vec=jnp.stack([w["rw_k_k"][l], w["rw_k_a"][l], w["rw_r_k"][l].reshape(-1),
                          w["rw_w0"][l, 0], w["rw_w0"][l, 1], w["rw_a0"][l, 0], w["rw_a0"][l, 1],
                          jnp.zeros((RW_WIDTH,), F32)], axis=0),
        ln_vec=jnp.stack([w["rw_ln_w"][l], w["rw_ln_b"][l]], axis=0),
        wq=_pad_heads(w["w_uq"][l], slice(0, QK_DIM), QK_DIM).astype(BF16),
        wk=_pad_heads(w["w_ukv"][l], slice(0, NOPE_DIM), NOPE_DIM + V_DIM).astype(BF16),
        wv=_pad_heads(w["w_ukv"][l], slice(NOPE_DIM, NOPE_DIM + V_DIM), NOPE_DIM + V_DIM).astype(BF16),
        vq=w["mla_q_norm"][l][None], vkv=w["mla_kv_norm"][l][None],
        gains_lat=jnp.stack([pad_gain(w["q_gain"][l]) * q_scale, pad_gain(w["k_gain"][l])], axis=0),
        wa=w["w_branch_a"][l].astype(BF16), wb=wb.astype(BF16), wc=w["w_branch_c"][l].astype(BF16),
        wo=w["w_out"][l].astype(BF16),
        w_up=w["w_up"][l].astype(BF16), cw_ffn=w["ffn_conv"][l], w_down=w["w_down"][l].astype(BF16),
    )


def _forward(x, c, ctx, c_ctx, w):
    depth = w["w_ada"].shape[0]
    x_lat = x[0]
    x_ctx = ctx[0]
    n_lat = x_lat.shape[0]
    n_ctx = x_ctx.shape[0]
    assert n_lat % CHUNK == 0 and n_ctx % CHUNK == 0 and n_lat % GRID_W == 0

    cc = jnp.zeros((8, D_MODEL), F32).at[0].set(c[0]).at[1].set(c_ctx)
    mods = _mods(cc, w["w_ada"], w["b_ada"])

    head = np.arange(RW_WIDTH) // RW_HEAD_DIM
    ones_bd = jnp.asarray((head[:, None] == head[None, :]).astype(np.float32))
    avg_bd = ones_bd * (1.0 / RW_HEAD_DIM)
    ti = np.arange(CHUNK)
    tri = jnp.asarray(np.stack([ti[None, :] <= ti[:, None], ti[None, :] >= ti[:, None]]).astype(np.float32),
                      BF16)
    chan_mat = _chan_dft_matrix()
    rope_tab = _rope_tables(n_lat)

    for l in range(depth):
        last = l == depth - 1
        p = _layer_weights(l, w)
        streams = [(x_lat, mods[l, 0], rope_tab)]
        if True:
            streams.append((x_ctx, mods[l, 1], None))
        us, preps, qkv = [], [], []
        for xs, mod, rt in streams:
            sh1, sc1 = mod[0:D_MODEL][None], mod[D_MODEL:2 * D_MODEL][None]
            gain1 = w["g_norm1"][l][None] * (1.0 + sc1)
            u = _in_proj(xs, gain1, sh1, p["w_in"], p["cw_in"])
            us.append(u)
            preps.append(_rw_prep(u, RW_OFF // RW_COLS, p["wlora"], p["alora"], p["gup"], ones_bd, p["rw_vec"]))
            qkv.append(_mla_prep(u, MLA_OFF // MLA_PAD, p["wq"], p["wk"], p["wv"], p["vq"], p["vkv"],
                                 p["gains_lat"], rt))
        com = jnp.concatenate([preps[0][0], preps[1][0]], axis=1)
        dirs = jnp.concatenate([preps[0][1], preps[1][1]], axis=2)
        yf, yb = _rw_scan(com, dirs, tri, n_ctx // CHUNK)
        k_all = jnp.concatenate([qkv[0][1], qkv[1][1]], axis=1)
        v_all = jnp.concatenate([qkv[0][2], qkv[1][2]], axis=1)

        new = []
        for si, (xs, mod, rt) in enumerate(streams):
            if si == 1 and last:
                continue
            row_off = 0 if si == 0 else n_lat
            ya = _rw_post(yf, yb, preps[si][2], avg_bd, p["ln_vec"], row_off)
            if si == 0:
                yb_att = _attention(qkv[0][0], k_all, v_all)
            else:
                yb_att = _attention(qkv[1][0], qkv[1][1], qkv[1][2])
            yc = _fourier_mix(us[si], FT_OFF // FT_WIDTH, chan_mat)
            gt1 = mod[2 * D_MODEL:3 * D_MODEL][None]
            x1 = _merge(ya, yb_att, yc, us[si], xs, gt1, p["wa"], p["wb"], p["wc"], p["wo"])
            sh2, sc2, gt2 = (mod[3 * D_MODEL:4 * D_MODEL][None], mod[4 * D_MODEL:5 * D_MODEL][None],
                             mod[5 * D_MODEL:6 * D_MODEL][None])
            gain2 = w["g_norm2"][l][None] * (1.0 + sc2)
            new.append(_ffn(x1, gain2, sh2, gt2, p["w_up"], p["cw_ffn"], p["w_down"]))
        x_lat = new[0]
        if not last:
            x_ctx = new[1]
    return x_lat[None]


def kernel(x, c, ctx, c_ctx, w_ada, b_ada, g_norm1, g_norm2, w_in, rw_conv, rw_w0, rw_w_up, rw_a0, rw_a_up,
           rw_g_up, rw_k_k, rw_k_a, rw_r_k, rw_ln_w, rw_ln_b, mla_q_norm, w_uq, mla_kv_norm, w_ukv, q_gain,
           k_gain, w_branch_a, w_branch_b, w_branch_c, w_out, ffn_conv, w_up, w_down):
    w = dict(w_ada=w_ada, b_ada=b_ada, g_norm1=g_norm1, g_norm2=g_norm2, w_in=w_in, rw_conv=rw_conv,
             rw_w0=rw_w0, rw_w_up=rw_w_up, rw_a0=rw_a0, rw_a_up=rw_a_up, rw_g_up=rw_g_up, rw_k_k=rw_k_k,
             rw_k_a=rw_k_a, rw_r_k=rw_r_k, rw_ln_w=rw_ln_w, rw_ln_b=rw_ln_b, mla_q_norm=mla_q_norm,
             w_uq=w_uq, mla_kv_norm=mla_kv_norm, w_ukv=w_ukv, q_gain=q_gain, k_gain=k_gain,
             w_branch_a=w_branch_a, w_branch_b=w_branch_b, w_branch_c=w_branch_c, w_out=w_out,
             ffn_conv=ffn_conv, w_up=w_up, w_down=w_down)
    return _forward(x, c, ctx, c_ctx, w)
```

```python
import functools
import math

import numpy as np
import jax
import jax.numpy as jnp
from jax import lax
from jax.experimental import pallas as pl
from jax.experimental.pallas import tpu as pltpu

F32 = jnp.float32
BF16 = jnp.bfloat16

D_MODEL = 1024
GRID_W = 64
NORM_EPS = 1e-6
RW_HEADS = 6
RW_HEAD_DIM = 64
RW_WIDTH = RW_HEADS * RW_HEAD_DIM
DECAY_LORA = 64
ICLR_LORA = 64
GATE_LORA = 128
RW_LN_EPS = 64e-5
MLA_HEADS = 6
Q_LORA = 384
KV_LORA = 256
NOPE_DIM = 64
ROPE_DIM = 32
V_DIM = 64
QK_DIM = NOPE_DIM + ROPE_DIM
ROPE_BASE = 10000.0
FT_GROUPS = 4
FT_GROUP_DIM = 64
FT_WIDTH = FT_GROUPS * FT_GROUP_DIM
D_FF = 2816
N_MOD = 6
RW_COLS = 3 * RW_WIDTH + 2 * DECAY_LORA + 2 * ICLR_LORA + GATE_LORA
MLA_COLS = Q_LORA + KV_LORA + ROPE_DIM

LANES = 128
HEAD_PAD = 128
MLA_PAD = 768
GATE_OFF = 0
RW_OFF = 3 * D_MODEL
MLA_OFF = RW_OFF + RW_COLS
FT_OFF = MLA_OFF + MLA_PAD
IN_PAD = FT_OFF + FT_WIDTH
HALO = 16
CHUNK = 128
VMEM_LIMIT = 48 * 1024 * 1024

NN = (((1,), (0,)), ((), ()))
NT = (((1,), (1,)), ((), ()))


def _pieces(x, n):
    if x.dtype == BF16:
        return [x]
    out = []
    rem = x
    for i in range(n):
        p = rem.astype(BF16)
        out.append(p)
        if i + 1 < n:
            rem = rem - p.astype(F32)
    return out


def _mm(a, b, pa=1, pb=1, dims=NN):
    ap = _pieces(a, pa)
    bp = _pieces(b, pb)
    n = max(len(ap), len(bp))
    acc = None
    for i in reversed(range(len(ap))):
        for j in reversed(range(len(bp))):
            if i + j < n:
                t = lax.dot_general(ap[i], bp[j], dims, preferred_element_type=F32)
                acc = t if acc is None else acc + t
    return acc


def _sigmoid(x):
    return 1.0 / (1.0 + jnp.exp(-x))


def _cparams(sem):
    return pltpu.CompilerParams(dimension_semantics=sem, vmem_limit_bytes=VMEM_LIMIT)


def _row_tile(n, pref):
    t = min(n, pref)
    while n % t:
        t //= 2
    return t


def _mods_kernel(cc_ref, w_ref, b_ref, o_ref):
    cc = cc_ref[...]
    s = cc * _sigmoid(cc)
    o_ref[0] = _mm(s, w_ref[0], 2, 2) + b_ref[0]


def _mods(cc, w_ada, b_ada):
    depth, d, n = w_ada.shape
    tn = 1536
    return pl.pallas_call(
        _mods_kernel,
        grid=(depth, n // tn),
        in_specs=[pl.BlockSpec((8, d), lambda l, j: (0, 0)),
                  pl.BlockSpec((1, d, tn), lambda l, j: (l, 0, j)),
                  pl.BlockSpec((1, 1, tn), lambda l, j: (l, 0, j))],
        out_specs=pl.BlockSpec((1, 8, tn), lambda l, j: (l, 0, j)),
        out_shape=jax.ShapeDtypeStruct((depth, 8, n), F32),
        compiler_params=_cparams(("parallel", "parallel")),
        name="mods",
    )(cc, w_ada, b_ada.reshape(depth, 1, n))


def _modnorm(xv, g, s):
    ms = jnp.mean(xv * xv, axis=-1, keepdims=True)
    return (xv * lax.rsqrt(ms + NORM_EPS)) * g + s


def _fill_normed(h_scr, x_ref, xp_ref, xn_ref, g, s, i, nrt, tm):
    h_scr[HALO:HALO + tm] = _modnorm(x_ref[...], g, s).astype(BF16)
    hp = jnp.where(i > 0, _modnorm(xp_ref[...], g, s), 0.0)
    h_scr[0:HALO] = hp.astype(BF16)
    hn = jnp.where(i < nrt - 1, _modnorm(xn_ref[...], g, s), 0.0)
    h_scr[HALO + tm:2 * HALO + tm] = hn.astype(BF16)


def _conv3(u, cw, tm):
    rows = u.shape[0]
    up = pltpu.roll(u, 1, 0)
    un = pltpu.roll(u, rows - 1, 0)
    out = cw[0:1] * up + cw[1:2] * u + cw[2:3] * un
    return out[HALO:HALO + tm]


def _halo_specs(tm, d, n_rows):
    r = tm // HALO
    last = n_rows // HALO - 1
    return [pl.BlockSpec((tm, d), lambda i, j: (i, 0)),
            pl.BlockSpec((HALO, d), lambda i, j: (jnp.maximum(i * r - 1, 0), 0)),
            pl.BlockSpec((HALO, d), lambda i, j: (jnp.minimum((i + 1) * r, last), 0))]


def _in_kernel(x_ref, xp_ref, xn_ref, g_ref, s_ref, w_ref, cw_ref, o_ref, h_scr, *, tm, nrt):
    i = pl.program_id(0)

    @pl.when(pl.program_id(1) == 0)
    def _():
        _fill_normed(h_scr, x_ref, xp_ref, xn_ref, g_ref[...], s_ref[...], i, nrt, tm)

    u = jnp.dot(h_scr[...], w_ref[...], preferred_element_type=F32)
    o_ref[...] = _conv3(u, cw_ref[...], tm)


def _in_proj(x, gain, shift, w, cw):
    n_rows, d = x.shape
    n = w.shape[1]
    tm = _row_tile(n_rows, 1024)
    tn = 512
    nrt = n_rows // tm
    return pl.pallas_call(
        functools.partial(_in_kernel, tm=tm, nrt=nrt),
        grid=(nrt, n // tn),
        in_specs=_halo_specs(tm, d, n_rows) + [
            pl.BlockSpec((1, d), lambda i, j: (0, 0)),
            pl.BlockSpec((1, d), lambda i, j: (0, 0)),
            pl.BlockSpec((d, tn), lambda i, j: (0, j)),
            pl.BlockSpec((3, tn), lambda i, j: (0, j))],
        out_specs=pl.BlockSpec((tm, tn), lambda i, j: (i, j)),
        out_shape=jax.ShapeDtypeStruct((n_rows, n), F32),
        scratch_shapes=[pltpu.VMEM((tm + 2 * HALO, d), BF16)],
        compiler_params=_cparams(("parallel", "arbitrary")),
        name="in_proj",
    )(x, x, x, gain, shift, w, cw)


def _ffn_kernel(x_ref, xp_ref, xn_ref, g_ref, s_ref, gt_ref, wa_ref, wb_ref, cwa_ref, cwb_ref,
                wd_ref, o_ref, h_scr, acc_scr, *, tm, nrt, nj):
    i = pl.program_id(0)
    j = pl.program_id(1)

    @pl.when(j == 0)
    def _():
        _fill_normed(h_scr, x_ref, xp_ref, xn_ref, g_ref[...], s_ref[...], i, nrt, tm)
        acc_scr[...] = jnp.zeros_like(acc_scr)

    h = h_scr[...]
    a = _conv3(jnp.dot(h, wa_ref[...], preferred_element_type=F32), cwa_ref[...], tm)
    b = _conv3(jnp.dot(h, wb_ref[...], preferred_element_type=F32), cwb_ref[...], tm)
    gated = (a * _sigmoid(a) * b).astype(BF16)
    acc_scr[...] += jnp.dot(gated, wd_ref[...], preferred_element_type=F32)

    @pl.when(j == nj - 1)
    def _():
        o_ref[...] = x_ref[...] + gt_ref[...] * acc_scr[...]


def _ffn(x, gain, shift, gate, w_up, cw, w_down):
    n_rows, d = x.shape
    tm = _row_tile(n_rows, 1024)
    tf = 256
    nj = D_FF // tf
    nrt = n_rows // tm
    vec = pl.BlockSpec((1, d), lambda i, j: (0, 0))
    return pl.pallas_call(
        functools.partial(_ffn_kernel, tm=tm, nrt=nrt, nj=nj),
        grid=(nrt, nj),
        in_specs=_halo_specs(tm, d, n_rows) + [
            vec, vec, vec,
            pl.BlockSpec((d, tf), lambda i, j: (0, j)),
            pl.BlockSpec((d, tf), lambda i, j: (0, j + nj)),
            pl.BlockSpec((3, tf), lambda i, j: (0, j)),
            pl.BlockSpec((3, tf), lambda i, j: (0, j + nj)),
            pl.BlockSpec((tf, d), lambda i, j: (j, 0))],
        out_specs=pl.BlockSpec((tm, d), lambda i, j: (i, 0)),
        out_shape=jax.ShapeDtypeStruct((n_rows, d), F32),
        scratch_shapes=[pltpu.VMEM((tm + 2 * HALO, d), BF16), pltpu.VMEM((tm, d), F32)],
        compiler_params=_cparams(("parallel", "arbitrary")),
        name="ffn",
    )(x, x, x, gain, shift, gate, w_up, w_up, cw, cw, w_down)


def _seg_sum(x, ones_bd):
    return _mm(x, ones_bd, 3, 1)


def _rw_prep_kernel(u_ref, wlora_ref, alora_ref, gup_ref, ones_ref, vec_ref,
                    com_ref, dir_ref, post_ref):
    u = u_ref[...]
    w = RW_WIDTH
    r = u[:, 0:w]
    k = u[:, w:2 * w]
    v = u[:, 2 * w:3 * w]
    wd = u[:, 3 * w:3 * w + 2 * DECAY_LORA]
    ad = u[:, 3 * w + 2 * DECAY_LORA:3 * w + 2 * DECAY_LORA + 2 * ICLR_LORA]
    gd = u[:, 3 * w + 2 * DECAY_LORA + 2 * ICLR_LORA:]
    vec = vec_ref[...]
    k_k, k_a, r_k = vec[0:1], vec[1:2], vec[2:3]
    ones_bd = ones_ref[...]

    kk = k * k_k
    kk = kk * lax.rsqrt(_seg_sum(kk * kk, ones_bd) + 1e-12)
    g = _mm(_sigmoid(gd), gup_ref[...], 2, 2)
    wl = _mm(jnp.tanh(wd), wlora_ref[...], 2, 2)
    al = _mm(ad, alora_ref[...], 2, 2)
    com_ref[0] = r
    com_ref[1] = v
    com_ref[2] = -kk
    kd_sum = None
    for d in range(2):
        w0 = vec[3 + d:4 + d]
        a0 = vec[5 + d:6 + d]
        lw = -math.exp(-0.5) * _sigmoid(w0 + wl[:, d * w:(d + 1) * w])
        a = _sigmoid(a0 + al[:, d * w:(d + 1) * w])
        kd = k * (1.0 + (a - 1.0) * k_a)
        dir_ref[d, 0] = lw
        dir_ref[d, 1] = kd
        dir_ref[d, 2] = kk * a
        kd_sum = kd if kd_sum is None else kd_sum + kd
    post_ref[0] = g
    post_ref[1] = _seg_sum(r * kd_sum * r_k, ones_bd) * v


def _rw_prep(u, col_block, wlora, alora, gup, ones_bd, vec):
    n_rows = u.shape[0]
    tm = _row_tile(n_rows, 512)
    w = RW_WIDTH
    full = lambda a: pl.BlockSpec(a.shape, lambda i: (0,) * a.ndim)
    return pl.pallas_call(
        _rw_prep_kernel,
        grid=(n_rows // tm,),
        in_specs=[pl.BlockSpec((tm, RW_COLS), lambda i: (i, col_block)),
                  full(wlora), full(alora), full(gup), full(ones_bd), full(vec)],
        out_specs=[pl.BlockSpec((3, tm, w), lambda i: (0, i, 0)),
                   pl.BlockSpec((2, 3, tm, w), lambda i: (0, 0, i, 0)),
                   pl.BlockSpec((2, tm, w), lambda i: (0, i, 0))],
        out_shape=[jax.ShapeDtypeStruct((3, n_rows, w), F32),
                   jax.ShapeDtypeStruct((2, 3, n_rows, w), F32),
                   jax.ShapeDtypeStruct((2, n_rows, w), F32)],
        compiler_params=_cparams(("parallel",)),
        name="rw_prep",
    )(u, wlora, alora, gup, ones_bd, vec)


def _smm(a, b, dims=NN):
    return _mm(a, b, 1, 1, dims)


def _scan_dir(com, dr, tri, slot, reverse):
    c = CHUNK
    r, v, a = com[0], com[1], com[2]
    lw, kd, b = dr[0], dr[1], dr[2]
    cs = _mm(tri, lw, 1, 3)
    tot = cs[0:1] if reverse else cs[c - 1:c]
    e_pos = jnp.exp(cs)
    e_neg = jnp.exp(-cs)
    e_end = jnp.exp(tot - cs)
    at = a * jnp.exp(cs - lw)
    rt = r * e_pos
    bt = b * e_neg
    kt = kd * e_neg
    bh = b * e_end
    kh = kd * e_end
    p_end = jnp.exp(tot)

    row = lax.broadcasted_iota(jnp.int32, (c, c), 0)
    col = lax.broadcasted_iota(jnp.int32, (c, c), 1)
    before = (col > row) if reverse else (col < row)
    upto = (col >= row) if reverse else (col <= row)
    eye = row == col
    lane = lax.broadcasted_iota(jnp.int32, (c, LANES), 1)
    same_head = (row // RW_HEAD_DIM) == (col // RW_HEAD_DIM)

    pairs = []
    for p in range(RW_HEADS // 2):
        sl = slice(p * LANES, (p + 1) * LANES)
        pairs.append(dict(
            slot=slot, p=p, at=at[:, sl], rt=rt[:, sl], v=v[:, sl].astype(BF16), v_t=v[:, sl].T,
            bk=jnp.concatenate([bt[:, sl], kt[:, sl]], axis=0).astype(BF16),
            bh=bh[:, sl].astype(BF16), kh=kh[:, sl].astype(BF16), p_end=p_end[:, sl],
            masks=(before, upto, eye, same_head),
            heads=[(lane // RW_HEAD_DIM) == h for h in range(2)]))
    return pairs


def _scan_chunk(pairs, s_ref):
    c = CHUNK
    chains = []
    for pr in pairs:
        before, upto, eye, _ = pr["masks"]
        for mh in pr["heads"]:
            ar = jnp.concatenate([jnp.where(mh, pr["at"], 0.0), jnp.where(mh, pr["rt"], 0.0)], axis=0)
            gm = _smm(ar, pr["bk"], NT)
            a_ab = jnp.where(before, gm[:c, :c], 0.0)
            chains.append(dict(
                pr=pr, mh=mh, apow=a_ab, t_inv=jnp.where(eye, 1.0, a_ab),
                a_ak=jnp.where(before, gm[:c, c:], 0.0).astype(BF16),
                a_rbk=jnp.concatenate([jnp.where(upto, gm[c:, :c], 0.0),
                                       jnp.where(upto, gm[c:, c:], 0.0)], axis=1).astype(BF16)))
    for ch in chains:
        ch["apow"] = ch["apow"].astype(BF16)
    for _ in range(int(math.log2(c)) - 1):
        for ch in chains:
            ch["apow"] = _smm(ch["apow"], ch["apow"]).astype(BF16)
        for ch in chains:
            ch["t_inv"] = ch["t_inv"] + _smm(ch["t_inv"], ch["apow"])
    for pr in pairs:
        s0 = s_ref[pr["slot"], pr["p"]]
        pr["s0"] = s0
        hr = _smm(jnp.concatenate([pr["at"], pr["rt"]], axis=0), s0, NT)
        pr["ah"], pr["rh"] = hr[:c], hr[c:]
        pr["u"] = None
    for ch in chains:
        pr = ch["pr"]
        x = _smm(ch["a_ak"], pr["v"])
        u_h = jnp.where(ch["mh"], _smm(ch["t_inv"], pr["ah"] + x), 0.0)
        pr["u"] = u_h if pr["u"] is None else pr["u"] + u_h
    outs = {}
    for pr in pairs:
        pr["uv"] = jnp.concatenate([pr["u"].astype(BF16), pr["v"]], axis=0)
        pr["y"] = pr["rh"]
    for ch in chains:
        pr = ch["pr"]
        pr["y"] = pr["y"] + jnp.where(ch["mh"], _smm(ch["a_rbk"], pr["uv"]), 0.0)
    for pr in pairs:
        same_head = pr["masks"][3]
        bkh = jnp.concatenate([pr["bh"], pr["kh"]], axis=0)
        uv_t = jnp.concatenate([pr["u"].T, pr["v_t"]], axis=1)
        upd = _smm(uv_t, bkh)
        s_ref[pr["slot"], pr["p"]] = pr["s0"] * pr["p_end"] + jnp.where(same_head, upd, 0.0)
        outs[(pr["slot"], pr["p"])] = pr["y"]
    n_pairs = RW_HEADS // 2
    return [jnp.concatenate([outs[(d, p)] for p in range(n_pairs)], axis=1) for d in range(2)]


def _rw_scan_kernel(comf_ref, dirf_ref, comb_ref, dirb_ref, tri_ref, yf_ref, yb_ref, s_ref):
    @pl.when(pl.program_id(0) == 0)
    def _():
        s_ref[...] = jnp.zeros_like(s_ref)

    pairs = (_scan_dir(comf_ref[...], dirf_ref[0], tri_ref[0], 0, False)
             + _scan_dir(comb_ref[...], dirb_ref[0], tri_ref[1], 1, True))
    yf, yb = _scan_chunk(pairs, s_ref)
    yf_ref[...] = yf
    yb_ref[...] = yb


def _rw_scan(com, dirs, tri, n_ctx_chunks):
    n_rows = com.shape[1]
    w = RW_WIDTH
    c = CHUNK
    assert c == LANES
    n_chunks = n_rows // c
    nc = n_ctx_chunks
    nl = n_chunks - nc

    def fwd_block(i):
        return jnp.where(i < nc, nl + i, i - nc)

    def bwd_block(i):
        return jnp.where(i < nc, nl + nc - 1 - i, nl - 1 - (i - nc))

    return pl.pallas_call(
        _rw_scan_kernel,
        grid=(n_chunks,),
        in_specs=[pl.BlockSpec((3, c, w), lambda i: (0, fwd_block(i), 0)),
                  pl.BlockSpec((1, 3, c, w), lambda i: (0, 0, fwd_block(i), 0)),
                  pl.BlockSpec((3, c, w), lambda i: (0, bwd_block(i), 0)),
                  pl.BlockSpec((1, 3, c, w), lambda i: (1, 0, bwd_block(i), 0)),
                  pl.BlockSpec((2, c, c), lambda i: (0, 0, 0))],
        out_specs=[pl.BlockSpec((c, w), lambda i: (fwd_block(i), 0)),
                   pl.BlockSpec((c, w), lambda i: (bwd_block(i), 0))],
        out_shape=[jax.ShapeDtypeStruct((n_rows, w), F32)] * 2,
        scratch_shapes=[pltpu.VMEM((2, RW_HEADS // 2, LANES, LANES), F32)],
        compiler_params=_cparams(("arbitrary",)),
        name="rw_scan",
    )(com, dirs, com, dirs, tri)


def _rw_post_kernel(yf_ref, yb_ref, post_ref, avg_ref, vec_ref, o_ref):
    y = yf_ref[...] + yb_ref[...]
    avg = avg_ref[...]
    mu = _mm(y, avg, 3, 1)
    yc = y - mu
    var = _mm(yc * yc, avg, 3, 1)
    yn = yc * lax.rsqrt(var + RW_LN_EPS)
    vec = vec_ref[...]
    o_ref[...] = (yn * vec[0:1] + vec[1:2] + post_ref[1]) * post_ref[0]


def _rw_post(yf, yb, post, avg_bd, vec, row_off):
    n_rows = post.shape[1]
    w = RW_WIDTH
    tm = _row_tile(n_rows, 512)
    assert row_off % tm == 0
    ob = row_off // tm
    full = lambda a: pl.BlockSpec(a.shape, lambda i: (0,) * a.ndim)
    return pl.pallas_call(
        _rw_post_kernel,
        grid=(n_rows // tm,),
        in_specs=[pl.BlockSpec((tm, w), lambda i: (i + ob, 0)),
                  pl.BlockSpec((tm, w), lambda i: (i + ob, 0)),
                  pl.BlockSpec((2, tm, w), lambda i: (0, i, 0)),
                  full(avg_bd), full(vec)],
        out_specs=pl.BlockSpec((tm, w), lambda i: (i, 0)),
        out_shape=jax.ShapeDtypeStruct((n_rows, w), F32),
        compiler_params=_cparams(("parallel",)),
        name="rw_post",
    )(yf, yb, post, avg_bd, vec)


def _rms(x, n):
    return x * lax.rsqrt(jnp.sum(x * x, axis=-1, keepdims=True) * (1.0 / n) + NORM_EPS)


def _mla_prep_kernel(*refs, rope):
    if rope:
        u_ref, wq_ref, wk_ref, wv_ref, vq_ref, vkv_ref, gains_ref, rope_ref, q_ref, k_ref, v_ref = refs
    else:
        u_ref, wq_ref, wk_ref, wv_ref, vq_ref, vkv_ref, gains_ref, q_ref, k_ref, v_ref = refs
    u = u_ref[...]
    cq = _rms(u[:, :Q_LORA], Q_LORA) * vq_ref[...]
    q = jnp.dot(cq.astype(BF16), wq_ref[...], preferred_element_type=F32)
    ckv = _rms(u[:, Q_LORA:Q_LORA + KV_LORA], KV_LORA) * vkv_ref[...]
    ckv = ckv.astype(BF16)
    kn = jnp.dot(ckv, wk_ref[...], preferred_element_type=F32)
    vv = jnp.dot(ckv, wv_ref[...], preferred_element_type=F32)
    kr = pltpu.roll(u[:, Q_LORA + KV_LORA:], NOPE_DIM, 1)
    gains = gains_ref[...]
    lane = lax.broadcasted_iota(jnp.int32, (1, HEAD_PAD), 1)
    ones_col = (lane == V_DIM).astype(F32)

    def rot(t):
        if not rope:
            return t
        tab = rope_ref[...]
        half = ROPE_DIM // 4
        return (t * tab[0] + pltpu.roll(t, HEAD_PAD - half, 1) * tab[1]
                + pltpu.roll(t, half, 1) * tab[2])

    for h in range(MLA_HEADS):
        sl = slice(h * HEAD_PAD, (h + 1) * HEAD_PAD)
        qh = _rms(q[:, sl], QK_DIM) * gains[0:1]
        kh = _rms(kn[:, sl] + kr, QK_DIM) * gains[1:2]
        q_ref[h] = rot(qh).astype(BF16)
        k_ref[h] = rot(kh).astype(BF16)
        v_ref[h] = (vv[:, sl] + ones_col).astype(BF16)


def _mla_prep(u, col_block, wq, wk, wv, vq, vkv, gains, rope_tab):
    n_rows = u.shape[0]
    tm = _row_tile(n_rows, 512)
    rope = rope_tab is not None
    full = lambda a: pl.BlockSpec(a.shape, lambda i: (0,) * a.ndim)
    in_specs = [pl.BlockSpec((tm, MLA_PAD), lambda i: (i, col_block)),
                full(wq), full(wk), full(wv), full(vq), full(vkv), full(gains)]
    args = [u, wq, wk, wv, vq, vkv, gains]
    if rope:
        in_specs.append(pl.BlockSpec((3, tm, HEAD_PAD), lambda i: (0, i, 0)))
        args.append(rope_tab)
    hs = pl.BlockSpec((MLA_HEADS, tm, HEAD_PAD), lambda i: (0, i, 0))
    shp = jax.ShapeDtypeStruct((MLA_HEADS, n_rows, HEAD_PAD), BF16)
    return pl.pallas_call(
        functools.partial(_mla_prep_kernel, rope=rope),
        grid=(n_rows // tm,),
        in_specs=in_specs,
        out_specs=[hs, hs, hs],
        out_shape=[shp, shp, shp],
        compiler_params=_cparams(("parallel",)),
        name="mla_prep",
    )(*args)


def _attn_kernel(q_ref, k_ref, v_ref, o_ref, m_scr, acc_scr, *, nk, sub):
    kv = pl.program_id(2)

    @pl.when(kv == 0)
    def _():
        m_scr[...] = jnp.full_like(m_scr, -jnp.inf)
        acc_scr[...] = jnp.zeros_like(acc_scr)

    k = k_ref[0]
    v = v_ref[0]
    tk = k.shape[0]
    for r in range(q_ref.shape[1] // sub):
        rows = slice(r * sub, (r + 1) * sub)
        s = lax.dot_general(q_ref[0, rows], k, NT, preferred_element_type=F32)
        m_prev = m_scr[rows]
        m_cur = s[:, 0:LANES]
        for cblk in range(1, tk // LANES):
            m_cur = jnp.maximum(m_cur, s[:, cblk * LANES:(cblk + 1) * LANES])
        m_new = jnp.maximum(m_prev, jnp.max(m_cur, axis=-1, keepdims=True))
        alpha = jnp.exp2(m_prev - m_new)
        p = jnp.concatenate(
            [jnp.exp2(s[:, cblk * LANES:(cblk + 1) * LANES] - m_new).astype(BF16)
             for cblk in range(tk // LANES)], axis=1)
        acc_scr[rows] = alpha * acc_scr[rows] + jnp.dot(p, v, preferred_element_type=F32)
        m_scr[rows] = m_new

    @pl.when(kv == nk - 1)
    def _():
        acc = acc_scr[...]
        lane = lax.broadcasted_iota(jnp.int32, acc.shape, 1)
        denom = jnp.sum(jnp.where(lane == V_DIM, acc, 0.0), axis=-1, keepdims=True)
        o_ref[...] = acc / denom


def _attention(q, k, v):
    h, lq, dp = q.shape
    lk = k.shape[1]
    tq = _row_tile(lq, 1024)
    sub = min(tq, 256)
    tk = lk
    for cand in (1280, 1024, 768, 512, 384, 256, 128):
        if lk % cand == 0:
            tk = cand
            break
    nk = lk // tk
    return pl.pallas_call(
        functools.partial(_attn_kernel, nk=nk, sub=sub),
        grid=(h, lq // tq, nk),
        in_specs=[pl.BlockSpec((1, tq, dp), lambda hh, i, j: (hh, i, 0)),
                  pl.BlockSpec((1, tk, dp), lambda hh, i, j: (hh, j, 0)),
                  pl.BlockSpec((1, tk, dp), lambda hh, i, j: (hh, j, 0))],
        out_specs=pl.BlockSpec((tq, dp), lambda hh, i, j: (i, hh)),
        out_shape=jax.ShapeDtypeStruct((lq, h * dp), F32),
        scratch_shapes=[pltpu.VMEM((tq, LANES), F32), pltpu.VMEM((tq, dp), F32)],
        compiler_params=_cparams(("parallel", "parallel", "arbitrary")),
        name="attn",
    )(q, k, v)


def _ft_chan_kernel(u_ref, m_ref, o_ref):
    o_ref[...] = _mm(u_ref[...], m_ref[...], 2, 2)


def _ft_chan(u, col_block, mat):
    n_rows = u.shape[0]
    tm = _row_tile(n_rows, 512)
    return pl.pallas_call(
        _ft_chan_kernel,
        grid=(n_rows // tm,),
        in_specs=[pl.BlockSpec((tm, FT_WIDTH), lambda i: (i, col_block)),
                  pl.BlockSpec(mat.shape, lambda i: (0, 0))],
        out_specs=pl.BlockSpec((tm, 2 * FT_WIDTH), lambda i: (i, 0)),
        out_shape=jax.ShapeDtypeStruct((n_rows, 2 * FT_WIDTH), F32),
        compiler_params=_cparams(("parallel",)),
        name="ft_chan",
    )(u, mat)


def _ft_s1_kernel(w_ref, cs_ref, tw_ref, o_ref, *, l1, g):
    z = _mm(cs_ref[...], w_ref[...], 2, 2)
    fw = FT_WIDTH
    for t in range(g):
        zr = z[:, t * 2 * fw:t * 2 * fw + fw]
        zi = z[:, t * 2 * fw + fw:(t + 1) * 2 * fw]
        a_re = zr[:l1] + zi[l1:]
        a_im = zi[:l1] - zr[l1:]
        tc = jnp.concatenate([tw_ref[0, t]] * (fw // LANES), axis=1)
        ts = jnp.concatenate([tw_ref[1, t]] * (fw // LANES), axis=1)
        o_ref[:, t * 2 * fw:t * 2 * fw + fw] = a_re * tc + a_im * ts
        o_ref[:, t * 2 * fw + fw:(t + 1) * 2 * fw] = a_im * tc - a_re * ts


def _ft_s1(w2d, cs, tw, l1, l2):
    g = min(8, l2)
    blk = g * 2 * FT_WIDTH
    return pl.pallas_call(
        functools.partial(_ft_s1_kernel, l1=l1, g=g),
        grid=(l2 // g,),
        in_specs=[pl.BlockSpec((l1, blk), lambda i: (0, i)),
                  pl.BlockSpec(cs.shape, lambda i: (0, 0)),
                  pl.BlockSpec((2, g, l1, LANES), lambda i: (0, i, 0, 0))],
        out_specs=pl.BlockSpec((l1, blk), lambda i: (0, i)),
        out_shape=jax.ShapeDtypeStruct((l1, l2 * 2 * FT_WIDTH), F32),
        compiler_params=_cparams(("parallel",)),
        name="ft_s1",
    )(w2d, cs, tw)


def _ft_s2_kernel(b_ref, cs_ref, o_ref, *, g, scale):
    fw = FT_WIDTH
    cmat = cs_ref[0]
    smat = cs_ref[1]
    for kk in range(g):
        blk = b_ref[kk]
        out = _mm(cmat, blk[:, :fw], 2, 2) + _mm(smat, blk[:, fw:], 2, 2)
        o_ref[:, kk * fw:(kk + 1) * fw] = out * scale


def _ft_s2(b3d, cs2, l1, l2, scale):
    g = min(8, l1)
    return pl.pallas_call(
        functools.partial(_ft_s2_kernel, g=g, scale=scale),
        grid=(l1 // g,),
        in_specs=[pl.BlockSpec((g, l2, 2 * FT_WIDTH), lambda i: (i, 0, 0)),
                  pl.BlockSpec(cs2.shape, lambda i: (0, 0, 0))],
        out_specs=pl.BlockSpec((l2, g * FT_WIDTH), lambda i: (0, i)),
        out_shape=jax.ShapeDtypeStruct((l2, l1 * FT_WIDTH), F32),
        compiler_params=_cparams(("parallel",)),
        name="ft_s2",
    )(b3d, cs2)


def _ft_tables(n):
    l1 = 1 << (int(math.log2(n)) // 2)
    l2 = n // l1
    assert l1 * l2 == n and l1 % 8 == 0 and l2 % 8 == 0
    k1 = np.arange(l1)
    ang1 = 2.0 * np.pi * np.outer(k1, np.arange(l1)) / l1
    cs1 = np.concatenate([np.cos(ang1), np.sin(ang1)], axis=0)
    angt = 2.0 * np.pi * np.outer(np.arange(l2), k1) / n
    tw = np.stack([np.cos(angt), np.sin(angt)], axis=0)
    tw = np.broadcast_to(tw[..., None], (2, l2, l1, LANES))
    ang2 = 2.0 * np.pi * np.outer(np.arange(l2), np.arange(l2)) / l2
    cs2 = np.stack([np.cos(ang2), np.sin(ang2)], axis=0)
    f = lambda a: jnp.asarray(np.ascontiguousarray(a), F32)
    return l1, l2, f(cs1), f(tw), f(cs2)


def _chan_dft_matrix():
    n = FT_GROUP_DIM
    ang = 2.0 * np.pi * np.outer(np.arange(n), np.arange(n)) / n
    eye = np.eye(FT_GROUPS)
    mat = np.concatenate([np.kron(eye, np.cos(ang)), -np.kron(eye, np.sin(ang))], axis=1)
    return jnp.asarray(mat, F32)


def _fourier_mix(u, col_block, chan_mat):
    n = u.shape[0]
    l1, l2, cs1, tw, cs2 = _ft_tables(n)
    wc = _ft_chan(u, col_block, chan_mat)
    b = _ft_s1(wc.reshape(l1, l2 * 2 * FT_WIDTH), cs1, tw, l1, l2)
    scale = 1.0 / math.sqrt(n * FT_GROUP_DIM)
    out = _ft_s2(b.reshape(l1, l2, 2 * FT_WIDTH), cs2, l1, l2, scale)
    return out.reshape(n, FT_WIDTH)


def _merge_kernel(ya_ref, yb_ref, yc_ref, ga_ref, gb_ref, gc_ref, x_ref, gt_ref,
                  wa_ref, wb_ref, wc_ref, wo_ref, o_ref):
    def branch(y_ref, w_ref, g_ref):
        t = jnp.dot(y_ref[...].astype(BF16), w_ref[...], preferred_element_type=F32)
        return _sigmoid(g_ref[...]) * t

    m = branch(ya_ref, wa_ref, ga_ref) + branch(yb_ref, wb_ref, gb_ref) + branch(yc_ref, wc_ref, gc_ref)
    out = jnp.dot(m.astype(BF16), wo_ref[...], preferred_element_type=F32)
    o_ref[...] = x_ref[...] + gt_ref[...] * out


def _merge(ya, yb, yc, u, x, gt, wa, wb, wc, wo):
    n_rows, d = x.shape
    tm = _row_tile(n_rows, 512)
    full = lambda a: pl.BlockSpec(a.shape, lambda i: (0,) * a.ndim)
    rows = lambda a: pl.BlockSpec((tm, a.shape[1]), lambda i: (i, 0))
    gate = lambda b: pl.BlockSpec((tm, d), lambda i: (i, b))
    return pl.pallas_call(
        _merge_kernel,
        grid=(n_rows // tm,),
        in_specs=[rows(ya), rows(yb), rows(yc), gate(0), gate(1), gate(2), rows(x), full(gt),
                  full(wa), full(wb), full(wc), full(wo)],
        out_specs=pl.BlockSpec((tm, d), lambda i: (i, 0)),
        out_shape=jax.ShapeDtypeStruct((n_rows, d), F32),
        compiler_params=_cparams(("parallel",)),
        name="merge",
    )(ya, yb, yc, u, u, u, x, gt, wa, wb, wc, wo)


def _pad_heads(w, src_cols, width):
    k = w.shape[0]
    wh = w.reshape(k, MLA_HEADS, width)[:, :, src_cols]
    wh = jnp.pad(wh, ((0, 0), (0, 0), (0, HEAD_PAD - wh.shape[2])))
    return wh.reshape(k, MLA_HEADS * HEAD_PAD)


def _block_diag2(a, b):
    z01 = jnp.zeros((a.shape[0], b.shape[1]), a.dtype)
    z10 = jnp.zeros((b.shape[0], a.shape[1]), a.dtype)
    return jnp.concatenate([jnp.concatenate([a, z01], axis=1), jnp.concatenate([z10, b], axis=1)], axis=0)


def _rope_tables(n):
    half = ROPE_DIM // 2
    inv = ROPE_BASE ** (-np.arange(0, half, 2, dtype=np.float32) / half)
    t = np.arange(n)
    row = (t // GRID_W).astype(np.float32)
    col = (t % GRID_W).astype(np.float32)
    q = half // 2
    cos = np.ones((n, HEAD_PAD), np.float32)
    sin_a = np.zeros((n, HEAD_PAD), np.float32)
    sin_b = np.zeros((n, HEAD_PAD), np.float32)
    for base, pos in ((NOPE_DIM, row), (NOPE_DIM + half, col)):
        ang = pos[:, None] * inv[None, :]
        c, s = np.cos(ang), np.sin(ang)
        cos[:, base:base + q] = c
        cos[:, base + q:base + 2 * q] = c
        sin_a[:, base:base + q] = -s
        sin_b[:, base + q:base + 2 * q] = s
    return jnp.asarray(np.stack([cos, sin_a, sin_b], axis=0))


def _layer_weights(l, w):
    w_in = w["w_in"][l]
    o1 = RW_COLS
    o2 = o1 + MLA_COLS
    o3 = o2 + FT_WIDTH
    w_in_p = jnp.concatenate(
        [w_in[:, o3:], w_in[:, :o1], w_in[:, o1:o2],
         jnp.zeros((D_MODEL, MLA_PAD - MLA_COLS), F32), w_in[:, o2:o3]], axis=1).astype(BF16)
    ident = jnp.tile(jnp.array([[0.0], [1.0], [0.0]], F32), (1, IN_PAD))
    cw_in = ident.at[:, RW_OFF:RW_OFF + RW_COLS].set(w["rw_conv"][l])
    pad_gain = lambda g: jnp.pad(g, (0, HEAD_PAD - QK_DIM))
    wb = w["w_branch_b"][l].reshape(MLA_HEADS, V_DIM, D_MODEL)
    wb = jnp.pad(wb, ((0, 0), (0, HEAD_PAD - V_DIM), (0, 0))).reshape(MLA_HEADS * HEAD_PAD, D_MODEL)
    q_scale = math.log2(math.e) * QK_DIM ** -0.5
    return dict(
        w_in=w_in_p, cw_in=cw_in,
        wlora=_block_diag2(w["rw_w_up"][l, 0], w["rw_w_up"][l, 1]),
        alora=_block_diag2(w["rw_a_up"][l, 0], w["rw_a_up"][l, 1]),
        gup=w["rw_g_up"][l],
        rw_vec=jnp.stack([w["rw_k_k"][l], w["rw_k_a"][l], w["rw_r_k"][l].reshape(-1),
                          w["rw_w0"][l, 0], w["rw_w0"][l, 1], w["rw_a0"][l, 0], w["rw_a0"][l, 1],
                          jnp.zeros((RW_WIDTH,), F32)], axis=0),
        ln_vec=jnp.stack([w["rw_ln_w"][l], w["rw_ln_b"][l]], axis=0),
        wq=_pad_heads(w["w_uq"][l], slice(0, QK_DIM), QK_DIM).astype(BF16),
        wk=_pad_heads(w["w_ukv"][l], slice(0, NOPE_DIM), NOPE_DIM + V_DIM).astype(BF16),
        wv=_pad_heads(w["w_ukv"][l], slice(NOPE_DIM, NOPE_DIM + V_DIM), NOPE_DIM + V_DIM).astype(BF16),
        vq=w["mla_q_norm"][l][None], vkv=w["mla_kv_norm"][l][None],
        gains_lat=jnp.stack([pad_gain(w["q_gain"][l]) * q_scale, pad_gain(w["k_gain"][l])], axis=0),
        wa=w["w_branch_a"][l].astype(BF16), wb=wb.astype(BF16), wc=w["w_branch_c"][l].astype(BF16),
        wo=w["w_out"][l].astype(BF16),
        w_up=w["w_up"][l].astype(BF16), cw_ffn=w["ffn_conv"][l], w_down=w["w_down"][l].astype(BF16),
    )


def _forward(x, c, ctx, c_ctx, w):
    depth = w["w_ada"].shape[0]
    x_lat = x[0]
    x_ctx = ctx[0]
    n_lat = x_lat.shape[0]
    n_ctx = x_ctx.shape[0]
    assert n_lat % CHUNK == 0 and n_ctx % CHUNK == 0 and n_lat % GRID_W == 0

    cc = jnp.zeros((8, D_MODEL), F32).at[0].set(c[0]).at[1].set(c_ctx)
    mods = _mods(cc, w["w_ada"], w["b_ada"])

    head = np.arange(RW_WIDTH) // RW_HEAD_DIM
    ones_bd = jnp.asarray((head[:, None] == head[None, :]).astype(np.float32))
    avg_bd = ones_bd * (1.0 / RW_HEAD_DIM)
    ti = np.arange(CHUNK)
    tri = jnp.asarray(np.stack([ti[None, :] <= ti[:, None], ti[None, :] >= ti[:, None]]).astype(np.float32),
                      BF16)
    chan_mat = _chan_dft_matrix()
    rope_tab = _rope_tables(n_lat)

    for l in range(depth):
        last = l == depth - 1
        p = _layer_weights(l, w)
        streams = [(x_lat, mods[l, 0], rope_tab)]
        if True:
            streams.append((x_ctx, mods[l, 1], None))
        us, preps, qkv = [], [], []
        for xs, mod, rt in streams:
            sh1, sc1 = mod[0:D_MODEL][None], mod[D_MODEL:2 * D_MODEL][None]
            gain1 = w["g_norm1"][l][None] * (1.0 + sc1)
            u = _in_proj(xs, gain1, sh1, p["w_in"], p["cw_in"])
            us.append(u)
            preps.append(_rw_prep(u, RW_OFF // RW_COLS, p["wlora"], p["alora"], p["gup"], ones_bd, p["rw_vec"]))
            qkv.append(_mla_prep(u, MLA_OFF // MLA_PAD, p["wq"], p["wk"], p["wv"], p["vq"], p["vkv"],
                                 p["gains_lat"], rt))
        com = jnp.concatenate([preps[0][0], preps[1][0]], axis=1)
        dirs = jnp.concatenate([preps[0][1], preps[1][1]], axis=2)
        yf, yb = _rw_scan(com, dirs, tri, n_ctx // CHUNK)
        k_all = jnp.concatenate([qkv[0][1], qkv[1][1]], axis=1)
        v_all = jnp.concatenate([qkv[0][2], qkv[1][2]], axis=1)

        new = []
        for si, (xs, mod, rt) in enumerate(streams):
            if si == 1 and last:
                continue
            row_off = 0 if si == 0 else n_lat
            ya = _rw_post(yf, yb, preps[si][2], avg_bd, p["ln_vec"], row_off)
            if si == 0:
                yb_att = _attention(qkv[0][0], k_all, v_all)
            else:
                yb_att = _attention(qkv[1][0], qkv[1][1], qkv[1][2])
            yc = _fourier_mix(us[si], FT_OFF // FT_WIDTH, chan_mat)
            gt1 = mod[2 * D_MODEL:3 * D_MODEL][None]
            x1 = _merge(ya, yb_att, yc, us[si], xs, gt1, p["wa"], p["wb"], p["wc"], p["wo"])
            sh2, sc2, gt2 = (mod[3 * D_MODEL:4 * D_MODEL][None], mod[4 * D_MODEL:5 * D_MODEL][None],
                             mod[5 * D_MODEL:6 * D_MODEL][None])
            gain2 = w["g_norm2"][l][None] * (1.0 + sc2)
            new.append(_ffn(x1, gain2, sh2, gt2, p["w_up"], p["cw_ffn"], p["w_down"]))
        x_lat = new[0]
        if not last:
            x_ctx = new[1]
    return x_lat[None]


def kernel(x, c, ctx, c_ctx, w_ada, b_ada, g_norm1, g_norm2, w_in, rw_conv, rw_w0, rw_w_up, rw_a0, rw_a_up,
           rw_g_up, rw_k_k, rw_k_a, rw_r_k, rw_ln_w, rw_ln_b, mla_q_norm, w_uq, mla_kv_norm, w_ukv, q_gain,
           k_gain, w_branch_a, w_branch_b, w_branch_c, w_out, ffn_conv, w_up, w_down):
    w = dict(w_ada=w_ada, b_ada=b_ada, g_norm1=g_norm1, g_norm2=g_norm2, w_in=w_in, rw_conv=rw_conv,
             rw_w0=rw_w0, rw_w_up=rw_w_up, rw_a0=rw_a0, rw_a_up=rw_a_up, rw_g_up=rw_g_up, rw_k_k=rw_k_k,
             rw_k_a=rw_k_a, rw_r_k=rw_r_k, rw_ln_w=rw_ln_w, rw_ln_b=rw_ln_b, mla_q_norm=mla_q_norm,
             w_uq=w_uq, mla_kv_norm=mla_kv_norm, w_ukv=w_ukv, q_gain=q_gain, k_gain=k_gain,
             w_branch_a=w_branch_a, w_branch_b=w_branch_b, w_branch_c=w_branch_c, w_out=w_out,
             ffn_conv=ffn_conv, w_up=w_up, w_down=w_down)
    return _forward(x, c, ctx, c_ctx, w)
```

```python
import functools
import math

import numpy as np
import jax
import jax.numpy as jnp
from jax import lax
from jax.experimental import pallas as pl
from jax.experimental.pallas import tpu as pltpu

F32 = jnp.float32
BF16 = jnp.bfloat16

D_MODEL = 1024
GRID_W = 64
NORM_EPS = 1e-6
RW_HEADS = 6
RW_HEAD_DIM = 64
RW_WIDTH = RW_HEADS * RW_HEAD_DIM
DECAY_LORA = 64
ICLR_LORA = 64
GATE_LORA = 128
RW_LN_EPS = 64e-5
MLA_HEADS = 6
Q_LORA = 384
KV_LORA = 256
NOPE_DIM = 64
ROPE_DIM = 32
V_DIM = 64
QK_DIM = NOPE_DIM + ROPE_DIM
ROPE_BASE = 10000.0
FT_GROUPS = 4
FT_GROUP_DIM = 64
FT_WIDTH = FT_GROUPS * FT_GROUP_DIM
D_FF = 2816
N_MOD = 6
RW_COLS = 3 * RW_WIDTH + 2 * DECAY_LORA + 2 * ICLR_LORA + GATE_LORA
MLA_COLS = Q_LORA + KV_LORA + ROPE_DIM

LANES = 128
HEAD_PAD = 128
MLA_PAD = 768
GATE_OFF = 0
RW_OFF = 3 * D_MODEL
MLA_OFF = RW_OFF + RW_COLS
FT_OFF = MLA_OFF + MLA_PAD
IN_PAD = FT_OFF + FT_WIDTH
HALO = 16
CHUNK = 128
VMEM_LIMIT = 48 * 1024 * 1024
V_ROWS = 80
BOUND_MARGIN = 1.02
MAX_SCORE_SPAN = 100.0

NN = (((1,), (0,)), ((), ()))
NT = (((1,), (1,)), ((), ()))


def _pieces(x, n):
    if x.dtype == BF16:
        return [x]
    out = []
    rem = x
    for i in range(n):
        p = rem.astype(BF16)
        out.append(p)
        if i + 1 < n:
            rem = rem - p.astype(F32)
    return out


def _mm(a, b, pa=1, pb=1, dims=NN):
    ap = _pieces(a, pa)
    bp = _pieces(b, pb)
    n = max(len(ap), len(bp))
    acc = None
    for i in reversed(range(len(ap))):
        for j in reversed(range(len(bp))):
            if i + j < n:
                t = lax.dot_general(ap[i], bp[j], dims, preferred_element_type=F32)
                acc = t if acc is None else acc + t
    return acc


def _sigmoid(x):
    return 1.0 / (1.0 + jnp.exp(-x))


def _cparams(sem):
    return pltpu.CompilerParams(dimension_semantics=sem, vmem_limit_bytes=VMEM_LIMIT)


def _row_tile(n, pref):
    t = min(n, pref)
    while n % t:
        t //= 2
    return t


def _mods_kernel(cc_ref, w_ref, b_ref, o_ref):
    cc = cc_ref[...]
    s = cc * _sigmoid(cc)
    o_ref[0] = _mm(s, w_ref[0], 2, 2) + b_ref[0]


def _mods(cc, w_ada, b_ada):
    depth, d, n = w_ada.shape
    tn = 1536
    return pl.pallas_call(
        _mods_kernel,
        grid=(depth, n // tn),
        in_specs=[pl.BlockSpec((8, d), lambda l, j: (0, 0)),
                  pl.BlockSpec((1, d, tn), lambda l, j: (l, 0, j)),
                  pl.BlockSpec((1, 1, tn), lambda l, j: (l, 0, j))],
        out_specs=pl.BlockSpec((1, 8, tn), lambda l, j: (l, 0, j)),
        out_shape=jax.ShapeDtypeStruct((depth, 8, n), F32),
        compiler_params=_cparams(("parallel", "parallel")),
        name="mods",
    )(cc, w_ada, b_ada.reshape(depth, 1, n))


def _modnorm(xv, g, s):
    ms = jnp.mean(xv * xv, axis=-1, keepdims=True)
    return (xv * lax.rsqrt(ms + NORM_EPS)) * g + s


def _fill_normed(h_scr, x_ref, xp_ref, xn_ref, g, s, i, nrt, tm):
    h_scr[HALO:HALO + tm] = _modnorm(x_ref[...], g, s).astype(BF16)
    hp = jnp.where(i > 0, _modnorm(xp_ref[...], g, s), 0.0)
    h_scr[0:HALO] = hp.astype(BF16)
    hn = jnp.where(i < nrt - 1, _modnorm(xn_ref[...], g, s), 0.0)
    h_scr[HALO + tm:2 * HALO + tm] = hn.astype(BF16)


def _conv3(u, cw, tm):
    rows = u.shape[0]
    up = pltpu.roll(u, 1, 0)
    un = pltpu.roll(u, rows - 1, 0)
    out = cw[0:1] * up + cw[1:2] * u + cw[2:3] * un
    return out[HALO:HALO + tm]


def _halo_specs(tm, d, n_rows):
    r = tm // HALO
    last = n_rows // HALO - 1
    return [pl.BlockSpec((tm, d), lambda i, j: (i, 0)),
            pl.BlockSpec((HALO, d), lambda i, j: (jnp.maximum(i * r - 1, 0), 0)),
            pl.BlockSpec((HALO, d), lambda i, j: (jnp.minimum((i + 1) * r, last), 0))]


def _in_kernel(x_ref, xp_ref, xn_ref, g_ref, s_ref, w_ref, cw_ref, o_ref, h_scr, *, tm, nrt):
    i = pl.program_id(0)

    @pl.when(pl.program_id(1) == 0)
    def _():
        _fill_normed(h_scr, x_ref, xp_ref, xn_ref, g_ref[...], s_ref[...], i, nrt, tm)

    u = jnp.dot(h_scr[...], w_ref[...], preferred_element_type=F32)
    o_ref[...] = _conv3(u, cw_ref[...], tm)


def _in_proj(x, gain, shift, w, cw):
    n_rows, d = x.shape
    n = w.shape[1]
    tm = _row_tile(n_rows, 1024)
    tn = 512
    nrt = n_rows // tm
    return pl.pallas_call(
        functools.partial(_in_kernel, tm=tm, nrt=nrt),
        grid=(nrt, n // tn),
        in_specs=_halo_specs(tm, d, n_rows) + [
            pl.BlockSpec((1, d), lambda i, j: (0, 0)),
            pl.BlockSpec((1, d), lambda i, j: (0, 0)),
            pl.BlockSpec((d, tn), lambda i, j: (0, j)),
            pl.BlockSpec((3, tn), lambda i, j: (0, j))],
        out_specs=pl.BlockSpec((tm, tn), lambda i, j: (i, j)),
        out_shape=jax.ShapeDtypeStruct((n_rows, n), F32),
        scratch_shapes=[pltpu.VMEM((tm + 2 * HALO, d), BF16)],
        compiler_params=_cparams(("parallel", "arbitrary")),
        name="in_proj",
    )(x, x, x, gain, shift, w, cw)


def _ffn_kernel(x_ref, xp_ref, xn_ref, g_ref, s_ref, gt_ref, wa_ref, wb_ref, cwa_ref, cwb_ref,
                wd_ref, o_ref, h_scr, acc_scr, *, tm, nrt, nj):
    i = pl.program_id(0)
    j = pl.program_id(1)

    @pl.when(j == 0)
    def _():
        _fill_normed(h_scr, x_ref, xp_ref, xn_ref, g_ref[...], s_ref[...], i, nrt, tm)
        acc_scr[...] = jnp.zeros_like(acc_scr)

    h = h_scr[...]
    a = _conv3(jnp.dot(h, wa_ref[...], preferred_element_type=F32), cwa_ref[...], tm)
    b = _conv3(jnp.dot(h, wb_ref[...], preferred_element_type=F32), cwb_ref[...], tm)
    gated = (a * _sigmoid(a) * b).astype(BF16)
    acc_scr[...] += jnp.dot(gated, wd_ref[...], preferred_element_type=F32)

    @pl.when(j == nj - 1)
    def _():
        o_ref[...] = x_ref[...] + gt_ref[...] * acc_scr[...]


def _ffn(x, gain, shift, gate, w_up, cw, w_down):
    n_rows, d = x.shape
    tm = _row_tile(n_rows, 1024)
    tf = 256
    nj = D_FF // tf
    nrt = n_rows // tm
    vec = pl.BlockSpec((1, d), lambda i, j: (0, 0))
    return pl.pallas_call(
        functools.partial(_ffn_kernel, tm=tm, nrt=nrt, nj=nj),
        grid=(nrt, nj),
        in_specs=_halo_specs(tm, d, n_rows) + [
            vec, vec, vec,
            pl.BlockSpec((d, tf), lambda i, j: (0, j)),
            pl.BlockSpec((d, tf), lambda i, j: (0, j + nj)),
            pl.BlockSpec((3, tf), lambda i, j: (0, j)),
            pl.BlockSpec((3, tf), lambda i, j: (0, j + nj)),
            pl.BlockSpec((tf, d), lambda i, j: (j, 0))],
        out_specs=pl.BlockSpec((tm, d), lambda i, j: (i, 0)),
        out_shape=jax.ShapeDtypeStruct((n_rows, d), F32),
        scratch_shapes=[pltpu.VMEM((tm + 2 * HALO, d), BF16), pltpu.VMEM((tm, d), F32)],
        compiler_params=_cparams(("parallel", "arbitrary")),
        name="ffn",
    )(x, x, x, gain, shift, gate, w_up, w_up, cw, cw, w_down)


def _seg_sum(x, ones_bd):
    return _mm(x, ones_bd, 3, 1)


def _rw_prep_kernel(u_ref, wlora_ref, alora_ref, gup_ref, ones_ref, vec_ref,
                    com_ref, dir_ref, post_ref):
    u = u_ref[...]
    w = RW_WIDTH
    r = u[:, 0:w]
    k = u[:, w:2 * w]
    v = u[:, 2 * w:3 * w]
    wd = u[:, 3 * w:3 * w + 2 * DECAY_LORA]
    ad = u[:, 3 * w + 2 * DECAY_LORA:3 * w + 2 * DECAY_LORA + 2 * ICLR_LORA]
    gd = u[:, 3 * w + 2 * DECAY_LORA + 2 * ICLR_LORA:]
    vec = vec_ref[...]
    k_k, k_a, r_k = vec[0:1], vec[1:2], vec[2:3]
    ones_bd = ones_ref[...]

    kk = k * k_k
    kk = kk * lax.rsqrt(_seg_sum(kk * kk, ones_bd) + 1e-12)
    g = _mm(_sigmoid(gd), gup_ref[...], 2, 2)
    wl = _mm(jnp.tanh(wd), wlora_ref[...], 2, 2)
    al = _mm(ad, alora_ref[...], 2, 2)
    com_ref[0] = r
    com_ref[1] = v
    com_ref[2] = -kk
    kd_sum = None
    for d in range(2):
        w0 = vec[3 + d:4 + d]
        a0 = vec[5 + d:6 + d]
        lw = -math.exp(-0.5) * _sigmoid(w0 + wl[:, d * w:(d + 1) * w])
        a = _sigmoid(a0 + al[:, d * w:(d + 1) * w])
        kd = k * (1.0 + (a - 1.0) * k_a)
        dir_ref[d, 0] = lw
        dir_ref[d, 1] = kd
        dir_ref[d, 2] = kk * a
        kd_sum = kd if kd_sum is None else kd_sum + kd
    post_ref[0] = g
    post_ref[1] = _seg_sum(r * kd_sum * r_k, ones_bd) * v


def _rw_prep(u, col_block, wlora, alora, gup, ones_bd, vec):
    n_rows = u.shape[0]
    tm = _row_tile(n_rows, 512)
    w = RW_WIDTH
    full = lambda a: pl.BlockSpec(a.shape, lambda i: (0,) * a.ndim)
    return pl.pallas_call(
        _rw_prep_kernel,
        grid=(n_rows // tm,),
        in_specs=[pl.BlockSpec((tm, RW_COLS), lambda i: (i, col_block)),
                  full(wlora), full(alora), full(gup), full(ones_bd), full(vec)],
        out_specs=[pl.BlockSpec((3, tm, w), lambda i: (0, i, 0)),
                   pl.BlockSpec((2, 3, tm, w), lambda i: (0, 0, i, 0)),
                   pl.BlockSpec((2, tm, w), lambda i: (0, i, 0))],
        out_shape=[jax.ShapeDtypeStruct((3, n_rows, w), F32),
                   jax.ShapeDtypeStruct((2, 3, n_rows, w), F32),
                   jax.ShapeDtypeStruct((2, n_rows, w), F32)],
        compiler_params=_cparams(("parallel",)),
        name="rw_prep",
    )(u, wlora, alora, gup, ones_bd, vec)


def _smm(a, b, dims=NN):
    return _mm(a, b, 1, 1, dims)


def _scan_dir(com, dr, tri, slot, reverse):
    c = CHUNK
    r, v, a = com[0], com[1], com[2]
    lw, kd, b = dr[0], dr[1], dr[2]
    cs = _mm(tri, lw, 1, 3)
    tot = cs[0:1] if reverse else cs[c - 1:c]
    e_pos = jnp.exp(cs)
    e_neg = jnp.exp(-cs)
    e_end = jnp.exp(tot - cs)
    at = a * jnp.exp(cs - lw)
    rt = r * e_pos
    bt = b * e_neg
    kt = kd * e_neg
    bh = b * e_end
    kh = kd * e_end
    p_end = jnp.exp(tot)

    row = lax.broadcasted_iota(jnp.int32, (c, c), 0)
    col = lax.broadcasted_iota(jnp.int32, (c, c), 1)
    before = (col > row) if reverse else (col < row)
    upto = (col >= row) if reverse else (col <= row)
    eye = row == col
    lane = lax.broadcasted_iota(jnp.int32, (c, LANES), 1)
    same_head = (row // RW_HEAD_DIM) == (col // RW_HEAD_DIM)

    pairs = []
    for p in range(RW_HEADS // 2):
        sl = slice(p * LANES, (p + 1) * LANES)
        pairs.append(dict(
            slot=slot, p=p, at=at[:, sl], rt=rt[:, sl], v=v[:, sl].astype(BF16), v_t=v[:, sl].T,
            bk=jnp.concatenate([bt[:, sl], kt[:, sl]], axis=0).astype(BF16),
            bh=bh[:, sl].astype(BF16), kh=kh[:, sl].astype(BF16), p_end=p_end[:, sl],
            masks=(before, upto, eye, same_head),
            heads=[(lane // RW_HEAD_DIM) == h for h in range(2)]))
    return pairs


def _scan_chunk(pairs, s_ref):
    c = CHUNK
    chains = []
    for pr in pairs:
        before, upto, eye, _ = pr["masks"]
        for mh in pr["heads"]:
            ar = jnp.concatenate([jnp.where(mh, pr["at"], 0.0), jnp.where(mh, pr["rt"], 0.0)], axis=0)
            gm = _smm(ar, pr["bk"], NT)
            a_ab = jnp.where(before, gm[:c, :c], 0.0)
            chains.append(dict(
                pr=pr, mh=mh, apow=a_ab, t_inv=jnp.where(eye, 1.0, a_ab),
                a_ak=jnp.where(before, gm[:c, c:], 0.0).astype(BF16),
                a_rbk=jnp.concatenate([jnp.where(upto, gm[c:, :c], 0.0),
                                       jnp.where(upto, gm[c:, c:], 0.0)], axis=1).astype(BF16)))
    for ch in chains:
        ch["apow"] = ch["apow"].astype(BF16)
    for _ in range(int(math.log2(c)) - 1):
        for ch in chains:
            ch["apow"] = _smm(ch["apow"], ch["apow"]).astype(BF16)
        for ch in chains:
            ch["t_inv"] = ch["t_inv"] + _smm(ch["t_inv"], ch["apow"])
    for pr in pairs:
        s0 = s_ref[pr["slot"], pr["p"]]
        pr["s0"] = s0
        hr = _smm(jnp.concatenate([pr["at"], pr["rt"]], axis=0), s0, NT)
        pr["ah"], pr["rh"] = hr[:c], hr[c:]
        pr["u"] = None
    for ch in chains:
        pr = ch["pr"]
        x = _smm(ch["a_ak"], pr["v"])
        u_h = jnp.where(ch["mh"], _smm(ch["t_inv"], pr["ah"] + x), 0.0)
        pr["u"] = u_h if pr["u"] is None else pr["u"] + u_h
    outs = {}
    for pr in pairs:
        pr["uv"] = jnp.concatenate([pr["u"].astype(BF16), pr["v"]], axis=0)
        pr["y"] = pr["rh"]
    for ch in chains:
        pr = ch["pr"]
        pr["y"] = pr["y"] + jnp.where(ch["mh"], _smm(ch["a_rbk"], pr["uv"]), 0.0)
    for pr in pairs:
        same_head = pr["masks"][3]
        bkh = jnp.concatenate([pr["bh"], pr["kh"]], axis=0)
        uv_t = jnp.concatenate([pr["u"].T, pr["v_t"]], axis=1)
        upd = _smm(uv_t, bkh)
        s_ref[pr["slot"], pr["p"]] = pr["s0"] * pr["p_end"] + jnp.where(same_head, upd, 0.0)
        outs[(pr["slot"], pr["p"])] = pr["y"]
    n_pairs = RW_HEADS // 2
    return [jnp.concatenate([outs[(d, p)] for p in range(n_pairs)], axis=1) for d in range(2)]


def _rw_scan_kernel(comf_ref, dirf_ref, comb_ref, dirb_ref, tri_ref, yf_ref, yb_ref, s_ref):
    @pl.when(pl.program_id(0) == 0)
    def _():
        s_ref[...] = jnp.zeros_like(s_ref)

    pairs = (_scan_dir(comf_ref[...], dirf_ref[0], tri_ref[0], 0, False)
             + _scan_dir(comb_ref[...], dirb_ref[0], tri_ref[1], 1, True))
    yf, yb = _scan_chunk(pairs, s_ref)
    yf_ref[...] = yf
    yb_ref[...] = yb


def _rw_scan(com, dirs, tri, n_ctx_chunks):
    n_rows = com.shape[1]
    w = RW_WIDTH
    c = CHUNK
    assert c == LANES
    n_chunks = n_rows // c
    nc = n_ctx_chunks
    nl = n_chunks - nc

    def fwd_block(i):
        return jnp.where(i < nc, nl + i, i - nc)

    def bwd_block(i):
        return jnp.where(i < nc, nl + nc - 1 - i, nl - 1 - (i - nc))

    return pl.pallas_call(
        _rw_scan_kernel,
        grid=(n_chunks,),
        in_specs=[pl.BlockSpec((3, c, w), lambda i: (0, fwd_block(i), 0)),
                  pl.BlockSpec((1, 3, c, w), lambda i: (0, 0, fwd_block(i), 0)),
                  pl.BlockSpec((3, c, w), lambda i: (0, bwd_block(i), 0)),
                  pl.BlockSpec((1, 3, c, w), lambda i: (1, 0, bwd_block(i), 0)),
                  pl.BlockSpec((2, c, c), lambda i: (0, 0, 0))],
        out_specs=[pl.BlockSpec((c, w), lambda i: (fwd_block(i), 0)),
                   pl.BlockSpec((c, w), lambda i: (bwd_block(i), 0))],
        out_shape=[jax.ShapeDtypeStruct((n_rows, w), F32)] * 2,
        scratch_shapes=[pltpu.VMEM((2, RW_HEADS // 2, LANES, LANES), F32)],
        compiler_params=_cparams(("arbitrary",)),
        name="rw_scan",
    )(com, dirs, com, dirs, tri)


def _rw_post_kernel(yf_ref, yb_ref, post_ref, avg_ref, vec_ref, o_ref):
    y = yf_ref[...] + yb_ref[...]
    avg = avg_ref[...]
    mu = _mm(y, avg, 3, 1)
    yc = y - mu
    var = _mm(yc * yc, avg, 3, 1)
    yn = yc * lax.rsqrt(var + RW_LN_EPS)
    vec = vec_ref[...]
    o_ref[...] = (yn * vec[0:1] + vec[1:2] + post_ref[1]) * post_ref[0]


def _rw_post(yf, yb, post, avg_bd, vec, row_off):
    n_rows = post.shape[1]
    w = RW_WIDTH
    tm = _row_tile(n_rows, 512)
    assert row_off % tm == 0
    ob = row_off // tm
    full = lambda a: pl.BlockSpec(a.shape, lambda i: (0,) * a.ndim)
    return pl.pallas_call(
        _rw_post_kernel,
        grid=(n_rows // tm,),
        in_specs=[pl.BlockSpec((tm, w), lambda i: (i + ob, 0)),
                  pl.BlockSpec((tm, w), lambda i: (i + ob, 0)),
                  pl.BlockSpec((2, tm, w), lambda i: (0, i, 0)),
                  full(avg_bd), full(vec)],
        out_specs=pl.BlockSpec((tm, w), lambda i: (i, 0)),
        out_shape=jax.ShapeDtypeStruct((n_rows, w), F32),
        compiler_params=_cparams(("parallel",)),
        name="rw_post",
    )(yf, yb, post, avg_bd, vec)


def _rms(x, n):
    return x * lax.rsqrt(jnp.sum(x * x, axis=-1, keepdims=True) * (1.0 / n) + NORM_EPS)


def _mla_prep_kernel(*refs, rope):
    if rope:
        u_ref, wq_ref, wk_ref, wv_ref, vq_ref, vkv_ref, gains_ref, rope_ref, q_ref, k_ref, vt_ref = refs
    else:
        u_ref, wq_ref, wk_ref, wv_ref, vq_ref, vkv_ref, gains_ref, q_ref, k_ref, vt_ref = refs
    u = u_ref[...]
    cq = _rms(u[:, :Q_LORA], Q_LORA) * vq_ref[...]
    q = jnp.dot(cq.astype(BF16), wq_ref[...], preferred_element_type=F32)
    ckv = _rms(u[:, Q_LORA:Q_LORA + KV_LORA], KV_LORA) * vkv_ref[...]
    ckv = ckv.astype(BF16)
    kn = jnp.dot(ckv, wk_ref[...], preferred_element_type=F32)
    vv = jnp.dot(ckv, wv_ref[...], preferred_element_type=F32)
    kr = pltpu.roll(u[:, Q_LORA + KV_LORA:], NOPE_DIM, 1)
    gains = gains_ref[...]
    lane = lax.broadcasted_iota(jnp.int32, (1, HEAD_PAD), 1)
    ones_col = (lane == V_DIM).astype(F32)

    def rot(t):
        if not rope:
            return t
        tab = rope_ref[...]
        half = ROPE_DIM // 4
        return (t * tab[0] + pltpu.roll(t, HEAD_PAD - half, 1) * tab[1]
                + pltpu.roll(t, half, 1) * tab[2])

    for h in range(MLA_HEADS):
        sl = slice(h * HEAD_PAD, (h + 1) * HEAD_PAD)
        qh = _rms(q[:, sl], QK_DIM) * gains[0:1]
        kh = _rms(kn[:, sl] + kr, QK_DIM) * gains[1:2]
        q_ref[h] = (rot(qh) + gains[2:3]).astype(BF16)
        k_ref[h] = (rot(kh) + gains[3:4]).astype(BF16)
        vt_ref[h] = (vv[:, sl] + ones_col).T[0:V_ROWS].astype(BF16)


def _mla_prep(u, col_block, wq, wk, wv, vq, vkv, gains, rope_tab):
    n_rows = u.shape[0]
    tm = _row_tile(n_rows, 512)
    rope = rope_tab is not None
    full = lambda a: pl.BlockSpec(a.shape, lambda i: (0,) * a.ndim)
    in_specs = [pl.BlockSpec((tm, MLA_PAD), lambda i: (i, col_block)),
                full(wq), full(wk), full(wv), full(vq), full(vkv), full(gains)]
    args = [u, wq, wk, wv, vq, vkv, gains]
    if rope:
        in_specs.append(pl.BlockSpec((3, tm, HEAD_PAD), lambda i: (0, i, 0)))
        args.append(rope_tab)
    hs = pl.BlockSpec((MLA_HEADS, tm, HEAD_PAD), lambda i: (0, i, 0))
    shp = jax.ShapeDtypeStruct((MLA_HEADS, n_rows, HEAD_PAD), BF16)
    vs = pl.BlockSpec((MLA_HEADS, V_ROWS, tm), lambda i: (0, 0, i))
    vshp = jax.ShapeDtypeStruct((MLA_HEADS, V_ROWS, n_rows), BF16)
    return pl.pallas_call(
        functools.partial(_mla_prep_kernel, rope=rope),
        grid=(n_rows // tm,),
        in_specs=in_specs,
        out_specs=[hs, hs, vs],
        out_shape=[shp, shp, vshp],
        compiler_params=_cparams(("parallel",)),
        name="mla_prep",
    )(*args)


def _attn_kernel(q_ref, k_ref, vt_ref, o_ref, acc_scr, *rest, nk, sub, online):
    kv = pl.program_id(2)
    if online:
        m_scr, = rest

    @pl.when(kv == 0)
    def _():
        acc_scr[...] = jnp.zeros_like(acc_scr)
        if online:
            m_scr[...] = jnp.full_like(m_scr, -jnp.inf)

    k = k_ref[0]
    vt = vt_ref[0]
    n = q_ref.shape[1] // sub
    st, p, alpha = {}, {}, {}
    for t in range(n + 2):
        if t < n:
            st[t] = lax.dot_general(k, q_ref[0, t * sub:(t + 1) * sub], NT, preferred_element_type=F32)
        if 0 <= t - 1 < n:
            r = t - 1
            cols = slice(r * sub, (r + 1) * sub)
            s_r = st.pop(r)
            if online:
                m_prev = m_scr[:, cols]
                m_new = jnp.maximum(m_prev, jnp.max(s_r, axis=0, keepdims=True))
                alpha[r] = jnp.exp2(m_prev - m_new)
                m_scr[:, cols] = m_new
                s_r = s_r - m_new
            p[r] = jnp.exp2(s_r).astype(BF16)
        if 0 <= t - 2 < n:
            r = t - 2
            cols = slice(r * sub, (r + 1) * sub)
            upd = jnp.dot(vt, p.pop(r), preferred_element_type=F32)
            prev = acc_scr[0:V_ROWS, cols]
            acc_scr[0:V_ROWS, cols] = (alpha.pop(r) * prev if online else prev) + upd

    @pl.when(kv == nk - 1)
    def _():
        acc = acc_scr[...]
        o_ref[...] = (acc / acc[V_DIM:V_DIM + 1]).T


def _attention_call(q, k, vt, online):
    h, lq, dp = q.shape
    lk = k.shape[1]
    tq = _row_tile(lq, 1024)
    sub = min(tq, 256)
    tk = lk
    for cand in (1280, 1024, 768, 512, 384, 256, 128):
        if lk % cand == 0:
            tk = cand
            break
    nk = lk // tk
    scratch = [pltpu.VMEM((HEAD_PAD, tq), F32)]
    if online:
        scratch.append(pltpu.VMEM((1, tq), F32))
    return pl.pallas_call(
        functools.partial(_attn_kernel, nk=nk, sub=sub, online=online),
        grid=(h, lq // tq, nk),
        in_specs=[pl.BlockSpec((1, tq, dp), lambda hh, i, j: (hh, i, 0)),
                  pl.BlockSpec((1, tk, dp), lambda hh, i, j: (hh, j, 0)),
                  pl.BlockSpec((1, V_ROWS, tk), lambda hh, i, j: (hh, 0, j))],
        out_specs=pl.BlockSpec((tq, dp), lambda hh, i, j: (i, hh)),
        out_shape=jax.ShapeDtypeStruct((lq, h * dp), F32),
        scratch_shapes=scratch,
        compiler_params=_cparams(("parallel", "parallel", "arbitrary")),
        name="attn_online" if online else "attn",
    )(q, k, vt)


def _attention(q, k, vt, score_bound):
    return lax.cond(2.0 * score_bound <= MAX_SCORE_SPAN,
                    lambda: _attention_call(q, k, vt, False),
                    lambda: _attention_call(q, k, vt, True))


def _ft_chan_kernel(u_ref, m_ref, o_ref):
    o_ref[...] = _mm(u_ref[...], m_ref[...], 2, 2)


def _ft_chan(u, col_block, mat):
    n_rows = u.shape[0]
    tm = _row_tile(n_rows, 512)
    return pl.pallas_call(
        _ft_chan_kernel,
        grid=(n_rows // tm,),
        in_specs=[pl.BlockSpec((tm, FT_WIDTH), lambda i: (i, col_block)),
                  pl.BlockSpec(mat.shape, lambda i: (0, 0))],
        out_specs=pl.BlockSpec((tm, 2 * FT_WIDTH), lambda i: (i, 0)),
        out_shape=jax.ShapeDtypeStruct((n_rows, 2 * FT_WIDTH), F32),
        compiler_params=_cparams(("parallel",)),
        name="ft_chan",
    )(u, mat)


def _ft_s1_kernel(w_ref, cs_ref, tw_ref, o_ref, *, l1, g):
    z = _mm(cs_ref[...], w_ref[...], 2, 2)
    fw = FT_WIDTH
    for t in range(g):
        zr = z[:, t * 2 * fw:t * 2 * fw + fw]
        zi = z[:, t * 2 * fw + fw:(t + 1) * 2 * fw]
        a_re = zr[:l1] + zi[l1:]
        a_im = zi[:l1] - zr[l1:]
        tc = jnp.concatenate([tw_ref[0, t]] * (fw // LANES), axis=1)
        ts = jnp.concatenate([tw_ref[1, t]] * (fw // LANES), axis=1)
        o_ref[:, t * 2 * fw:t * 2 * fw + fw] = a_re * tc + a_im * ts
        o_ref[:, t * 2 * fw + fw:(t + 1) * 2 * fw] = a_im * tc - a_re * ts


def _ft_s1(w2d, cs, tw, l1, l2):
    g = min(8, l2)
    blk = g * 2 * FT_WIDTH
    return pl.pallas_call(
        functools.partial(_ft_s1_kernel, l1=l1, g=g),
        grid=(l2 // g,),
        in_specs=[pl.BlockSpec((l1, blk), lambda i: (0, i)),
                  pl.BlockSpec(cs.shape, lambda i: (0, 0)),
                  pl.BlockSpec((2, g, l1, LANES), lambda i: (0, i, 0, 0))],
        out_specs=pl.BlockSpec((l1, blk), lambda i: (0, i)),
        out_shape=jax.ShapeDtypeStruct((l1, l2 * 2 * FT_WIDTH), F32),
        compiler_params=_cparams(("parallel",)),
        name="ft_s1",
    )(w2d, cs, tw)


def _ft_s2_kernel(b_ref, cs_ref, o_ref, *, g, scale):
    fw = FT_WIDTH
    cmat = cs_ref[0]
    smat = cs_ref[1]
    for kk in range(g):
        blk = b_ref[kk]
        out = _mm(cmat, blk[:, :fw], 2, 2) + _mm(smat, blk[:, fw:], 2, 2)
        o_ref[:, kk * fw:(kk + 1) * fw] = out * scale


def _ft_s2(b3d, cs2, l1, l2, scale):
    g = min(8, l1)
    return pl.pallas_call(
        functools.partial(_ft_s2_kernel, g=g, scale=scale),
        grid=(l1 // g,),
        in_specs=[pl.BlockSpec((g, l2, 2 * FT_WIDTH), lambda i: (i, 0, 0)),
                  pl.BlockSpec(cs2.shape, lambda i: (0, 0, 0))],
        out_specs=pl.BlockSpec((l2, g * FT_WIDTH), lambda i: (0, i)),
        out_shape=jax.ShapeDtypeStruct((l2, l1 * FT_WIDTH), F32),
        compiler_params=_cparams(("parallel",)),
        name="ft_s2",
    )(b3d, cs2)


def _ft_tables(n):
    l1 = 1 << (int(math.log2(n)) // 2)
    l2 = n // l1
    assert l1 * l2 == n and l1 % 8 == 0 and l2 % 8 == 0
    k1 = np.arange(l1)
    ang1 = 2.0 * np.pi * np.outer(k1, np.arange(l1)) / l1
    cs1 = np.concatenate([np.cos(ang1), np.sin(ang1)], axis=0)
    angt = 2.0 * np.pi * np.outer(np.arange(l2), k1) / n
    tw = np.stack([np.cos(angt), np.sin(angt)], axis=0)
    tw = np.broadcast_to(tw[..., None], (2, l2, l1, LANES))
    ang2 = 2.0 * np.pi * np.outer(np.arange(l2), np.arange(l2)) / l2
    cs2 = np.stack([np.cos(ang2), np.sin(ang2)], axis=0)
    f = lambda a: jnp.asarray(np.ascontiguousarray(a), F32)
    return l1, l2, f(cs1), f(tw), f(cs2)


def _chan_dft_matrix():
    n = FT_GROUP_DIM
    ang = 2.0 * np.pi * np.outer(np.arange(n), np.arange(n)) / n
    eye = np.eye(FT_GROUPS)
    mat = np.concatenate([np.kron(eye, np.cos(ang)), -np.kron(eye, np.sin(ang))], axis=1)
    return jnp.asarray(mat, F32)


def _fourier_mix(u, col_block, chan_mat):
    n = u.shape[0]
    l1, l2, cs1, tw, cs2 = _ft_tables(n)
    wc = _ft_chan(u, col_block, chan_mat)
    b = _ft_s1(wc.reshape(l1, l2 * 2 * FT_WIDTH), cs1, tw, l1, l2)
    scale = 1.0 / math.sqrt(n * FT_GROUP_DIM)
    out = _ft_s2(b.reshape(l1, l2, 2 * FT_WIDTH), cs2, l1, l2, scale)
    return out.reshape(n, FT_WIDTH)


def _merge_kernel(ya_ref, yb_ref, yc_ref, ga_ref, gb_ref, gc_ref, x_ref, gt_ref,
                  wa_ref, wb_ref, wc_ref, wo_ref, o_ref):
    def branch(y_ref, w_ref, g_ref):
        t = jnp.dot(y_ref[...].astype(BF16), w_ref[...], preferred_element_type=F32)
        return _sigmoid(g_ref[...]) * t

    m = branch(ya_ref, wa_ref, ga_ref) + branch(yb_ref, wb_ref, gb_ref) + branch(yc_ref, wc_ref, gc_ref)
    out = jnp.dot(m.astype(BF16), wo_ref[...], preferred_element_type=F32)
    o_ref[...] = x_ref[...] + gt_ref[...] * out


def _merge(ya, yb, yc, u, x, gt, wa, wb, wc, wo):
    n_rows, d = x.shape
    tm = _row_tile(n_rows, 512)
    full = lambda a: pl.BlockSpec(a.shape, lambda i: (0,) * a.ndim)
    rows = lambda a: pl.BlockSpec((tm, a.shape[1]), lambda i: (i, 0))
    gate = lambda b: pl.BlockSpec((tm, d), lambda i: (i, b))
    return pl.pallas_call(
        _merge_kernel,
        grid=(n_rows // tm,),
        in_specs=[rows(ya), rows(yb), rows(yc), gate(0), gate(1), gate(2), rows(x), full(gt),
                  full(wa), full(wb), full(wc), full(wo)],
        out_specs=pl.BlockSpec((tm, d), lambda i: (i, 0)),
        out_shape=jax.ShapeDtypeStruct((n_rows, d), F32),
        compiler_params=_cparams(("parallel",)),
        name="merge",
    )(ya, yb, yc, u, u, u, x, gt, wa, wb, wc, wo)


def _pad_heads(w, src_cols, width):
    k = w.shape[0]
    wh = w.reshape(k, MLA_HEADS, width)[:, :, src_cols]
    wh = jnp.pad(wh, ((0, 0), (0, 0), (0, HEAD_PAD - wh.shape[2])))
    return wh.reshape(k, MLA_HEADS * HEAD_PAD)


def _block_diag2(a, b):
    z01 = jnp.zeros((a.shape[0], b.shape[1]), a.dtype)
    z10 = jnp.zeros((b.shape[0], a.shape[1]), a.dtype)
    return jnp.concatenate([jnp.concatenate([a, z01], axis=1), jnp.concatenate([z10, b], axis=1)], axis=0)


def _rope_tables(n):
    half = ROPE_DIM // 2
    inv = ROPE_BASE ** (-np.arange(0, half, 2, dtype=np.float32) / half)
    t = np.arange(n)
    row = (t // GRID_W).astype(np.float32)
    col = (t % GRID_W).astype(np.float32)
    q = half // 2
    cos = np.ones((n, HEAD_PAD), np.float32)
    sin_a = np.zeros((n, HEAD_PAD), np.float32)
    sin_b = np.zeros((n, HEAD_PAD), np.float32)
    for base, pos in ((NOPE_DIM, row), (NOPE_DIM + half, col)):
        ang = pos[:, None] * inv[None, :]
        c, s = np.cos(ang), np.sin(ang)
        cos[:, base:base + q] = c
        cos[:, base + q:base + 2 * q] = c
        sin_a[:, base:base + q] = -s
        sin_b[:, base + q:base + 2 * q] = s
    return jnp.asarray(np.stack([cos, sin_a, sin_b], axis=0))


def _layer_weights(l, w):
    w_in = w["w_in"][l]
    o1 = RW_COLS
    o2 = o1 + MLA_COLS
    o3 = o2 + FT_WIDTH
    w_in_p = jnp.concatenate(
        [w_in[:, o3:], w_in[:, :o1], w_in[:, o1:o2],
         jnp.zeros((D_MODEL, MLA_PAD - MLA_COLS), F32), w_in[:, o2:o3]], axis=1).astype(BF16)
    ident = jnp.tile(jnp.array([[0.0], [1.0], [0.0]], F32), (1, IN_PAD))
    cw_in = ident.at[:, RW_OFF:RW_OFF + RW_COLS].set(w["rw_conv"][l])
    pad_gain = lambda g: jnp.pad(g, (0, HEAD_PAD - QK_DIM))
    wb = w["w_branch_b"][l].reshape(MLA_HEADS, V_DIM, D_MODEL)
    wb = jnp.pad(wb, ((0, 0), (0, HEAD_PAD - V_DIM), (0, 0))).reshape(MLA_HEADS * HEAD_PAD, D_MODEL)
    q_scale = math.log2(math.e) * QK_DIM ** -0.5
    score_bound = (BOUND_MARGIN * QK_DIM * q_scale * jnp.max(jnp.abs(w["q_gain"][l]))
                   * jnp.max(jnp.abs(w["k_gain"][l])))
    shift_lane = (jnp.arange(HEAD_PAD) == QK_DIM).astype(F32)
    return dict(
        w_in=w_in_p, cw_in=cw_in,
        wlora=_block_diag2(w["rw_w_up"][l, 0], w["rw_w_up"][l, 1]),
        alora=_block_diag2(w["rw_a_up"][l, 0], w["rw_a_up"][l, 1]),
        gup=w["rw_g_up"][l],
        rw_vec=jnp.stack([w["rw_k_k"][l], w["rw_k_a"][l], w["rw_r_k"][l].reshape(-1),
                          w["rw_w0"][l, 0], w["rw_w0"][l, 1], w["rw_a0"][l, 0], w["rw_a0"][l, 1],
                          jnp.zeros((RW_WIDTH,), F32)], axis=0),
        ln_vec=jnp.stack([w["rw_ln_w"][l], w["rw_ln_b"][l]], axis=0),
        wq=_pad_heads(w["w_uq"][l], slice(0, QK_DIM), QK_DIM).astype(BF16),
        wk=_pad_heads(w["w_ukv"][l], slice(0, NOPE_DIM), NOPE_DIM + V_DIM).astype(BF16),
        wv=_pad_heads(w["w_ukv"][l], slice(NOPE_DIM, NOPE_DIM + V_DIM), NOPE_DIM + V_DIM).astype(BF16),
        vq=w["mla_q_norm"][l][None], vkv=w["mla_kv_norm"][l][None],
        gains_lat=jnp.stack([pad_gain(w["q_gain"][l]) * q_scale, pad_gain(w["k_gain"][l]),
                             -score_bound * shift_lane, shift_lane], axis=0),
        score_bound=score_bound,
        wa=w["w_branch_a"][l].astype(BF16), wb=wb.astype(BF16), wc=w["w_branch_c"][l].astype(BF16),
        wo=w["w_out"][l].astype(BF16),
        w_up=w["w_up"][l].astype(BF16), cw_ffn=w["ffn_conv"][l], w_down=w["w_down"][l].astype(BF16),
    )


def _forward(x, c, ctx, c_ctx, w):
    depth = w["w_ada"].shape[0]
    x_lat = x[0]
    x_ctx = ctx[0]
    n_lat = x_lat.shape[0]
    n_ctx = x_ctx.shape[0]
    assert n_lat % CHUNK == 0 and n_ctx % CHUNK == 0 and n_lat % GRID_W == 0

    cc = jnp.zeros((8, D_MODEL), F32).at[0].set(c[0]).at[1].set(c_ctx)
    mods = _mods(cc, w["w_ada"], w["b_ada"])

    head = np.arange(RW_WIDTH) // RW_HEAD_DIM
    ones_bd = jnp.asarray((head[:, None] == head[None, :]).astype(np.float32))
    avg_bd = ones_bd * (1.0 / RW_HEAD_DIM)
    ti = np.arange(CHUNK)
    tri = jnp.asarray(np.stack([ti[None, :] <= ti[:, None], ti[None, :] >= ti[:, None]]).astype(np.float32),
                      BF16)
    chan_mat = _chan_dft_matrix()
    rope_tab = _rope_tables(n_lat)

    for l in range(depth):
        last = l == depth - 1
        p = _layer_weights(l, w)
        streams = [(x_lat, mods[l, 0], rope_tab)]
        if True:
            streams.append((x_ctx, mods[l, 1], None))
        us, preps, qkv = [], [], []
        for xs, mod, rt in streams:
            sh1, sc1 = mod[0:D_MODEL][None], mod[D_MODEL:2 * D_MODEL][None]
            gain1 = w["g_norm1"][l][None] * (1.0 + sc1)
            u = _in_proj(xs, gain1, sh1, p["w_in"], p["cw_in"])
            us.append(u)
            preps.append(_rw_prep(u, RW_OFF // RW_COLS, p["wlora"], p["alora"], p["gup"], ones_bd, p["rw_vec"]))
            qkv.append(_mla_prep(u, MLA_OFF // MLA_PAD, p["wq"], p["wk"], p["wv"], p["vq"], p["vkv"],
                                 p["gains_lat"], rt))
        com = jnp.concatenate([preps[0][0], preps[1][0]], axis=1)
        dirs = jnp.concatenate([preps[0][1], preps[1][1]], axis=2)
        yf, yb = _rw_scan(com, dirs, tri, n_ctx // CHUNK)
        k_all = jnp.concatenate([qkv[0][1], qkv[1][1]], axis=1)
        vt_all = jnp.concatenate([qkv[0][2], qkv[1][2]], axis=2)

        new = []
        for si, (xs, mod, rt) in enumerate(streams):
            if si == 1 and last:
                continue
            row_off = 0 if si == 0 else n_lat
            ya = _rw_post(yf, yb, preps[si][2], avg_bd, p["ln_vec"], row_off)
            if si == 0:
                yb_att = _attention(qkv[0][0], k_all, vt_all, p["score_bound"])
            else:
                yb_att = _attention(qkv[1][0], qkv[1][1], qkv[1][2], p["score_bound"])
            yc = _fourier_mix(us[si], FT_OFF // FT_WIDTH, chan_mat)
            gt1 = mod[2 * D_MODEL:3 * D_MODEL][None]
            x1 = _merge(ya, yb_att, yc, us[si], xs, gt1, p["wa"], p["wb"], p["wc"], p["wo"])
            sh2, sc2, gt2 = (mod[3 * D_MODEL:4 * D_MODEL][None], mod[4 * D_MODEL:5 * D_MODEL][None],
                             mod[5 * D_MODEL:6 * D_MODEL][None])
            gain2 = w["g_norm2"][l][None] * (1.0 + sc2)
            new.append(_ffn(x1, gain2, sh2, gt2, p["w_up"], p["cw_ffn"], p["w_down"]))
        x_lat = new[0]
        if not last:
            x_ctx = new[1]
    return x_lat[None]


def kernel(x, c, ctx, c_ctx, w_ada, b_ada, g_norm1, g_norm2, w_in, rw_conv, rw_w0, rw_w_up, rw_a0, rw_a_up,
           rw_g_up, rw_k_k, rw_k_a, rw_r_k, rw_ln_w, rw_ln_b, mla_q_norm, w_uq, mla_kv_norm, w_ukv, q_gain,
           k_gain, w_branch_a, w_branch_b, w_branch_c, w_out, ffn_conv, w_up, w_down):
    w = dict(w_ada=w_ada, b_ada=b_ada, g_norm1=g_norm1, g_norm2=g_norm2, w_in=w_in, rw_conv=rw_conv,
             rw_w0=rw_w0, rw_w_up=rw_w_up, rw_a0=rw_a0, rw_a_up=rw_a_up, rw_g_up=rw_g_up, rw_k_k=rw_k_k,
             rw_k_a=rw_k_a, rw_r_k=rw_r_k, rw_ln_w=rw_ln_w, rw_ln_b=rw_ln_b, mla_q_norm=mla_q_norm,
             w_uq=w_uq, mla_kv_norm=mla_kv_norm, w_ukv=w_ukv, q_gain=q_gain, k_gain=k_gain,
             w_branch_a=w_branch_a, w_branch_b=w_branch_b, w_branch_c=w_branch_c, w_out=w_out,
             ffn_conv=ffn_conv, w_up=w_up, w_down=w_down)
    return _forward(x, c, ctx, c_ctx, w)
```

```python
import functools
import math

import numpy as np
import jax
import jax.numpy as jnp
from jax import lax
from jax.experimental import pallas as pl
from jax.experimental.pallas import tpu as pltpu

F32 = jnp.float32
BF16 = jnp.bfloat16

D_MODEL = 1024
GRID_W = 64
NORM_EPS = 1e-6
RW_HEADS = 6
RW_HEAD_DIM = 64
RW_WIDTH = RW_HEADS * RW_HEAD_DIM
DECAY_LORA = 64
ICLR_LORA = 64
GATE_LORA = 128
RW_LN_EPS = 64e-5
MLA_HEADS = 6
Q_LORA = 384
KV_LORA = 256
NOPE_DIM = 64
ROPE_DIM = 32
V_DIM = 64
QK_DIM = NOPE_DIM + ROPE_DIM
ROPE_BASE = 10000.0
FT_GROUPS = 4
FT_GROUP_DIM = 64
FT_WIDTH = FT_GROUPS * FT_GROUP_DIM
D_FF = 2816
N_MOD = 6
RW_COLS = 3 * RW_WIDTH + 2 * DECAY_LORA + 2 * ICLR_LORA + GATE_LORA
MLA_COLS = Q_LORA + KV_LORA + ROPE_DIM

LANES = 128
HEAD_PAD = 128
MLA_PAD = 768
GATE_OFF = 0
RW_OFF = 3 * D_MODEL
MLA_OFF = RW_OFF + RW_COLS
FT_OFF = MLA_OFF + MLA_PAD
IN_PAD = FT_OFF + FT_WIDTH
HALO = 16
CHUNK = 128
VMEM_LIMIT = 48 * 1024 * 1024
V_ROWS = 80
BOUND_MARGIN = 1.02
MAX_SCORE_SPAN = 100.0
ATT_TQ, ATT_SUB, ATT_TK = 2048, 256, 3328

NN = (((1,), (0,)), ((), ()))
NT = (((1,), (1,)), ((), ()))


def _pieces(x, n):
    if x.dtype == BF16:
        return [x]
    out = []
    rem = x
    for i in range(n):
        p = rem.astype(BF16)
        out.append(p)
        if i + 1 < n:
            rem = rem - p.astype(F32)
    return out


def _mm(a, b, pa=1, pb=1, dims=NN):
    ap = _pieces(a, pa)
    bp = _pieces(b, pb)
    n = max(len(ap), len(bp))
    acc = None
    for i in reversed(range(len(ap))):
        for j in reversed(range(len(bp))):
            if i + j < n:
                t = lax.dot_general(ap[i], bp[j], dims, preferred_element_type=F32)
                acc = t if acc is None else acc + t
    return acc


def _sigmoid(x):
    return 1.0 / (1.0 + jnp.exp(-x))


def _cparams(sem):
    return pltpu.CompilerParams(dimension_semantics=sem, vmem_limit_bytes=VMEM_LIMIT)


def _row_tile(n, pref):
    t = min(n, pref)
    while n % t:
        t //= 2
    return t


def _mods_kernel(cc_ref, w_ref, b_ref, o_ref):
    cc = cc_ref[...]
    s = cc * _sigmoid(cc)
    o_ref[0] = _mm(s, w_ref[0], 2, 2) + b_ref[0]


def _mods(cc, w_ada, b_ada):
    depth, d, n = w_ada.shape
    tn = 1536
    return pl.pallas_call(
        _mods_kernel,
        grid=(depth, n // tn),
        in_specs=[pl.BlockSpec((8, d), lambda l, j: (0, 0)),
                  pl.BlockSpec((1, d, tn), lambda l, j: (l, 0, j)),
                  pl.BlockSpec((1, 1, tn), lambda l, j: (l, 0, j))],
        out_specs=pl.BlockSpec((1, 8, tn), lambda l, j: (l, 0, j)),
        out_shape=jax.ShapeDtypeStruct((depth, 8, n), F32),
        compiler_params=_cparams(("parallel", "parallel")),
        name="mods",
    )(cc, w_ada, b_ada.reshape(depth, 1, n))


def _modnorm(xv, g, s):
    ms = jnp.mean(xv * xv, axis=-1, keepdims=True)
    return (xv * lax.rsqrt(ms + NORM_EPS)) * g + s


def _fill_normed(h_scr, x_ref, xp_ref, xn_ref, g, s, i, nrt, tm):
    h_scr[HALO:HALO + tm] = _modnorm(x_ref[...], g, s).astype(BF16)
    hp = jnp.where(i > 0, _modnorm(xp_ref[...], g, s), 0.0)
    h_scr[0:HALO] = hp.astype(BF16)
    hn = jnp.where(i < nrt - 1, _modnorm(xn_ref[...], g, s), 0.0)
    h_scr[HALO + tm:2 * HALO + tm] = hn.astype(BF16)


def _conv3(u, cw, tm):
    rows = u.shape[0]
    up = pltpu.roll(u, 1, 0)
    un = pltpu.roll(u, rows - 1, 0)
    out = cw[0:1] * up + cw[1:2] * u + cw[2:3] * un
    return out[HALO:HALO + tm]


def _halo_specs(tm, d, n_rows):
    r = tm // HALO
    last = n_rows // HALO - 1
    return [pl.BlockSpec((tm, d), lambda i: (i, 0)),
            pl.BlockSpec((HALO, d), lambda i: (jnp.maximum(i * r - 1, 0), 0)),
            pl.BlockSpec((HALO, d), lambda i: (jnp.minimum((i + 1) * r, last), 0))]


def _resident(a):
    return pl.BlockSpec(a.shape, lambda i: (0,) * a.ndim, pipeline_mode=pl.Buffered(1))


def _in_kernel(x_ref, xp_ref, xn_ref, g_ref, s_ref, w_ref, cw_ref, o_ref, h_scr, *, tm, nrt, tn):
    _fill_normed(h_scr, x_ref, xp_ref, xn_ref, g_ref[...], s_ref[...], pl.program_id(0), nrt, tm)
    h = h_scr[...]
    nj = w_ref.shape[1] // tn
    u = {}
    for t in range(nj + 1):
        if t < nj:
            u[t] = jnp.dot(h, w_ref[:, t * tn:(t + 1) * tn], preferred_element_type=F32)
        if t >= 1:
            cols = slice((t - 1) * tn, t * tn)
            o_ref[:, cols] = _conv3(u.pop(t - 1), cw_ref[:, cols], tm)


def _in_proj(x, gain, shift, w, cw):
    n_rows, d = x.shape
    n = w.shape[1]
    tm = _row_tile(n_rows, 512)
    nrt = n_rows // tm
    return pl.pallas_call(
        functools.partial(_in_kernel, tm=tm, nrt=nrt, tn=512),
        grid=(nrt,),
        in_specs=_halo_specs(tm, d, n_rows) + [_resident(gain), _resident(shift), _resident(w), _resident(cw)],
        out_specs=pl.BlockSpec((tm, n), lambda i: (i, 0)),
        out_shape=jax.ShapeDtypeStruct((n_rows, n), F32),
        scratch_shapes=[pltpu.VMEM((tm + 2 * HALO, d), BF16)],
        compiler_params=_cparams(("parallel",)),
        name="in_proj",
    )(x, x, x, gain, shift, w, cw)


def _ffn_kernel(x_ref, xp_ref, xn_ref, g_ref, s_ref, gt_ref, wu_ref, cw_ref, wd_ref, o_ref, h_scr, *,
                tm, nrt, tf):
    _fill_normed(h_scr, x_ref, xp_ref, xn_ref, g_ref[...], s_ref[...], pl.program_id(0), nrt, tm)
    h = h_scr[...]
    nj = D_FF // tf
    u, gated = {}, {}
    for t in range(nj + 2):
        if t < nj:
            ca = slice(t * tf, (t + 1) * tf)
            cb = slice(D_FF + t * tf, D_FF + (t + 1) * tf)
            u[t] = (jnp.dot(h, wu_ref[:, ca], preferred_element_type=F32),
                    jnp.dot(h, wu_ref[:, cb], preferred_element_type=F32))
        if 0 <= t - 1 < nj:
            j = t - 1
            ua, ub = u.pop(j)
            a = _conv3(ua, cw_ref[:, j * tf:(j + 1) * tf], tm)
            b = _conv3(ub, cw_ref[:, D_FF + j * tf:D_FF + (j + 1) * tf], tm)
            gated[j] = (a * _sigmoid(a) * b).astype(BF16)
        if 0 <= t - 2 < nj:
            j = t - 2
            down = jnp.dot(gated.pop(j), wd_ref[j * tf:(j + 1) * tf, :], preferred_element_type=F32)
            if j == 0:
                o_ref[...] = down
            else:
                o_ref[...] += down
    o_ref[...] = x_ref[...] + gt_ref[...] * o_ref[...]


def _ffn(x, gain, shift, gate, w_up, cw, w_down):
    n_rows, d = x.shape
    tm = _row_tile(n_rows, 512)
    nrt = n_rows // tm
    return pl.pallas_call(
        functools.partial(_ffn_kernel, tm=tm, nrt=nrt, tf=256),
        grid=(nrt,),
        in_specs=_halo_specs(tm, d, n_rows) + [_resident(gain), _resident(shift), _resident(gate),
                                               _resident(w_up), _resident(cw), _resident(w_down)],
        out_specs=pl.BlockSpec((tm, d), lambda i: (i, 0)),
        out_shape=jax.ShapeDtypeStruct((n_rows, d), F32),
        scratch_shapes=[pltpu.VMEM((tm + 2 * HALO, d), BF16)],
        compiler_params=_cparams(("parallel",)),
        name="ffn",
    )(x, x, x, gain, shift, gate, w_up, cw, w_down)


def _seg_sum(x, ones_bd):
    return _mm(x, ones_bd, 3, 1)


def _rw_prep_kernel(u_ref, wlora_ref, alora_ref, gup_ref, ones_ref, vec_ref,
                    com_ref, dir_ref, post_ref):
    u = u_ref[...]
    w = RW_WIDTH
    r = u[:, 0:w]
    k = u[:, w:2 * w]
    v = u[:, 2 * w:3 * w]
    wd = u[:, 3 * w:3 * w + 2 * DECAY_LORA]
    ad = u[:, 3 * w + 2 * DECAY_LORA:3 * w + 2 * DECAY_LORA + 2 * ICLR_LORA]
    gd = u[:, 3 * w + 2 * DECAY_LORA + 2 * ICLR_LORA:]
    vec = vec_ref[...]
    k_k, k_a, r_k = vec[0:1], vec[1:2], vec[2:3]
    ones_bd = ones_ref[...]

    kk = k * k_k
    kk = kk * lax.rsqrt(_seg_sum(kk * kk, ones_bd) + 1e-12)
    g = _mm(_sigmoid(gd), gup_ref[...], 2, 2)
    wl = _mm(jnp.tanh(wd), wlora_ref[...], 2, 2)
    al = _mm(ad, alora_ref[...], 2, 2)
    com_ref[0] = r
    com_ref[1] = v
    com_ref[2] = -kk
    kd_sum = None
    for d in range(2):
        w0 = vec[3 + d:4 + d]
        a0 = vec[5 + d:6 + d]
        lw = -math.exp(-0.5) * _sigmoid(w0 + wl[:, d * w:(d + 1) * w])
        a = _sigmoid(a0 + al[:, d * w:(d + 1) * w])
        kd = k * (1.0 + (a - 1.0) * k_a)
        dir_ref[d, 0] = lw
        dir_ref[d, 1] = kd
        dir_ref[d, 2] = kk * a
        kd_sum = kd if kd_sum is None else kd_sum + kd
    post_ref[0] = g
    post_ref[1] = _seg_sum(r * kd_sum * r_k, ones_bd) * v


def _rw_prep(u, col_block, wlora, alora, gup, ones_bd, vec):
    n_rows = u.shape[0]
    tm = _row_tile(n_rows, 512)
    w = RW_WIDTH
    full = lambda a: pl.BlockSpec(a.shape, lambda i: (0,) * a.ndim)
    return pl.pallas_call(
        _rw_prep_kernel,
        grid=(n_rows // tm,),
        in_specs=[pl.BlockSpec((tm, RW_COLS), lambda i: (i, col_block)),
                  full(wlora), full(alora), full(gup), full(ones_bd), full(vec)],
        out_specs=[pl.BlockSpec((3, tm, w), lambda i: (0, i, 0)),
                   pl.BlockSpec((2, 3, tm, w), lambda i: (0, 0, i, 0)),
                   pl.BlockSpec((2, tm, w), lambda i: (0, i, 0))],
        out_shape=[jax.ShapeDtypeStruct((3, n_rows, w), F32),
                   jax.ShapeDtypeStruct((2, 3, n_rows, w), F32),
                   jax.ShapeDtypeStruct((2, n_rows, w), F32)],
        compiler_params=_cparams(("parallel",)),
        name="rw_prep",
    )(u, wlora, alora, gup, ones_bd, vec)


def _smm(a, b, dims=NN):
    return _mm(a, b, 1, 1, dims)


def _scan_dir(com, dr, tri, slot, reverse):
    c = CHUNK
    r, v, a = com[0], com[1], com[2]
    lw, kd, b = dr[0], dr[1], dr[2]
    cs = _mm(tri, lw, 1, 3)
    tot = cs[0:1] if reverse else cs[c - 1:c]
    e_pos = jnp.exp(cs)
    e_neg = jnp.exp(-cs)
    e_end = jnp.exp(tot - cs)
    at = a * jnp.exp(cs - lw)
    rt = r * e_pos
    bt = b * e_neg
    kt = kd * e_neg
    bh = b * e_end
    kh = kd * e_end
    p_end = jnp.exp(tot)

    row = lax.broadcasted_iota(jnp.int32, (c, c), 0)
    col = lax.broadcasted_iota(jnp.int32, (c, c), 1)
    before = (col > row) if reverse else (col < row)
    upto = (col >= row) if reverse else (col <= row)
    eye = row == col
    lane = lax.broadcasted_iota(jnp.int32, (c, LANES), 1)
    same_head = (row // RW_HEAD_DIM) == (col // RW_HEAD_DIM)

    pairs = []
    for p in range(RW_HEADS // 2):
        sl = slice(p * LANES, (p + 1) * LANES)
        pairs.append(dict(
            slot=slot, p=p, at=at[:, sl], rt=rt[:, sl], v=v[:, sl].astype(BF16), v_t=v[:, sl].T,
            bk=jnp.concatenate([bt[:, sl], kt[:, sl]], axis=0).astype(BF16),
            bh=bh[:, sl].astype(BF16), kh=kh[:, sl].astype(BF16), p_end=p_end[:, sl],
            masks=(before, upto, eye, same_head),
            heads=[(lane // RW_HEAD_DIM) == h for h in range(2)]))
    return pairs


def _scan_chunk(pairs, s_ref):
    c = CHUNK
    chains = []
    for pr in pairs:
        before, upto, eye, _ = pr["masks"]
        for mh in pr["heads"]:
            ar = jnp.concatenate([jnp.where(mh, pr["at"], 0.0), jnp.where(mh, pr["rt"], 0.0)], axis=0)
            gm = _smm(ar, pr["bk"], NT)
            a_ab = jnp.where(before, gm[:c, :c], 0.0)
            chains.append(dict(
                pr=pr, mh=mh, apow=a_ab, t_inv=jnp.where(eye, 1.0, a_ab),
                a_ak=jnp.where(before, gm[:c, c:], 0.0).astype(BF16),
                a_rbk=jnp.concatenate([jnp.where(upto, gm[c:, :c], 0.0),
                                       jnp.where(upto, gm[c:, c:], 0.0)], axis=1).astype(BF16)))
    for ch in chains:
        ch["apow"] = ch["apow"].astype(BF16)
    for _ in range(int(math.log2(c)) - 1):
        for ch in chains:
            ch["apow"] = _smm(ch["apow"], ch["apow"]).astype(BF16)
        for ch in chains:
            ch["t_inv"] = ch["t_inv"] + _smm(ch["t_inv"], ch["apow"])
    for pr in pairs:
        s0 = s_ref[pr["slot"], pr["p"]]
        pr["s0"] = s0
        hr = _smm(jnp.concatenate([pr["at"], pr["rt"]], axis=0), s0, NT)
        pr["ah"], pr["rh"] = hr[:c], hr[c:]
        pr["u"] = None
    for ch in chains:
        pr = ch["pr"]
        x = _smm(ch["a_ak"], pr["v"])
        u_h = jnp.where(ch["mh"], _smm(ch["t_inv"], pr["ah"] + x), 0.0)
        pr["u"] = u_h if pr["u"] is None else pr["u"] + u_h
    outs = {}
    for pr in pairs:
        pr["uv"] = jnp.concatenate([pr["u"].astype(BF16), pr["v"]], axis=0)
        pr["y"] = pr["rh"]
    for ch in chains:
        pr = ch["pr"]
        pr["y"] = pr["y"] + jnp.where(ch["mh"], _smm(ch["a_rbk"], pr["uv"]), 0.0)
    for pr in pairs:
        same_head = pr["masks"][3]
        bkh = jnp.concatenate([pr["bh"], pr["kh"]], axis=0)
        uv_t = jnp.concatenate([pr["u"].T, pr["v_t"]], axis=1)
        upd = _smm(uv_t, bkh)
        s_ref[pr["slot"], pr["p"]] = pr["s0"] * pr["p_end"] + jnp.where(same_head, upd, 0.0)
        outs[(pr["slot"], pr["p"])] = pr["y"]
    n_pairs = RW_HEADS // 2
    return [jnp.concatenate([outs[(d, p)] for p in range(n_pairs)], axis=1) for d in range(2)]


def _rw_scan_kernel(comf_ref, dirf_ref, comb_ref, dirb_ref, tri_ref, yf_ref, yb_ref, s_ref):
    @pl.when(pl.program_id(0) == 0)
    def _():
        s_ref[...] = jnp.zeros_like(s_ref)

    pairs = (_scan_dir(comf_ref[...], dirf_ref[0], tri_ref[0], 0, False)
             + _scan_dir(comb_ref[...], dirb_ref[0], tri_ref[1], 1, True))
    yf, yb = _scan_chunk(pairs, s_ref)
    yf_ref[...] = yf
    yb_ref[...] = yb


def _rw_scan(com, dirs, tri, n_ctx_chunks):
    n_rows = com.shape[1]
    w = RW_WIDTH
    c = CHUNK
    assert c == LANES
    n_chunks = n_rows // c
    nc = n_ctx_chunks
    nl = n_chunks - nc

    def fwd_block(i):
        return jnp.where(i < nc, nl + i, i - nc)

    def bwd_block(i):
        return jnp.where(i < nc, nl + nc - 1 - i, nl - 1 - (i - nc))

    return pl.pallas_call(
        _rw_scan_kernel,
        grid=(n_chunks,),
        in_specs=[pl.BlockSpec((3, c, w), lambda i: (0, fwd_block(i), 0)),
                  pl.BlockSpec((1, 3, c, w), lambda i: (0, 0, fwd_block(i), 0)),
                  pl.BlockSpec((3, c, w), lambda i: (0, bwd_block(i), 0)),
                  pl.BlockSpec((1, 3, c, w), lambda i: (1, 0, bwd_block(i), 0)),
                  pl.BlockSpec((2, c, c), lambda i: (0, 0, 0))],
        out_specs=[pl.BlockSpec((c, w), lambda i: (fwd_block(i), 0)),
                   pl.BlockSpec((c, w), lambda i: (bwd_block(i), 0))],
        out_shape=[jax.ShapeDtypeStruct((n_rows, w), F32)] * 2,
        scratch_shapes=[pltpu.VMEM((2, RW_HEADS // 2, LANES, LANES), F32)],
        compiler_params=_cparams(("arbitrary",)),
        name="rw_scan",
    )(com, dirs, com, dirs, tri)


def _rw_post_kernel(yf_ref, yb_ref, post_ref, avg_ref, vec_ref, o_ref):
    y = yf_ref[...] + yb_ref[...]
    avg = avg_ref[...]
    mu = _mm(y, avg, 3, 1)
    yc = y - mu
    var = _mm(yc * yc, avg, 3, 1)
    yn = yc * lax.rsqrt(var + RW_LN_EPS)
    vec = vec_ref[...]
    o_ref[...] = (yn * vec[0:1] + vec[1:2] + post_ref[1]) * post_ref[0]


def _rw_post(yf, yb, post, avg_bd, vec, row_off):
    n_rows = post.shape[1]
    w = RW_WIDTH
    tm = _row_tile(n_rows, 512)
    assert row_off % tm == 0
    ob = row_off // tm
    full = lambda a: pl.BlockSpec(a.shape, lambda i: (0,) * a.ndim)
    return pl.pallas_call(
        _rw_post_kernel,
        grid=(n_rows // tm,),
        in_specs=[pl.BlockSpec((tm, w), lambda i: (i + ob, 0)),
                  pl.BlockSpec((tm, w), lambda i: (i + ob, 0)),
                  pl.BlockSpec((2, tm, w), lambda i: (0, i, 0)),
                  full(avg_bd), full(vec)],
        out_specs=pl.BlockSpec((tm, w), lambda i: (i, 0)),
        out_shape=jax.ShapeDtypeStruct((n_rows, w), F32),
        compiler_params=_cparams(("parallel",)),
        name="rw_post",
    )(yf, yb, post, avg_bd, vec)


def _rms(x, n):
    return x * lax.rsqrt(jnp.sum(x * x, axis=-1, keepdims=True) * (1.0 / n) + NORM_EPS)


def _mla_prep_kernel(*refs, rope):
    if rope:
        u_ref, wq_ref, wk_ref, wv_ref, vq_ref, vkv_ref, gains_ref, rope_ref, q_ref, k_ref, vt_ref = refs
    else:
        u_ref, wq_ref, wk_ref, wv_ref, vq_ref, vkv_ref, gains_ref, q_ref, k_ref, vt_ref = refs
    u = u_ref[...]
    cq = _rms(u[:, :Q_LORA], Q_LORA) * vq_ref[...]
    q = jnp.dot(cq.astype(BF16), wq_ref[...], preferred_element_type=F32)
    ckv = _rms(u[:, Q_LORA:Q_LORA + KV_LORA], KV_LORA) * vkv_ref[...]
    ckv = ckv.astype(BF16)
    kn = jnp.dot(ckv, wk_ref[...], preferred_element_type=F32)
    vv = jnp.dot(ckv, wv_ref[...], preferred_element_type=F32)
    kr = pltpu.roll(u[:, Q_LORA + KV_LORA:], NOPE_DIM, 1)
    gains = gains_ref[...]
    lane = lax.broadcasted_iota(jnp.int32, (1, HEAD_PAD), 1)
    ones_col = (lane == V_DIM).astype(F32)

    def rot(t):
        if not rope:
            return t
        tab = rope_ref[...]
        half = ROPE_DIM // 4
        return (t * tab[0] + pltpu.roll(t, HEAD_PAD - half, 1) * tab[1]
                + pltpu.roll(t, half, 1) * tab[2])

    for h in range(MLA_HEADS):
        sl = slice(h * HEAD_PAD, (h + 1) * HEAD_PAD)
        qh = _rms(q[:, sl], QK_DIM) * gains[0:1]
        kh = _rms(kn[:, sl] + kr, QK_DIM) * gains[1:2]
        q_ref[h] = (rot(qh) + gains[2:3]).astype(BF16)
        k_ref[h] = (rot(kh) + gains[3:4]).astype(BF16)
        vt_ref[h] = (vv[:, sl] + ones_col).T[0:V_ROWS].astype(BF16)


def _mla_prep(u, col_block, wq, wk, wv, vq, vkv, gains, rope_tab):
    n_rows = u.shape[0]
    tm = _row_tile(n_rows, 512)
    rope = rope_tab is not None
    full = lambda a: pl.BlockSpec(a.shape, lambda i: (0,) * a.ndim)
    in_specs = [pl.BlockSpec((tm, MLA_PAD), lambda i: (i, col_block)),
                full(wq), full(wk), full(wv), full(vq), full(vkv), full(gains)]
    args = [u, wq, wk, wv, vq, vkv, gains]
    if rope:
        in_specs.append(pl.BlockSpec((3, tm, HEAD_PAD), lambda i: (0, i, 0)))
        args.append(rope_tab)
    hs = pl.BlockSpec((MLA_HEADS, tm, HEAD_PAD), lambda i: (0, i, 0))
    shp = jax.ShapeDtypeStruct((MLA_HEADS, n_rows, HEAD_PAD), BF16)
    vs = pl.BlockSpec((MLA_HEADS, V_ROWS, tm), lambda i: (0, 0, i))
    vshp = jax.ShapeDtypeStruct((MLA_HEADS, V_ROWS, n_rows), BF16)
    return pl.pallas_call(
        functools.partial(_mla_prep_kernel, rope=rope),
        grid=(n_rows // tm,),
        in_specs=in_specs,
        out_specs=[hs, hs, vs],
        out_shape=[shp, shp, vshp],
        compiler_params=_cparams(("parallel",)),
        name="mla_prep",
    )(*args)


def _attn_kernel(q_ref, k_ref, vt_ref, o_ref, acc_scr, *rest, nk, sub, online):
    kv = pl.program_id(2)
    if online:
        m_scr, = rest

    @pl.when(kv == 0)
    def _():
        acc_scr[...] = jnp.zeros_like(acc_scr)
        if online:
            m_scr[...] = jnp.full_like(m_scr, -jnp.inf)

    k = k_ref[0]
    vt = vt_ref[0]
    n = q_ref.shape[1] // sub
    st, p, alpha = {}, {}, {}
    for t in range(n + 2):
        if t < n:
            st[t] = lax.dot_general(k, q_ref[0, t * sub:(t + 1) * sub], NT, preferred_element_type=F32)
        if 0 <= t - 1 < n:
            r = t - 1
            cols = slice(r * sub, (r + 1) * sub)
            s_r = st.pop(r)
            if online:
                m_prev = m_scr[:, cols]
                m_new = jnp.maximum(m_prev, jnp.max(s_r, axis=0, keepdims=True))
                alpha[r] = jnp.exp2(m_prev - m_new)
                m_scr[:, cols] = m_new
                s_r = s_r - m_new
            p[r] = jnp.exp2(s_r).astype(BF16)
        if 0 <= t - 2 < n:
            r = t - 2
            cols = slice(r * sub, (r + 1) * sub)
            upd = jnp.dot(vt, p.pop(r), preferred_element_type=F32)
            prev = acc_scr[0:V_ROWS, cols]
            acc_scr[0:V_ROWS, cols] = (alpha.pop(r) * prev if online else prev) + upd

    @pl.when(kv == nk - 1)
    def _():
        acc = acc_scr[...]
        o_ref[...] = (acc / acc[V_DIM:V_DIM + 1]).T


def _attention_call(q, k, vt, online):
    h, lq, dp = q.shape
    lk = k.shape[1]
    tq = _row_tile(lq, ATT_TQ)
    sub = min(tq, ATT_SUB)
    tk = max(t for t in range(LANES, min(lk, ATT_TK) + 1, LANES) if lk % t == 0)
    nk = lk // tk
    scratch = [pltpu.VMEM((HEAD_PAD, tq), F32)]
    if online:
        scratch.append(pltpu.VMEM((1, tq), F32))
    return pl.pallas_call(
        functools.partial(_attn_kernel, nk=nk, sub=sub, online=online),
        grid=(h, lq // tq, nk),
        in_specs=[pl.BlockSpec((1, tq, dp), lambda hh, i, j: (hh, i, 0)),
                  pl.BlockSpec((1, tk, dp), lambda hh, i, j: (hh, j, 0)),
                  pl.BlockSpec((1, V_ROWS, tk), lambda hh, i, j: (hh, 0, j))],
        out_specs=pl.BlockSpec((tq, dp), lambda hh, i, j: (i, hh)),
        out_shape=jax.ShapeDtypeStruct((lq, h * dp), F32),
        scratch_shapes=scratch,
        compiler_params=_cparams(("parallel", "parallel", "arbitrary")),
        name="attn_online" if online else "attn",
    )(q, k, vt)


def _attention(q, k, vt, score_bound):
    return lax.cond(2.0 * score_bound <= MAX_SCORE_SPAN,
                    lambda: _attention_call(q, k, vt, False),
                    lambda: _attention_call(q, k, vt, True))


def _ft_chan_kernel(u_ref, m_ref, o_ref):
    o_ref[...] = _mm(u_ref[...], m_ref[...], 2, 2)


def _ft_chan(u, col_block, mat):
    n_rows = u.shape[0]
    tm = _row_tile(n_rows, 512)
    return pl.pallas_call(
        _ft_chan_kernel,
        grid=(n_rows // tm,),
        in_specs=[pl.BlockSpec((tm, FT_WIDTH), lambda i: (i, col_block)),
                  pl.BlockSpec(mat.shape, lambda i: (0, 0))],
        out_specs=pl.BlockSpec((tm, 2 * FT_WIDTH), lambda i: (i, 0)),
        out_shape=jax.ShapeDtypeStruct((n_rows, 2 * FT_WIDTH), F32),
        compiler_params=_cparams(("parallel",)),
        name="ft_chan",
    )(u, mat)


def _ft_s1_kernel(w_ref, cs_ref, tw_ref, o_ref, *, l1, g):
    z = _mm(cs_ref[...], w_ref[...], 2, 2)
    fw = FT_WIDTH
    for t in range(g):
        zr = z[:, t * 2 * fw:t * 2 * fw + fw]
        zi = z[:, t * 2 * fw + fw:(t + 1) * 2 * fw]
        a_re = zr[:l1] + zi[l1:]
        a_im = zi[:l1] - zr[l1:]
        tc = jnp.concatenate([tw_ref[0, t]] * (fw // LANES), axis=1)
        ts = jnp.concatenate([tw_ref[1, t]] * (fw // LANES), axis=1)
        o_ref[:, t * 2 * fw:t * 2 * fw + fw] = a_re * tc + a_im * ts
        o_ref[:, t * 2 * fw + fw:(t + 1) * 2 * fw] = a_im * tc - a_re * ts


def _ft_s1(w2d, cs, tw, l1, l2):
    g = min(8, l2)
    blk = g * 2 * FT_WIDTH
    return pl.pallas_call(
        functools.partial(_ft_s1_kernel, l1=l1, g=g),
        grid=(l2 // g,),
        in_specs=[pl.BlockSpec((l1, blk), lambda i: (0, i)),
                  pl.BlockSpec(cs.shape, lambda i: (0, 0)),
                  pl.BlockSpec((2, g, l1, LANES), lambda i: (0, i, 0, 0))],
        out_specs=pl.BlockSpec((l1, blk), lambda i: (0, i)),
        out_shape=jax.ShapeDtypeStruct((l1, l2 * 2 * FT_WIDTH), F32),
        compiler_params=_cparams(("parallel",)),
        name="ft_s1",
    )(w2d, cs, tw)


def _ft_s2_kernel(b_ref, cs_ref, o_ref, *, g, scale):
    fw = FT_WIDTH
    cmat = cs_ref[0]
    smat = cs_ref[1]
    for kk in range(g):
        blk = b_ref[kk]
        out = _mm(cmat, blk[:, :fw], 2, 2) + _mm(smat, blk[:, fw:], 2, 2)
        o_ref[:, kk * fw:(kk + 1) * fw] = out * scale


def _ft_s2(b3d, cs2, l1, l2, scale):
    g = min(8, l1)
    return pl.pallas_call(
        functools.partial(_ft_s2_kernel, g=g, scale=scale),
        grid=(l1 // g,),
        in_specs=[pl.BlockSpec((g, l2, 2 * FT_WIDTH), lambda i: (i, 0, 0)),
                  pl.BlockSpec(cs2.shape, lambda i: (0, 0, 0))],
        out_specs=pl.BlockSpec((l2, g * FT_WIDTH), lambda i: (0, i)),
        out_shape=jax.ShapeDtypeStruct((l2, l1 * FT_WIDTH), F32),
        compiler_params=_cparams(("parallel",)),
        name="ft_s2",
    )(b3d, cs2)


def _ft_tables(n):
    l1 = 1 << (int(math.log2(n)) // 2)
    l2 = n // l1
    assert l1 * l2 == n and l1 % 8 == 0 and l2 % 8 == 0
    k1 = np.arange(l1)
    ang1 = 2.0 * np.pi * np.outer(k1, np.arange(l1)) / l1
    cs1 = np.concatenate([np.cos(ang1), np.sin(ang1)], axis=0)
    angt = 2.0 * np.pi * np.outer(np.arange(l2), k1) / n
    tw = np.stack([np.cos(angt), np.sin(angt)], axis=0)
    tw = np.broadcast_to(tw[..., None], (2, l2, l1, LANES))
    ang2 = 2.0 * np.pi * np.outer(np.arange(l2), np.arange(l2)) / l2
    cs2 = np.stack([np.cos(ang2), np.sin(ang2)], axis=0)
    f = lambda a: jnp.asarray(np.ascontiguousarray(a), F32)
    return l1, l2, f(cs1), f(tw), f(cs2)


def _chan_dft_matrix():
    n = FT_GROUP_DIM
    ang = 2.0 * np.pi * np.outer(np.arange(n), np.arange(n)) / n
    eye = np.eye(FT_GROUPS)
    mat = np.concatenate([np.kron(eye, np.cos(ang)), -np.kron(eye, np.sin(ang))], axis=1)
    return jnp.asarray(mat, F32)


def _fourier_mix(u, col_block, chan_mat):
    n = u.shape[0]
    l1, l2, cs1, tw, cs2 = _ft_tables(n)
    wc = _ft_chan(u, col_block, chan_mat)
    b = _ft_s1(wc.reshape(l1, l2 * 2 * FT_WIDTH), cs1, tw, l1, l2)
    scale = 1.0 / math.sqrt(n * FT_GROUP_DIM)
    out = _ft_s2(b.reshape(l1, l2, 2 * FT_WIDTH), cs2, l1, l2, scale)
    return out.reshape(n, FT_WIDTH)


def _merge_kernel(ya_ref, yb_ref, yc_ref, ga_ref, gb_ref, gc_ref, x_ref, gt_ref,
                  wa_ref, wb_ref, wc_ref, wo_ref, o_ref):
    def branch(y_ref, w_ref, g_ref):
        t = jnp.dot(y_ref[...].astype(BF16), w_ref[...], preferred_element_type=F32)
        return _sigmoid(g_ref[...]) * t

    m = branch(ya_ref, wa_ref, ga_ref) + branch(yb_ref, wb_ref, gb_ref) + branch(yc_ref, wc_ref, gc_ref)
    out = jnp.dot(m.astype(BF16), wo_ref[...], preferred_element_type=F32)
    o_ref[...] = x_ref[...] + gt_ref[...] * out


def _merge(ya, yb, yc, u, x, gt, wa, wb, wc, wo):
    n_rows, d = x.shape
    tm = _row_tile(n_rows, 512)
    full = lambda a: pl.BlockSpec(a.shape, lambda i: (0,) * a.ndim)
    rows = lambda a: pl.BlockSpec((tm, a.shape[1]), lambda i: (i, 0))
    gate = lambda b: pl.BlockSpec((tm, d), lambda i: (i, b))
    return pl.pallas_call(
        _merge_kernel,
        grid=(n_rows // tm,),
        in_specs=[rows(ya), rows(yb), rows(yc), gate(0), gate(1), gate(2), rows(x), full(gt),
                  full(wa), full(wb), full(wc), full(wo)],
        out_specs=pl.BlockSpec((tm, d), lambda i: (i, 0)),
        out_shape=jax.ShapeDtypeStruct((n_rows, d), F32),
        compiler_params=_cparams(("parallel",)),
        name="merge",
    )(ya, yb, yc, u, u, u, x, gt, wa, wb, wc, wo)


def _pad_heads(w, src_cols, width):
    k = w.shape[0]
    wh = w.reshape(k, MLA_HEADS, width)[:, :, src_cols]
    wh = jnp.pad(wh, ((0, 0), (0, 0), (0, HEAD_PAD - wh.shape[2])))
    return wh.reshape(k, MLA_HEADS * HEAD_PAD)


def _block_diag2(a, b):
    z01 = jnp.zeros((a.shape[0], b.shape[1]), a.dtype)
    z10 = jnp.zeros((b.shape[0], a.shape[1]), a.dtype)
    return jnp.concatenate([jnp.concatenate([a, z01], axis=1), jnp.concatenate([z10, b], axis=1)], axis=0)


def _rope_tables(n):
    half = ROPE_DIM // 2
    inv = ROPE_BASE ** (-np.arange(0, half, 2, dtype=np.float32) / half)
    t = np.arange(n)
    row = (t // GRID_W).astype(np.float32)
    col = (t % GRID_W).astype(np.float32)
    q = half // 2
    cos = np.ones((n, HEAD_PAD), np.float32)
    sin_a = np.zeros((n, HEAD_PAD), np.float32)
    sin_b = np.zeros((n, HEAD_PAD), np.float32)
    for base, pos in ((NOPE_DIM, row), (NOPE_DIM + half, col)):
        ang = pos[:, None] * inv[None, :]
        c, s = np.cos(ang), np.sin(ang)
        cos[:, base:base + q] = c
        cos[:, base + q:base + 2 * q] = c
        sin_a[:, base:base + q] = -s
        sin_b[:, base + q:base + 2 * q] = s
    return jnp.asarray(np.stack([cos, sin_a, sin_b], axis=0))


def _layer_weights(l, w):
    w_in = w["w_in"][l]
    o1 = RW_COLS
    o2 = o1 + MLA_COLS
    o3 = o2 + FT_WIDTH
    w_in_p = jnp.concatenate(
        [w_in[:, o3:], w_in[:, :o1], w_in[:, o1:o2],
         jnp.zeros((D_MODEL, MLA_PAD - MLA_COLS), F32), w_in[:, o2:o3]], axis=1).astype(BF16)
    ident = jnp.tile(jnp.array([[0.0], [1.0], [0.0]], F32), (1, IN_PAD))
    cw_in = ident.at[:, RW_OFF:RW_OFF + RW_COLS].set(w["rw_conv"][l])
    pad_gain = lambda g: jnp.pad(g, (0, HEAD_PAD - QK_DIM))
    wb = w["w_branch_b"][l].reshape(MLA_HEADS, V_DIM, D_MODEL)
    wb = jnp.pad(wb, ((0, 0), (0, HEAD_PAD - V_DIM), (0, 0))).reshape(MLA_HEADS * HEAD_PAD, D_MODEL)
    q_scale = math.log2(math.e) * QK_DIM ** -0.5
    score_bound = (BOUND_MARGIN * QK_DIM * q_scale * jnp.max(jnp.abs(w["q_gain"][l]))
                   * jnp.max(jnp.abs(w["k_gain"][l])))
    shift_lane = (jnp.arange(HEAD_PAD) == QK_DIM).astype(F32)
    return dict(
        w_in=w_in_p, cw_in=cw_in,
        wlora=_block_diag2(w["rw_w_up"][l, 0], w["rw_w_up"][l, 1]),
        alora=_block_diag2(w["rw_a_up"][l, 0], w["rw_a_up"][l, 1]),
        gup=w["rw_g_up"][l],
        rw_vec=jnp.stack([w["rw_k_k"][l], w["rw_k_a"][l], w["rw_r_k"][l].reshape(-1),
                          w["rw_w0"][l, 0], w["rw_w0"][l, 1], w["rw_a0"][l, 0], w["rw_a0"][l, 1],
                          jnp.zeros((RW_WIDTH,), F32)], axis=0),
        ln_vec=jnp.stack([w["rw_ln_w"][l], w["rw_ln_b"][l]], axis=0),
        wq=_pad_heads(w["w_uq"][l], slice(0, QK_DIM), QK_DIM).astype(BF16),
        wk=_pad_heads(w["w_ukv"][l], slice(0, NOPE_DIM), NOPE_DIM + V_DIM).astype(BF16),
        wv=_pad_heads(w["w_ukv"][l], slice(NOPE_DIM, NOPE_DIM + V_DIM), NOPE_DIM + V_DIM).astype(BF16),
        vq=w["mla_q_norm"][l][None], vkv=w["mla_kv_norm"][l][None],
        gains_lat=jnp.stack([pad_gain(w["q_gain"][l]) * q_scale, pad_gain(w["k_gain"][l]),
                             -score_bound * shift_lane, shift_lane], axis=0),
        score_bound=score_bound,
        wa=w["w_branch_a"][l].astype(BF16), wb=wb.astype(BF16), wc=w["w_branch_c"][l].astype(BF16),
        wo=w["w_out"][l].astype(BF16),
        w_up=w["w_up"][l].astype(BF16), cw_ffn=w["ffn_conv"][l], w_down=w["w_down"][l].astype(BF16),
    )


def _forward(x, c, ctx, c_ctx, w):
    depth = w["w_ada"].shape[0]
    x_lat = x[0]
    x_ctx = ctx[0]
    n_lat = x_lat.shape[0]
    n_ctx = x_ctx.shape[0]
    assert n_lat % CHUNK == 0 and n_ctx % CHUNK == 0 and n_lat % GRID_W == 0

    cc = jnp.zeros((8, D_MODEL), F32).at[0].set(c[0]).at[1].set(c_ctx)
    mods = _mods(cc, w["w_ada"], w["b_ada"])

    head = np.arange(RW_WIDTH) // RW_HEAD_DIM
    ones_bd = jnp.asarray((head[:, None] == head[None, :]).astype(np.float32))
    avg_bd = ones_bd * (1.0 / RW_HEAD_DIM)
    ti = np.arange(CHUNK)
    tri = jnp.asarray(np.stack([ti[None, :] <= ti[:, None], ti[None, :] >= ti[:, None]]).astype(np.float32),
                      BF16)
    chan_mat = _chan_dft_matrix()
    rope_tab = _rope_tables(n_lat)

    for l in range(depth):
        last = l == depth - 1
        p = _layer_weights(l, w)
        streams = [(x_lat, mods[l, 0], rope_tab)]
        if True:
            streams.append((x_ctx, mods[l, 1], None))
        us, preps, qkv = [], [], []
        for xs, mod, rt in streams:
            sh1, sc1 = mod[0:D_MODEL][None], mod[D_MODEL:2 * D_MODEL][None]
            gain1 = w["g_norm1"][l][None] * (1.0 + sc1)
            u = _in_proj(xs, gain1, sh1, p["w_in"], p["cw_in"])
            us.append(u)
            preps.append(_rw_prep(u, RW_OFF // RW_COLS, p["wlora"], p["alora"], p["gup"], ones_bd, p["rw_vec"]))
            qkv.append(_mla_prep(u, MLA_OFF // MLA_PAD, p["wq"], p["wk"], p["wv"], p["vq"], p["vkv"],
                                 p["gains_lat"], rt))
        com = jnp.concatenate([preps[0][0], preps[1][0]], axis=1)
        dirs = jnp.concatenate([preps[0][1], preps[1][1]], axis=2)
        yf, yb = _rw_scan(com, dirs, tri, n_ctx // CHUNK)
        k_all = jnp.concatenate([qkv[0][1], qkv[1][1]], axis=1)
        vt_all = jnp.concatenate([qkv[0][2], qkv[1][2]], axis=2)

        new = []
        for si, (xs, mod, rt) in enumerate(streams):
            if si == 1 and last:
                continue
            row_off = 0 if si == 0 else n_lat
            ya = _rw_post(yf, yb, preps[si][2], avg_bd, p["ln_vec"], row_off)
            if si == 0:
                yb_att = _attention(qkv[0][0], k_all, vt_all, p["score_bound"])
            else:
                yb_att = _attention(qkv[1][0], qkv[1][1], qkv[1][2], p["score_bound"])
            yc = _fourier_mix(us[si], FT_OFF // FT_WIDTH, chan_mat)
            gt1 = mod[2 * D_MODEL:3 * D_MODEL][None]
            x1 = _merge(ya, yb_att, yc, us[si], xs, gt1, p["wa"], p["wb"], p["wc"], p["wo"])
            sh2, sc2, gt2 = (mod[3 * D_MODEL:4 * D_MODEL][None], mod[4 * D_MODEL:5 * D_MODEL][None],
                             mod[5 * D_MODEL:6 * D_MODEL][None])
            gain2 = w["g_norm2"][l][None] * (1.0 + sc2)
            new.append(_ffn(x1, gain2, sh2, gt2, p["w_up"], p["cw_ffn"], p["w_down"]))
        x_lat = new[0]
        if not last:
            x_ctx = new[1]
    return x_lat[None]


def kernel(x, c, ctx, c_ctx, w_ada, b_ada, g_norm1, g_norm2, w_in, rw_conv, rw_w0, rw_w_up, rw_a0, rw_a_up,
           rw_g_up, rw_k_k, rw_k_a, rw_r_k, rw_ln_w, rw_ln_b, mla_q_norm, w_uq, mla_kv_norm, w_ukv, q_gain,
           k_gain, w_branch_a, w_branch_b, w_branch_c, w_out, ffn_conv, w_up, w_down):
    w = dict(w_ada=w_ada, b_ada=b_ada, g_norm1=g_norm1, g_norm2=g_norm2, w_in=w_in, rw_conv=rw_conv,
             rw_w0=rw_w0, rw_w_up=rw_w_up, rw_a0=rw_a0, rw_a_up=rw_a_up, rw_g_up=rw_g_up, rw_k_k=rw_k_k,
             rw_k_a=rw_k_a, rw_r_k=rw_r_k, rw_ln_w=rw_ln_w, rw_ln_b=rw_ln_b, mla_q_norm=mla_q_norm,
             w_uq=w_uq, mla_kv_norm=mla_kv_norm, w_ukv=w_ukv, q_gain=q_gain, k_gain=k_gain,
             w_branch_a=w_branch_a, w_branch_b=w_branch_b, w_branch_c=w_branch_c, w_out=w_out,
             ffn_conv=ffn_conv, w_up=w_up, w_down=w_down)
    return _forward(x, c, ctx, c_ctx, w)
```

```python
import functools
import math

import numpy as np
import jax
import jax.numpy as jnp
from jax import lax
from jax.experimental import pallas as pl
from jax.experimental.pallas import tpu as pltpu

F32 = jnp.float32
BF16 = jnp.bfloat16

D_MODEL = 1024
GRID_W = 64
NORM_EPS = 1e-6
RW_HEADS = 6
RW_HEAD_DIM = 64
RW_WIDTH = RW_HEADS * RW_HEAD_DIM
DECAY_LORA = 64
ICLR_LORA = 64
GATE_LORA = 128
RW_LN_EPS = 64e-5
MLA_HEADS = 6
Q_LORA = 384
KV_LORA = 256
NOPE_DIM = 64
ROPE_DIM = 32
V_DIM = 64
QK_DIM = NOPE_DIM + ROPE_DIM
ROPE_BASE = 10000.0
FT_GROUPS = 4
FT_GROUP_DIM = 64
FT_WIDTH = FT_GROUPS * FT_GROUP_DIM
D_FF = 2816
N_MOD = 6
RW_COLS = 3 * RW_WIDTH + 2 * DECAY_LORA + 2 * ICLR_LORA + GATE_LORA
MLA_COLS = Q_LORA + KV_LORA + ROPE_DIM

LANES = 128
HEAD_PAD = 128
MLA_PAD = 768
GATE_OFF = 0
RW_OFF = 3 * D_MODEL
MLA_OFF = RW_OFF + RW_COLS
FT_OFF = MLA_OFF + MLA_PAD
IN_PAD = FT_OFF + FT_WIDTH
HALO = 16
STREAM_TILE = 256
CHUNK = 128
VMEM_LIMIT = 48 * 1024 * 1024
V_ROWS = 80
BOUND_MARGIN = 1.02
MAX_SCORE_SPAN = 100.0
ATT_TQ, ATT_SUB, ATT_TK = 2048, 256, 3328

NN = (((1,), (0,)), ((), ()))
NT = (((1,), (1,)), ((), ()))


def _pieces(x, n):
    if x.dtype == BF16:
        return [x]
    out = []
    rem = x
    for i in range(n):
        p = rem.astype(BF16)
        out.append(p)
        if i + 1 < n:
            rem = rem - p.astype(F32)
    return out


def _mm(a, b, pa=1, pb=1, dims=NN):
    ap = _pieces(a, pa)
    bp = _pieces(b, pb)
    n = max(len(ap), len(bp))
    acc = None
    for i in reversed(range(len(ap))):
        for j in reversed(range(len(bp))):
            if i + j < n:
                t = lax.dot_general(ap[i], bp[j], dims, preferred_element_type=F32)
                acc = t if acc is None else acc + t
    return acc


def _sigmoid(x):
    return 1.0 / (1.0 + jnp.exp(-x))


def _cparams(sem):
    return pltpu.CompilerParams(dimension_semantics=sem, vmem_limit_bytes=VMEM_LIMIT)


def _row_tile(n, pref):
    t = min(n, pref)
    while n % t:
        t //= 2
    return t


def _mods_kernel(cc_ref, w_ref, b_ref, o_ref):
    cc = cc_ref[...]
    s = cc * _sigmoid(cc)
    o_ref[0] = _mm(s, w_ref[0], 2, 2) + b_ref[0]


def _mods(cc, w_ada, b_ada):
    depth, d, n = w_ada.shape
    tn = 1536
    return pl.pallas_call(
        _mods_kernel,
        grid=(depth, n // tn),
        in_specs=[pl.BlockSpec((8, d), lambda l, j: (0, 0)),
                  pl.BlockSpec((1, d, tn), lambda l, j: (l, 0, j)),
                  pl.BlockSpec((1, 1, tn), lambda l, j: (l, 0, j))],
        out_specs=pl.BlockSpec((1, 8, tn), lambda l, j: (l, 0, j)),
        out_shape=jax.ShapeDtypeStruct((depth, 8, n), F32),
        compiler_params=_cparams(("parallel", "parallel")),
        name="mods",
    )(cc, w_ada, b_ada.reshape(depth, 1, n))


def _modnorm(xv, g, s):
    ms = jnp.mean(xv * xv, axis=-1, keepdims=True)
    return (xv * lax.rsqrt(ms + NORM_EPS)) * g + s


def _fill_normed(h_scr, x_ref, xp_ref, xn_ref, g, s, i, nrt, tm):
    h_scr[HALO:HALO + tm] = _modnorm(x_ref[...], g, s).astype(BF16)
    hp = jnp.where(i > 0, _modnorm(xp_ref[...], g, s), 0.0)
    h_scr[0:HALO] = hp.astype(BF16)
    hn = jnp.where(i < nrt - 1, _modnorm(xn_ref[...], g, s), 0.0)
    h_scr[HALO + tm:2 * HALO + tm] = hn.astype(BF16)


def _conv3(u, cw, tm):
    rows = u.shape[0]
    up = pltpu.roll(u, 1, 0)
    un = pltpu.roll(u, rows - 1, 0)
    out = cw[0:1] * up + cw[1:2] * u + cw[2:3] * un
    return out[HALO:HALO + tm]


def _halo_specs(tm, d, n_rows):
    r = tm // HALO
    last = n_rows // HALO - 1
    return [pl.BlockSpec((tm, d), lambda i: (i, 0)),
            pl.BlockSpec((HALO, d), lambda i: (jnp.maximum(i * r - 1, 0), 0)),
            pl.BlockSpec((HALO, d), lambda i: (jnp.minimum((i + 1) * r, last), 0))]


def _resident(a):
    return pl.BlockSpec(a.shape, lambda i: (0,) * a.ndim, pipeline_mode=pl.Buffered(1))


def _in_kernel(x_ref, xp_ref, xn_ref, g_ref, s_ref, w_ref, cw_ref, o_ref, h_scr, *, tm, nrt, tn):
    _fill_normed(h_scr, x_ref, xp_ref, xn_ref, g_ref[...], s_ref[...], pl.program_id(0), nrt, tm)
    h = h_scr[...]
    nj = w_ref.shape[1] // tn
    u = {}
    for t in range(nj + 1):
        if t < nj:
            u[t] = jnp.dot(h, w_ref[:, t * tn:(t + 1) * tn], preferred_element_type=F32)
        if t >= 1:
            cols = slice((t - 1) * tn, t * tn)
            o_ref[:, cols] = _conv3(u.pop(t - 1), cw_ref[:, cols], tm)


def _in_proj(x, gain, shift, w, cw):
    n_rows, d = x.shape
    n = w.shape[1]
    tm = _row_tile(n_rows, 512)
    nrt = n_rows // tm
    return pl.pallas_call(
        functools.partial(_in_kernel, tm=tm, nrt=nrt, tn=512),
        grid=(nrt,),
        in_specs=_halo_specs(tm, d, n_rows) + [_resident(gain), _resident(shift), _resident(w), _resident(cw)],
        out_specs=pl.BlockSpec((tm, n), lambda i: (i, 0)),
        out_shape=jax.ShapeDtypeStruct((n_rows, n), F32),
        scratch_shapes=[pltpu.VMEM((tm + 2 * HALO, d), BF16)],
        compiler_params=_cparams(("parallel",)),
        name="in_proj",
    )(x, x, x, gain, shift, w, cw)


def _ffn_kernel(x_ref, xp_ref, xn_ref, g_ref, s_ref, gt_ref, wu_ref, cw_ref, wd_ref, o_ref, h_scr, *,
                tm, nrt, tf):
    _fill_normed(h_scr, x_ref, xp_ref, xn_ref, g_ref[...], s_ref[...], pl.program_id(0), nrt, tm)
    h = h_scr[...]
    nj = D_FF // tf
    u, gated = {}, {}
    for t in range(nj + 2):
        if t < nj:
            ca = slice(t * tf, (t + 1) * tf)
            cb = slice(D_FF + t * tf, D_FF + (t + 1) * tf)
            u[t] = (jnp.dot(h, wu_ref[:, ca], preferred_element_type=F32),
                    jnp.dot(h, wu_ref[:, cb], preferred_element_type=F32))
        if 0 <= t - 1 < nj:
            j = t - 1
            ua, ub = u.pop(j)
            a = _conv3(ua, cw_ref[:, j * tf:(j + 1) * tf], tm)
            b = _conv3(ub, cw_ref[:, D_FF + j * tf:D_FF + (j + 1) * tf], tm)
            gated[j] = (a * _sigmoid(a) * b).astype(BF16)
        if 0 <= t - 2 < nj:
            j = t - 2
            down = jnp.dot(gated.pop(j), wd_ref[j * tf:(j + 1) * tf, :], preferred_element_type=F32)
            if j == 0:
                o_ref[...] = down
            else:
                o_ref[...] += down
    o_ref[...] = x_ref[...] + gt_ref[...] * o_ref[...]


def _ffn(x, gain, shift, gate, w_up, cw, w_down):
    n_rows, d = x.shape
    tm = _row_tile(n_rows, 512)
    nrt = n_rows // tm
    return pl.pallas_call(
        functools.partial(_ffn_kernel, tm=tm, nrt=nrt, tf=256),
        grid=(nrt,),
        in_specs=_halo_specs(tm, d, n_rows) + [_resident(gain), _resident(shift), _resident(gate),
                                               _resident(w_up), _resident(cw), _resident(w_down)],
        out_specs=pl.BlockSpec((tm, d), lambda i: (i, 0)),
        out_shape=jax.ShapeDtypeStruct((n_rows, d), F32),
        scratch_shapes=[pltpu.VMEM((tm + 2 * HALO, d), BF16)],
        compiler_params=_cparams(("parallel",)),
        name="ffn",
    )(x, x, x, gain, shift, gate, w_up, cw, w_down)


def _seg_sum(x, ones_bd):
    return _mm(x, ones_bd, 3, 1)


def _two_stream_specs(tm, width, col_block, n_lat_tiles):
    return [pl.BlockSpec((tm, width), lambda i: (jnp.minimum(i, n_lat_tiles - 1), col_block)),
            pl.BlockSpec((tm, width), lambda i: (jnp.maximum(i - n_lat_tiles, 0), col_block))]


def _rw_prep_kernel(ul_ref, uc_ref, wlora_ref, alora_ref, gup_ref, ones_ref, vec_ref,
                    com_ref, dir_ref, post_ref, *, n_lat_tiles):
    u = jnp.where(pl.program_id(0) < n_lat_tiles, ul_ref[...], uc_ref[...])
    w = RW_WIDTH
    r = u[:, 0:w]
    k = u[:, w:2 * w]
    v = u[:, 2 * w:3 * w]
    wd = u[:, 3 * w:3 * w + 2 * DECAY_LORA]
    ad = u[:, 3 * w + 2 * DECAY_LORA:3 * w + 2 * DECAY_LORA + 2 * ICLR_LORA]
    gd = u[:, 3 * w + 2 * DECAY_LORA + 2 * ICLR_LORA:]
    vec = vec_ref[...]
    k_k, k_a, r_k = vec[0:1], vec[1:2], vec[2:3]
    ones_bd = ones_ref[...]

    kk = k * k_k
    kk = kk * lax.rsqrt(_seg_sum(kk * kk, ones_bd) + 1e-12)
    g = _mm(_sigmoid(gd), gup_ref[...], 2, 2)
    wl = _mm(jnp.tanh(wd), wlora_ref[...], 2, 2)
    al = _mm(ad, alora_ref[...], 2, 2)
    com_ref[0] = r
    com_ref[1] = v
    com_ref[2] = -kk
    kd_sum = None
    for d in range(2):
        w0 = vec[3 + d:4 + d]
        a0 = vec[5 + d:6 + d]
        lw = -math.exp(-0.5) * _sigmoid(w0 + wl[:, d * w:(d + 1) * w])
        a = _sigmoid(a0 + al[:, d * w:(d + 1) * w])
        kd = k * (1.0 + (a - 1.0) * k_a)
        dir_ref[d, 0] = lw
        dir_ref[d, 1] = kd
        dir_ref[d, 2] = kk * a
        kd_sum = kd if kd_sum is None else kd_sum + kd
    post_ref[0] = g
    post_ref[1] = _seg_sum(r * kd_sum * r_k, ones_bd) * v


def _rw_prep(u_lat, u_ctx, col_block, wlora, alora, gup, ones_bd, vec):
    tm = math.gcd(STREAM_TILE, u_ctx.shape[0])
    n_lat_tiles = u_lat.shape[0] // tm
    n_rows = u_lat.shape[0] + u_ctx.shape[0]
    w = RW_WIDTH
    full = lambda a: pl.BlockSpec(a.shape, lambda i: (0,) * a.ndim)
    return pl.pallas_call(
        functools.partial(_rw_prep_kernel, n_lat_tiles=n_lat_tiles),
        grid=(n_rows // tm,),
        in_specs=_two_stream_specs(tm, RW_COLS, col_block, n_lat_tiles) + [
            full(wlora), full(alora), full(gup), full(ones_bd), full(vec)],
        out_specs=[pl.BlockSpec((3, tm, w), lambda i: (0, i, 0)),
                   pl.BlockSpec((2, 3, tm, w), lambda i: (0, 0, i, 0)),
                   pl.BlockSpec((2, tm, w), lambda i: (0, i, 0))],
        out_shape=[jax.ShapeDtypeStruct((3, n_rows, w), F32),
                   jax.ShapeDtypeStruct((2, 3, n_rows, w), F32),
                   jax.ShapeDtypeStruct((2, n_rows, w), F32)],
        compiler_params=_cparams(("parallel",)),
        name="rw_prep",
    )(u_lat, u_ctx, wlora, alora, gup, ones_bd, vec)


def _smm(a, b, dims=NN):
    return _mm(a, b, 1, 1, dims)


def _scan_dir(com, dr, tri, slot, reverse):
    c = CHUNK
    r, v, a = com[0], com[1], com[2]
    lw, kd, b = dr[0], dr[1], dr[2]
    cs = _mm(tri, lw, 1, 3)
    tot = cs[0:1] if reverse else cs[c - 1:c]
    e_pos = jnp.exp(cs)
    e_neg = jnp.exp(-cs)
    e_end = jnp.exp(tot - cs)
    at = a * jnp.exp(cs - lw)
    rt = r * e_pos
    bt = b * e_neg
    kt = kd * e_neg
    bh = b * e_end
    kh = kd * e_end
    p_end = jnp.exp(tot)

    row = lax.broadcasted_iota(jnp.int32, (c, c), 0)
    col = lax.broadcasted_iota(jnp.int32, (c, c), 1)
    before = (col > row) if reverse else (col < row)
    upto = (col >= row) if reverse else (col <= row)
    eye = row == col
    same_head = (row // RW_HEAD_DIM) == (col // RW_HEAD_DIM)
    both = lambda m: jnp.concatenate([m, m], axis=1)
    lane = lax.broadcasted_iota(jnp.int32, (c, LANES), 1)
    heads = [(lane // RW_HEAD_DIM) == h for h in range(2)]

    def by_head(x):
        return jnp.concatenate([jnp.where(m, x, 0.0) for m in heads], axis=0).astype(BF16)

    pairs = []
    for p in range(RW_HEADS // 2):
        sl = slice(p * LANES, (p + 1) * LANES)
        pairs.append(dict(
            slot=slot, p=p, by_head=by_head,
            atrt=jnp.concatenate([at[:, sl], rt[:, sl]], axis=0).astype(BF16),
            bk=jnp.concatenate([by_head(bt[:, sl]), by_head(kt[:, sl])], axis=0),
            v2=by_head(v[:, sl]), v_t=v[:, sl].T,
            bkh=jnp.concatenate([bh[:, sl], kh[:, sl]], axis=0).astype(BF16),
            p_end=p_end[:, sl], masks=(both(before), both(upto), both(eye), same_head)))
    return pairs


def _scan_chunk(pairs, s_ref):
    c = CHUNK
    zero = jnp.zeros((c, c), BF16)
    for pr in pairs:
        before, upto, eye, _ = pr["masks"]
        gm = _smm(pr["atrt"], pr["bk"], NT)
        a_ab = jnp.where(before, gm[:c, :2 * c], 0.0)
        pr["a_ak"] = jnp.where(before, gm[:c, 2 * c:], 0.0).astype(BF16)
        pr["a_rb"] = jnp.where(upto, gm[c:, :2 * c], 0.0).astype(BF16)
        pr["a_rk"] = jnp.where(upto, gm[c:, 2 * c:], 0.0).astype(BF16)
        pr["t_inv"] = jnp.where(eye, 1.0, a_ab)
        pr["apow"] = a_ab.astype(BF16)

    def block_diag(m):
        return jnp.concatenate([jnp.concatenate([m[:, :c], zero], axis=1),
                                jnp.concatenate([zero, m[:, c:]], axis=1)], axis=0)

    n_sq = int(math.log2(c)) - 1
    for pr in pairs:
        pr["apow"] = _smm(pr["apow"], block_diag(pr["apow"])).astype(BF16)
    for k in range(n_sq):
        for pr in pairs:
            w_bd = block_diag(pr["apow"])
            if k + 1 < n_sq:
                res = _smm(jnp.concatenate([pr["apow"], pr["t_inv"].astype(BF16)], axis=0), w_bd)
                pr["apow"] = res[:c].astype(BF16)
                pr["t_inv"] = pr["t_inv"] + res[c:]
            else:
                pr["t_inv"] = pr["t_inv"] + _smm(pr["t_inv"], w_bd)
    for pr in pairs:
        s0 = s_ref[pr["slot"], pr["p"]]
        pr["s0"] = s0
        hr = _smm(pr["atrt"], s0, NT)
        pr["ah"], pr["rh"] = hr[:c], hr[c:]
    for pr in pairs:
        x = _smm(pr["a_ak"], pr["v2"])
        pr["u"] = _smm(pr["t_inv"], pr["by_head"](pr["ah"] + x))
    outs = {}
    for pr in pairs:
        pr["y"] = pr["rh"] + _smm(pr["a_rb"], pr["by_head"](pr["u"])) + _smm(pr["a_rk"], pr["v2"])
    for pr in pairs:
        same_head = pr["masks"][3]
        uv_t = jnp.concatenate([pr["u"].T, pr["v_t"]], axis=1)
        upd = _smm(uv_t, pr["bkh"])
        s_ref[pr["slot"], pr["p"]] = pr["s0"] * pr["p_end"] + jnp.where(same_head, upd, 0.0)
        outs[(pr["slot"], pr["p"])] = pr["y"]
    n_pairs = RW_HEADS // 2
    return [jnp.concatenate([outs[(d, p)] for p in range(n_pairs)], axis=1) for d in range(2)]


def _rw_scan_kernel(comf_ref, dirf_ref, comb_ref, dirb_ref, tri_ref, yf_ref, yb_ref, s_ref):
    @pl.when(pl.program_id(0) == 0)
    def _():
        s_ref[...] = jnp.zeros_like(s_ref)

    pairs = (_scan_dir(comf_ref[...], dirf_ref[0], tri_ref[0], 0, False)
             + _scan_dir(comb_ref[...], dirb_ref[0], tri_ref[1], 1, True))
    yf, yb = _scan_chunk(pairs, s_ref)
    yf_ref[...] = yf
    yb_ref[...] = yb


def _rw_scan(com, dirs, tri, n_ctx_chunks):
    n_rows = com.shape[1]
    w = RW_WIDTH
    c = CHUNK
    assert c == LANES
    n_chunks = n_rows // c
    nc = n_ctx_chunks
    nl = n_chunks - nc

    def fwd_block(i):
        return jnp.where(i < nc, nl + i, i - nc)

    def bwd_block(i):
        return jnp.where(i < nc, nl + nc - 1 - i, nl - 1 - (i - nc))

    return pl.pallas_call(
        _rw_scan_kernel,
        grid=(n_chunks,),
        in_specs=[pl.BlockSpec((3, c, w), lambda i: (0, fwd_block(i), 0)),
                  pl.BlockSpec((1, 3, c, w), lambda i: (0, 0, fwd_block(i), 0)),
                  pl.BlockSpec((3, c, w), lambda i: (0, bwd_block(i), 0)),
                  pl.BlockSpec((1, 3, c, w), lambda i: (1, 0, bwd_block(i), 0)),
                  pl.BlockSpec((2, c, c), lambda i: (0, 0, 0))],
        out_specs=[pl.BlockSpec((c, w), lambda i: (fwd_block(i), 0)),
                   pl.BlockSpec((c, w), lambda i: (bwd_block(i), 0))],
        out_shape=[jax.ShapeDtypeStruct((n_rows, w), F32)] * 2,
        scratch_shapes=[pltpu.VMEM((2, RW_HEADS // 2, LANES, LANES), F32)],
        compiler_params=_cparams(("arbitrary",)),
        name="rw_scan",
    )(com, dirs, com, dirs, tri)


def _rw_post_kernel(yf_ref, yb_ref, post_ref, avg_ref, vec_ref, o_ref):
    y = yf_ref[...] + yb_ref[...]
    avg = avg_ref[...]
    mu = _mm(y, avg, 3, 1)
    yc = y - mu
    var = _mm(yc * yc, avg, 3, 1)
    yn = yc * lax.rsqrt(var + RW_LN_EPS)
    vec = vec_ref[...]
    o_ref[...] = (yn * vec[0:1] + vec[1:2] + post_ref[1]) * post_ref[0]


def _rw_post(yf, yb, post, avg_bd, vec, n_rows):
    w = RW_WIDTH
    tm = math.gcd(STREAM_TILE, n_rows)
    full = lambda a: pl.BlockSpec(a.shape, lambda i: (0,) * a.ndim)
    return pl.pallas_call(
        _rw_post_kernel,
        grid=(n_rows // tm,),
        in_specs=[pl.BlockSpec((tm, w), lambda i: (i, 0)),
                  pl.BlockSpec((tm, w), lambda i: (i, 0)),
                  pl.BlockSpec((2, tm, w), lambda i: (0, i, 0)),
                  full(avg_bd), full(vec)],
        out_specs=pl.BlockSpec((tm, w), lambda i: (i, 0)),
        out_shape=jax.ShapeDtypeStruct((n_rows, w), F32),
        compiler_params=_cparams(("parallel",)),
        name="rw_post",
    )(yf, yb, post, avg_bd, vec)


def _rms(x, n):
    return x * lax.rsqrt(jnp.sum(x * x, axis=-1, keepdims=True) * (1.0 / n) + NORM_EPS)


def _mla_prep_kernel(ul_ref, uc_ref, wq_ref, wk_ref, wv_ref, vq_ref, vkv_ref, gains_ref, rope_ref,
                     q_ref, k_ref, vt_ref, *, n_lat_tiles):
    u = jnp.where(pl.program_id(0) < n_lat_tiles, ul_ref[...], uc_ref[...])
    cq = _rms(u[:, :Q_LORA], Q_LORA) * vq_ref[...]
    q = jnp.dot(cq.astype(BF16), wq_ref[...], preferred_element_type=F32)
    ckv = _rms(u[:, Q_LORA:Q_LORA + KV_LORA], KV_LORA) * vkv_ref[...]
    ckv = ckv.astype(BF16)
    kn = jnp.dot(ckv, wk_ref[...], preferred_element_type=F32)
    vv = jnp.dot(ckv, wv_ref[...], preferred_element_type=F32)
    kr = pltpu.roll(u[:, Q_LORA + KV_LORA:], NOPE_DIM, 1)
    gains = gains_ref[...]
    lane = lax.broadcasted_iota(jnp.int32, (1, HEAD_PAD), 1)
    ones_col = (lane == V_DIM).astype(F32)

    def rot(t):
        tab = rope_ref[...]
        half = ROPE_DIM // 4
        return (t * tab[0] + pltpu.roll(t, HEAD_PAD - half, 1) * tab[1]
                + pltpu.roll(t, half, 1) * tab[2])

    for h in range(MLA_HEADS):
        sl = slice(h * HEAD_PAD, (h + 1) * HEAD_PAD)
        qh = _rms(q[:, sl], QK_DIM) * gains[0:1]
        kh = _rms(kn[:, sl] + kr, QK_DIM) * gains[1:2]
        q_ref[h] = (rot(qh) + gains[2:3]).astype(BF16)
        k_ref[h] = (rot(kh) + gains[3:4]).astype(BF16)
        vt_ref[h] = (vv[:, sl] + ones_col).T[0:V_ROWS].astype(BF16)


def _mla_prep(u_lat, u_ctx, col_block, wq, wk, wv, vq, vkv, gains, rope_tab):
    tm = math.gcd(STREAM_TILE, u_ctx.shape[0])
    n_lat_tiles = u_lat.shape[0] // tm
    n_rows = u_lat.shape[0] + u_ctx.shape[0]
    full = lambda a: pl.BlockSpec(a.shape, lambda i: (0,) * a.ndim)
    in_specs = _two_stream_specs(tm, MLA_PAD, col_block, n_lat_tiles) + [
        full(wq), full(wk), full(wv), full(vq), full(vkv), full(gains),
        pl.BlockSpec((3, tm, HEAD_PAD), lambda i: (0, i, 0))]
    args = [u_lat, u_ctx, wq, wk, wv, vq, vkv, gains, rope_tab]
    hs = pl.BlockSpec((MLA_HEADS, tm, HEAD_PAD), lambda i: (0, i, 0))
    shp = jax.ShapeDtypeStruct((MLA_HEADS, n_rows, HEAD_PAD), BF16)
    vs = pl.BlockSpec((MLA_HEADS, V_ROWS, tm), lambda i: (0, 0, i))
    vshp = jax.ShapeDtypeStruct((MLA_HEADS, V_ROWS, n_rows), BF16)
    return pl.pallas_call(
        functools.partial(_mla_prep_kernel, n_lat_tiles=n_lat_tiles),
        grid=(n_rows // tm,),
        in_specs=in_specs,
        out_specs=[hs, hs, vs],
        out_shape=[shp, shp, vshp],
        compiler_params=_cparams(("parallel",)),
        name="mla_prep",
    )(*args)


def _attn_kernel(q_ref, k_ref, vt_ref, o_ref, acc_scr, *rest, nk, sub, online):
    kv = pl.program_id(2)
    if online:
        m_scr, = rest

    @pl.when(kv == 0)
    def _():
        acc_scr[...] = jnp.zeros_like(acc_scr)
        if online:
            m_scr[...] = jnp.full_like(m_scr, -jnp.inf)

    k = k_ref[0]
    vt = vt_ref[0]
    n = q_ref.shape[1] // sub
    st, p, alpha = {}, {}, {}
    for t in range(n + 2):
        if t < n:
            st[t] = lax.dot_general(k, q_ref[0, t * sub:(t + 1) * sub], NT, preferred_element_type=F32)
        if 0 <= t - 1 < n:
            r = t - 1
            cols = slice(r * sub, (r + 1) * sub)
            s_r = st.pop(r)
            if online:
                m_prev = m_scr[:, cols]
                m_new = jnp.maximum(m_prev, jnp.max(s_r, axis=0, keepdims=True))
                alpha[r] = jnp.exp2(m_prev - m_new)
                m_scr[:, cols] = m_new
                s_r = s_r - m_new
            p[r] = jnp.exp2(s_r).astype(BF16)
        if 0 <= t - 2 < n:
            r = t - 2
            cols = slice(r * sub, (r + 1) * sub)
            upd = jnp.dot(vt, p.pop(r), preferred_element_type=F32)
            prev = acc_scr[0:V_ROWS, cols]
            acc_scr[0:V_ROWS, cols] = (alpha.pop(r) * prev if online else prev) + upd

    @pl.when(kv == nk - 1)
    def _():
        acc = acc_scr[...]
        o_ref[...] = (acc / acc[V_DIM:V_DIM + 1]).T


def _attention_call(q, k, vt, online, q_rows, k_rows):
    h, _, dp = q.shape
    lq = q_rows[1] - q_rows[0]
    lk = k_rows[1] - k_rows[0]
    tq = _row_tile(lq, ATT_TQ)
    sub = min(tq, ATT_SUB)
    tk = max(t for t in range(LANES, min(lk, ATT_TK) + 1, LANES) if lk % t == 0 and k_rows[0] % t == 0)
    nk = lk // tk
    assert q_rows[0] % tq == 0
    qo = q_rows[0] // tq
    ko = k_rows[0] // tk
    scratch = [pltpu.VMEM((HEAD_PAD, tq), F32)]
    if online:
        scratch.append(pltpu.VMEM((1, tq), F32))
    return pl.pallas_call(
        functools.partial(_attn_kernel, nk=nk, sub=sub, online=online),
        grid=(h, lq // tq, nk),
        in_specs=[pl.BlockSpec((1, tq, dp), lambda hh, i, j: (hh, i + qo, 0)),
                  pl.BlockSpec((1, tk, dp), lambda hh, i, j: (hh, j + ko, 0)),
                  pl.BlockSpec((1, V_ROWS, tk), lambda hh, i, j: (hh, 0, j + ko))],
        out_specs=pl.BlockSpec((tq, dp), lambda hh, i, j: (i, hh)),
        out_shape=jax.ShapeDtypeStruct((lq, h * dp), F32),
        scratch_shapes=scratch,
        compiler_params=_cparams(("parallel", "parallel", "arbitrary")),
        name="attn_online" if online else "attn",
    )(q, k, vt)


def _attention(q, k, vt, score_bound, q_rows, k_rows):
    return lax.cond(2.0 * score_bound <= MAX_SCORE_SPAN,
                    lambda: _attention_call(q, k, vt, False, q_rows, k_rows),
                    lambda: _attention_call(q, k, vt, True, q_rows, k_rows))


def _ft_chan_kernel(u_ref, m_ref, o_ref):
    o_ref[...] = _mm(u_ref[...], m_ref[...], 2, 2)


def _ft_chan(u, col_block, mat):
    n_rows = u.shape[0]
    tm = _row_tile(n_rows, 512)
    return pl.pallas_call(
        _ft_chan_kernel,
        grid=(n_rows // tm,),
        in_specs=[pl.BlockSpec((tm, FT_WIDTH), lambda i: (i, col_block)),
                  pl.BlockSpec(mat.shape, lambda i: (0, 0))],
        out_specs=pl.BlockSpec((tm, 2 * FT_WIDTH), lambda i: (i, 0)),
        out_shape=jax.ShapeDtypeStruct((n_rows, 2 * FT_WIDTH), F32),
        compiler_params=_cparams(("parallel",)),
        name="ft_chan",
    )(u, mat)


def _ft_s1_kernel(w_ref, cs_ref, tw_ref, o_ref, *, l1, g):
    z = _mm(cs_ref[...], w_ref[...], 2, 2)
    fw = FT_WIDTH
    for t in range(g):
        zr = z[:, t * 2 * fw:t * 2 * fw + fw]
        zi = z[:, t * 2 * fw + fw:(t + 1) * 2 * fw]
        a_re = zr[:l1] + zi[l1:]
        a_im = zi[:l1] - zr[l1:]
        tc = jnp.concatenate([tw_ref[0, t]] * (fw // LANES), axis=1)
        ts = jnp.concatenate([tw_ref[1, t]] * (fw // LANES), axis=1)
        o_ref[:, t * 2 * fw:t * 2 * fw + fw] = a_re * tc + a_im * ts
        o_ref[:, t * 2 * fw + fw:(t + 1) * 2 * fw] = a_im * tc - a_re * ts


def _ft_s1(w2d, cs, tw, l1, l2):
    g = min(8, l2)
    blk = g * 2 * FT_WIDTH
    return pl.pallas_call(
        functools.partial(_ft_s1_kernel, l1=l1, g=g),
        grid=(l2 // g,),
        in_specs=[pl.BlockSpec((l1, blk), lambda i: (0, i)),
                  pl.BlockSpec(cs.shape, lambda i: (0, 0)),
                  pl.BlockSpec((2, g, l1, LANES), lambda i: (0, i, 0, 0))],
        out_specs=pl.BlockSpec((l1, blk), lambda i: (0, i)),
        out_shape=jax.ShapeDtypeStruct((l1, l2 * 2 * FT_WIDTH), F32),
        compiler_params=_cparams(("parallel",)),
        name="ft_s1",
    )(w2d, cs, tw)


def _ft_s2_kernel(b_ref, cs_ref, o_ref, *, g, scale):
    fw = FT_WIDTH
    cmat = cs_ref[0]
    smat = cs_ref[1]
    for kk in range(g):
        blk = b_ref[kk]
        out = _mm(cmat, blk[:, :fw], 2, 2) + _mm(smat, blk[:, fw:], 2, 2)
        o_ref[:, kk * fw:(kk + 1) * fw] = out * scale


def _ft_s2(b3d, cs2, l1, l2, scale):
    g = min(8, l1)
    return pl.pallas_call(
        functools.partial(_ft_s2_kernel, g=g, scale=scale),
        grid=(l1 // g,),
        in_specs=[pl.BlockSpec((g, l2, 2 * FT_WIDTH), lambda i: (i, 0, 0)),
                  pl.BlockSpec(cs2.shape, lambda i: (0, 0, 0))],
        out_specs=pl.BlockSpec((l2, g * FT_WIDTH), lambda i: (0, i)),
        out_shape=jax.ShapeDtypeStruct((l2, l1 * FT_WIDTH), F32),
        compiler_params=_cparams(("parallel",)),
        name="ft_s2",
    )(b3d, cs2)


def _ft_tables(n):
    l1 = 1 << (int(math.log2(n)) // 2)
    l2 = n // l1
    assert l1 * l2 == n and l1 % 8 == 0 and l2 % 8 == 0
    k1 = np.arange(l1)
    ang1 = 2.0 * np.pi * np.outer(k1, np.arange(l1)) / l1
    cs1 = np.concatenate([np.cos(ang1), np.sin(ang1)], axis=0)
    angt = 2.0 * np.pi * np.outer(np.arange(l2), k1) / n
    tw = np.stack([np.cos(angt), np.sin(angt)], axis=0)
    tw = np.broadcast_to(tw[..., None], (2, l2, l1, LANES))
    ang2 = 2.0 * np.pi * np.outer(np.arange(l2), np.arange(l2)) / l2
    cs2 = np.stack([np.cos(ang2), np.sin(ang2)], axis=0)
    f = lambda a: jnp.asarray(np.ascontiguousarray(a), F32)
    return l1, l2, f(cs1), f(tw), f(cs2)


def _chan_dft_matrix():
    n = FT_GROUP_DIM
    ang = 2.0 * np.pi * np.outer(np.arange(n), np.arange(n)) / n
    eye = np.eye(FT_GROUPS)
    mat = np.concatenate([np.kron(eye, np.cos(ang)), -np.kron(eye, np.sin(ang))], axis=1)
    return jnp.asarray(mat, F32)


def _fourier_mix(u, col_block, chan_mat):
    n = u.shape[0]
    l1, l2, cs1, tw, cs2 = _ft_tables(n)
    wc = _ft_chan(u, col_block, chan_mat)
    b = _ft_s1(wc.reshape(l1, l2 * 2 * FT_WIDTH), cs1, tw, l1, l2)
    scale = 1.0 / math.sqrt(n * FT_GROUP_DIM)
    out = _ft_s2(b.reshape(l1, l2, 2 * FT_WIDTH), cs2, l1, l2, scale)
    return out.reshape(n, FT_WIDTH)


def _merge_kernel(ya_ref, yb_ref, yc_ref, ga_ref, gb_ref, gc_ref, x_ref, gt_ref,
                  wa_ref, wb_ref, wc_ref, wo_ref, o_ref):
    def branch(y_ref, w_ref, g_ref):
        t = jnp.dot(y_ref[...].astype(BF16), w_ref[...], preferred_element_type=F32)
        return _sigmoid(g_ref[...]) * t

    m = branch(ya_ref, wa_ref, ga_ref) + branch(yb_ref, wb_ref, gb_ref) + branch(yc_ref, wc_ref, gc_ref)
    out = jnp.dot(m.astype(BF16), wo_ref[...], preferred_element_type=F32)
    o_ref[...] = x_ref[...] + gt_ref[...] * out


def _merge(ya, ya_row_off, yb, yc, u, x, gt, wa, wb, wc, wo):
    n_rows, d = x.shape
    tm = _row_tile(n_rows, 512)
    assert ya_row_off % tm == 0
    ya_off = ya_row_off // tm
    full = lambda a: pl.BlockSpec(a.shape, lambda i: (0,) * a.ndim)
    rows = lambda a: pl.BlockSpec((tm, a.shape[1]), lambda i: (i, 0))
    gate = lambda b: pl.BlockSpec((tm, d), lambda i: (i, b))
    return pl.pallas_call(
        _merge_kernel,
        grid=(n_rows // tm,),
        in_specs=[pl.BlockSpec((tm, ya.shape[1]), lambda i: (i + ya_off, 0)),
                  rows(yb), rows(yc), gate(0), gate(1), gate(2), rows(x), full(gt),
                  full(wa), full(wb), full(wc), full(wo)],
        out_specs=pl.BlockSpec((tm, d), lambda i: (i, 0)),
        out_shape=jax.ShapeDtypeStruct((n_rows, d), F32),
        compiler_params=_cparams(("parallel",)),
        name="merge",
    )(ya, yb, yc, u, u, u, x, gt, wa, wb, wc, wo)


def _stream_mods(mod):
    return [mod[i * D_MODEL:(i + 1) * D_MODEL][None] for i in range(N_MOD)]


def _pad_heads(w, src_cols, width):
    k = w.shape[0]
    wh = w.reshape(k, MLA_HEADS, width)[:, :, src_cols]
    wh = jnp.pad(wh, ((0, 0), (0, 0), (0, HEAD_PAD - wh.shape[2])))
    return wh.reshape(k, MLA_HEADS * HEAD_PAD)


def _block_diag2(a, b):
    z01 = jnp.zeros((a.shape[0], b.shape[1]), a.dtype)
    z10 = jnp.zeros((b.shape[0], a.shape[1]), a.dtype)
    return jnp.concatenate([jnp.concatenate([a, z01], axis=1), jnp.concatenate([z10, b], axis=1)], axis=0)


def _rope_tables(n, n_ctx):
    half = ROPE_DIM // 2
    inv = ROPE_BASE ** (-np.arange(0, half, 2, dtype=np.float32) / half)
    t = np.arange(n)
    row = (t // GRID_W).astype(np.float32)
    col = (t % GRID_W).astype(np.float32)
    q = half // 2
    cos = np.ones((n, HEAD_PAD), np.float32)
    sin_a = np.zeros((n, HEAD_PAD), np.float32)
    sin_b = np.zeros((n, HEAD_PAD), np.float32)
    for base, pos in ((NOPE_DIM, row), (NOPE_DIM + half, col)):
        ang = pos[:, None] * inv[None, :]
        c, s = np.cos(ang), np.sin(ang)
        cos[:, base:base + q] = c
        cos[:, base + q:base + 2 * q] = c
        sin_a[:, base:base + q] = -s
        sin_b[:, base + q:base + 2 * q] = s
    tab = np.stack([cos, sin_a, sin_b], axis=0)
    ident = np.stack([np.ones((n_ctx, HEAD_PAD), np.float32)] + [np.zeros((n_ctx, HEAD_PAD), np.float32)] * 2)
    return jnp.asarray(np.concatenate([tab, ident], axis=1))


def _layer_weights(l, w):
    w_in = w["w_in"][l]
    o1 = RW_COLS
    o2 = o1 + MLA_COLS
    o3 = o2 + FT_WIDTH
    w_in_p = jnp.concatenate(
        [w_in[:, o3:], w_in[:, :o1], w_in[:, o1:o2],
         jnp.zeros((D_MODEL, MLA_PAD - MLA_COLS), F32), w_in[:, o2:o3]], axis=1).astype(BF16)
    ident = jnp.tile(jnp.array([[0.0], [1.0], [0.0]], F32), (1, IN_PAD))
    cw_in = ident.at[:, RW_OFF:RW_OFF + RW_COLS].set(w["rw_conv"][l])
    pad_gain = lambda g: jnp.pad(g, (0, HEAD_PAD - QK_DIM))
    wb = w["w_branch_b"][l].reshape(MLA_HEADS, V_DIM, D_MODEL)
    wb = jnp.pad(wb, ((0, 0), (0, HEAD_PAD - V_DIM), (0, 0))).reshape(MLA_HEADS * HEAD_PAD, D_MODEL)
    q_scale = math.log2(math.e) * QK_DIM ** -0.5
    score_bound = (BOUND_MARGIN * QK_DIM * q_scale * jnp.max(jnp.abs(w["q_gain"][l]))
                   * jnp.max(jnp.abs(w["k_gain"][l])))
    shift_lane = (jnp.arange(HEAD_PAD) == QK_DIM).astype(F32)
    return dict(
        w_in=w_in_p, cw_in=cw_in,
        wlora=_block_diag2(w["rw_w_up"][l, 0], w["rw_w_up"][l, 1]),
        alora=_block_diag2(w["rw_a_up"][l, 0], w["rw_a_up"][l, 1]),
        gup=w["rw_g_up"][l],
        rw_vec=jnp.stack([w["rw_k_k"][l], w["rw_k_a"][l], w["rw_r_k"][l].reshape(-1),
                          w["rw_w0"][l, 0], w["rw_w0"][l, 1], w["rw_a0"][l, 0], w["rw_a0"][l, 1],
                          jnp.zeros((RW_WIDTH,), F32)], axis=0),
        ln_vec=jnp.stack([w["rw_ln_w"][l], w["rw_ln_b"][l]], axis=0),
        wq=_pad_heads(w["w_uq"][l], slice(0, QK_DIM), QK_DIM).astype(BF16),
        wk=_pad_heads(w["w_ukv"][l], slice(0, NOPE_DIM), NOPE_DIM + V_DIM).astype(BF16),
        wv=_pad_heads(w["w_ukv"][l], slice(NOPE_DIM, NOPE_DIM + V_DIM), NOPE_DIM + V_DIM).astype(BF16),
        vq=w["mla_q_norm"][l][None], vkv=w["mla_kv_norm"][l][None],
        gains=jnp.stack([pad_gain(w["q_gain"][l]) * q_scale, pad_gain(w["k_gain"][l]),
                             -score_bound * shift_lane, shift_lane], axis=0),
        score_bound=score_bound,
        wa=w["w_branch_a"][l].astype(BF16), wb=wb.astype(BF16), wc=w["w_branch_c"][l].astype(BF16),
        wo=w["w_out"][l].astype(BF16),
        w_up=w["w_up"][l].astype(BF16), cw_ffn=w["ffn_conv"][l], w_down=w["w_down"][l].astype(BF16),
    )


def _forward(x, c, ctx, c_ctx, w):
    depth = w["w_ada"].shape[0]
    x_lat = x[0]
    x_ctx = ctx[0]
    n_lat = x_lat.shape[0]
    n_ctx = x_ctx.shape[0]
    assert n_lat % CHUNK == 0 and n_ctx % CHUNK == 0 and n_lat % GRID_W == 0
    n_all = n_lat + n_ctx

    cc = jnp.zeros((8, D_MODEL), F32).at[0].set(c[0]).at[1].set(c_ctx)
    mods = _mods(cc, w["w_ada"], w["b_ada"])

    head = np.arange(RW_WIDTH) // RW_HEAD_DIM
    ones_bd = jnp.asarray((head[:, None] == head[None, :]).astype(np.float32))
    avg_bd = ones_bd * (1.0 / RW_HEAD_DIM)
    ti = np.arange(CHUNK)
    tri = jnp.asarray(np.stack([ti[None, :] <= ti[:, None], ti[None, :] >= ti[:, None]]).astype(np.float32),
                      BF16)
    chan_mat = _chan_dft_matrix()
    rope_tab = _rope_tables(n_lat, n_ctx)

    for l in range(depth):
        last = l == depth - 1
        p = _layer_weights(l, w)
        streams = [(x_lat, _stream_mods(mods[l, 0]), (0, n_lat)),
                   (x_ctx, _stream_mods(mods[l, 1]), (n_lat, n_all))]
        us = []
        for xs, (sh1, sc1, _, _, _, _), _ in streams:
            gain1 = w["g_norm1"][l][None] * (1.0 + sc1)
            us.append(_in_proj(xs, gain1, sh1, p["w_in"], p["cw_in"]))
        com, dirs, post = _rw_prep(us[0], us[1], RW_OFF // RW_COLS, p["wlora"], p["alora"], p["gup"],
                                   ones_bd, p["rw_vec"])
        q, k, vt = _mla_prep(us[0], us[1], MLA_OFF // MLA_PAD, p["wq"], p["wk"], p["wv"], p["vq"], p["vkv"],
                             p["gains"], rope_tab)
        yf, yb = _rw_scan(com, dirs, tri, n_ctx // CHUNK)
        ya = _rw_post(yf, yb, post, avg_bd, p["ln_vec"], n_lat if last else n_all)

        new = []
        for si, (xs, (_, _, gt1, sh2, sc2, gt2), rows) in enumerate(streams):
            if si == 1 and last:
                continue
            y_att = _attention(q, k, vt, p["score_bound"], rows, (0, n_all) if si == 0 else rows)
            yc = _fourier_mix(us[si], FT_OFF // FT_WIDTH, chan_mat)
            x1 = _merge(ya, rows[0], y_att, yc, us[si], xs, gt1, p["wa"], p["wb"], p["wc"], p["wo"])
            gain2 = w["g_norm2"][l][None] * (1.0 + sc2)
            new.append(_ffn(x1, gain2, sh2, gt2, p["w_up"], p["cw_ffn"], p["w_down"]))
        x_lat = new[0]
        if not last:
            x_ctx = new[1]
    return x_lat[None]


def kernel(x, c, ctx, c_ctx, w_ada, b_ada, g_norm1, g_norm2, w_in, rw_conv, rw_w0, rw_w_up, rw_a0, rw_a_up,
           rw_g_up, rw_k_k, rw_k_a, rw_r_k, rw_ln_w, rw_ln_b, mla_q_norm, w_uq, mla_kv_norm, w_ukv, q_gain,
           k_gain, w_branch_a, w_branch_b, w_branch_c, w_out, ffn_conv, w_up, w_down):
    w = dict(w_ada=w_ada, b_ada=b_ada, g_norm1=g_norm1, g_norm2=g_norm2, w_in=w_in, rw_conv=rw_conv,
             rw_w0=rw_w0, rw_w_up=rw_w_up, rw_a0=rw_a0, rw_a_up=rw_a_up, rw_g_up=rw_g_up, rw_k_k=rw_k_k,
             rw_k_a=rw_k_a, rw_r_k=rw_r_k, rw_ln_w=rw_ln_w, rw_ln_b=rw_ln_b, mla_q_norm=mla_q_norm,
             w_uq=w_uq, mla_kv_norm=mla_kv_norm, w_ukv=w_ukv, q_gain=q_gain, k_gain=k_gain,
             w_branch_a=w_branch_a, w_branch_b=w_branch_b, w_branch_c=w_branch_c, w_out=w_out,
             ffn_conv=ffn_conv, w_up=w_up, w_down=w_down)
    return _forward(x, c, ctx, c_ctx, w)
```

```python
import functools
import math

import numpy as np
import jax
import jax.numpy as jnp
from jax import lax
from jax.experimental import pallas as pl
from jax.experimental.pallas import tpu as pltpu

F32 = jnp.float32
BF16 = jnp.bfloat16

D_MODEL = 1024
GRID_W = 64
NORM_EPS = 1e-6
RW_HEADS = 6
RW_HEAD_DIM = 64
RW_WIDTH = RW_HEADS * RW_HEAD_DIM
DECAY_LORA = 64
ICLR_LORA = 64
GATE_LORA = 128
RW_LN_EPS = 64e-5
MLA_HEADS = 6
Q_LORA = 384
KV_LORA = 256
NOPE_DIM = 64
ROPE_DIM = 32
V_DIM = 64
QK_DIM = NOPE_DIM + ROPE_DIM
ROPE_BASE = 10000.0
FT_GROUPS = 4
FT_GROUP_DIM = 64
FT_WIDTH = FT_GROUPS * FT_GROUP_DIM
D_FF = 2816
N_MOD = 6
RW_COLS = 3 * RW_WIDTH + 2 * DECAY_LORA + 2 * ICLR_LORA + GATE_LORA
MLA_COLS = Q_LORA + KV_LORA + ROPE_DIM

LANES = 128
HEAD_PAD = 128
MLA_PAD = 768
GATE_OFF = 0
RW_OFF = 3 * D_MODEL
MLA_OFF = RW_OFF + RW_COLS
FT_OFF = MLA_OFF + MLA_PAD
IN_PAD = FT_OFF + FT_WIDTH
HALO = 16
STREAM_TILE = 256
CHUNK = 128
VMEM_LIMIT = 48 * 1024 * 1024
V_ROWS = 80
BOUND_MARGIN = 1.02
MAX_SCORE_SPAN = 100.0
ATT_TQ, ATT_SUB, ATT_TK = 2048, 256, 3328

NN = (((1,), (0,)), ((), ()))
NT = (((1,), (1,)), ((), ()))


def _pieces(x, n):
    if x.dtype == BF16:
        return [x]
    out = []
    rem = x
    for i in range(n):
        p = rem.astype(BF16)
        out.append(p)
        if i + 1 < n:
            rem = rem - p.astype(F32)
    return out


def _mm(a, b, pa=1, pb=1, dims=NN):
    ap = _pieces(a, pa)
    bp = _pieces(b, pb)
    n = max(len(ap), len(bp))
    acc = None
    for i in reversed(range(len(ap))):
        for j in reversed(range(len(bp))):
            if i + j < n:
                t = lax.dot_general(ap[i], bp[j], dims, preferred_element_type=F32)
                acc = t if acc is None else acc + t
    return acc


def _sigmoid(x):
    return 0.5 * jnp.tanh(0.5 * x) + 0.5


def _cparams(sem):
    return pltpu.CompilerParams(dimension_semantics=sem, vmem_limit_bytes=VMEM_LIMIT)


def _row_tile(n, pref):
    t = min(n, pref)
    while n % t:
        t //= 2
    return t


def _mods_kernel(cc_ref, w_ref, b_ref, o_ref):
    cc = cc_ref[...]
    s = cc * _sigmoid(cc)
    o_ref[0] = _mm(s, w_ref[0], 2, 2) + b_ref[0]


def _mods(cc, w_ada, b_ada):
    depth, d, n = w_ada.shape
    tn = 1536
    return pl.pallas_call(
        _mods_kernel,
        grid=(depth, n // tn),
        in_specs=[pl.BlockSpec((8, d), lambda l, j: (0, 0)),
                  pl.BlockSpec((1, d, tn), lambda l, j: (l, 0, j)),
                  pl.BlockSpec((1, 1, tn), lambda l, j: (l, 0, j))],
        out_specs=pl.BlockSpec((1, 8, tn), lambda l, j: (l, 0, j)),
        out_shape=jax.ShapeDtypeStruct((depth, 8, n), F32),
        compiler_params=_cparams(("parallel", "parallel")),
        name="mods",
    )(cc, w_ada, b_ada.reshape(depth, 1, n))


def _modnorm(xv, g, s):
    ms = jnp.mean(xv * xv, axis=-1, keepdims=True)
    return (xv * lax.rsqrt(ms + NORM_EPS)) * g + s


def _fill_normed(h_scr, x_ref, xp_ref, xn_ref, g, s, i, nrt, tm):
    h_scr[HALO:HALO + tm] = _modnorm(x_ref[...], g, s).astype(BF16)
    hp = jnp.where(i > 0, _modnorm(xp_ref[...], g, s), 0.0)
    h_scr[0:HALO] = hp.astype(BF16)
    hn = jnp.where(i < nrt - 1, _modnorm(xn_ref[...], g, s), 0.0)
    h_scr[HALO + tm:2 * HALO + tm] = hn.astype(BF16)


def _conv3(u, cw, tm):
    rows = u.shape[0]
    up = pltpu.roll(u, 1, 0)
    un = pltpu.roll(u, rows - 1, 0)
    out = cw[0:1] * up + cw[1:2] * u + cw[2:3] * un
    return out[HALO:HALO + tm]


def _halo_specs(tm, d, n_rows):
    r = tm // HALO
    last = n_rows // HALO - 1
    return [pl.BlockSpec((tm, d), lambda i: (i, 0)),
            pl.BlockSpec((HALO, d), lambda i: (jnp.maximum(i * r - 1, 0), 0)),
            pl.BlockSpec((HALO, d), lambda i: (jnp.minimum((i + 1) * r, last), 0))]


def _resident(a):
    return pl.BlockSpec(a.shape, lambda i: (0,) * a.ndim, pipeline_mode=pl.Buffered(1))


def _in_kernel(x_ref, xp_ref, xn_ref, g_ref, s_ref, w_ref, cw_ref, o_ref, h_scr, *, tm, nrt, tn):
    _fill_normed(h_scr, x_ref, xp_ref, xn_ref, g_ref[...], s_ref[...], pl.program_id(0), nrt, tm)
    h = h_scr[...]
    nj = w_ref.shape[1] // tn
    u = {}
    for t in range(nj + 1):
        if t < nj:
            u[t] = jnp.dot(h, w_ref[:, t * tn:(t + 1) * tn], preferred_element_type=F32)
        if t >= 1:
            cols = slice((t - 1) * tn, t * tn)
            o_ref[:, cols] = _conv3(u.pop(t - 1), cw_ref[:, cols], tm)


def _in_proj(x, gain, shift, w, cw):
    n_rows, d = x.shape
    n = w.shape[1]
    tm = _row_tile(n_rows, 512)
    nrt = n_rows // tm
    return pl.pallas_call(
        functools.partial(_in_kernel, tm=tm, nrt=nrt, tn=512),
        grid=(nrt,),
        in_specs=_halo_specs(tm, d, n_rows) + [_resident(gain), _resident(shift), _resident(w), _resident(cw)],
        out_specs=pl.BlockSpec((tm, n), lambda i: (i, 0)),
        out_shape=jax.ShapeDtypeStruct((n_rows, n), F32),
        scratch_shapes=[pltpu.VMEM((tm + 2 * HALO, d), BF16)],
        compiler_params=_cparams(("parallel",)),
        name="in_proj",
    )(x, x, x, gain, shift, w, cw)


def _ffn_kernel(x_ref, xp_ref, xn_ref, g_ref, s_ref, gt_ref, wu_ref, cw_ref, wd_ref, o_ref, h_scr, *,
                tm, nrt, tf):
    _fill_normed(h_scr, x_ref, xp_ref, xn_ref, g_ref[...], s_ref[...], pl.program_id(0), nrt, tm)
    h = h_scr[...]
    nj = D_FF // tf
    u, gated = {}, {}
    for t in range(nj + 2):
        if t < nj:
            ca = slice(t * tf, (t + 1) * tf)
            cb = slice(D_FF + t * tf, D_FF + (t + 1) * tf)
            u[t] = (jnp.dot(h, wu_ref[:, ca], preferred_element_type=F32),
                    jnp.dot(h, wu_ref[:, cb], preferred_element_type=F32))
        if 0 <= t - 1 < nj:
            j = t - 1
            ua, ub = u.pop(j)
            a = _conv3(ua, cw_ref[:, j * tf:(j + 1) * tf], tm)
            b = _conv3(ub, cw_ref[:, D_FF + j * tf:D_FF + (j + 1) * tf], tm)
            gated[j] = (a * _sigmoid(a) * b).astype(BF16)
        if 0 <= t - 2 < nj:
            j = t - 2
            down = jnp.dot(gated.pop(j), wd_ref[j * tf:(j + 1) * tf, :], preferred_element_type=F32)
            if j == 0:
                o_ref[...] = down
            else:
                o_ref[...] += down
    o_ref[...] = x_ref[...] + gt_ref[...] * o_ref[...]


def _ffn(x, gain, shift, gate, w_up, cw, w_down):
    n_rows, d = x.shape
    tm = _row_tile(n_rows, 1024)
    nrt = n_rows // tm
    return pl.pallas_call(
        functools.partial(_ffn_kernel, tm=tm, nrt=nrt, tf=256),
        grid=(nrt,),
        in_specs=_halo_specs(tm, d, n_rows) + [_resident(gain), _resident(shift), _resident(gate),
                                               _resident(w_up), _resident(cw), _resident(w_down)],
        out_specs=pl.BlockSpec((tm, d), lambda i: (i, 0)),
        out_shape=jax.ShapeDtypeStruct((n_rows, d), F32),
        scratch_shapes=[pltpu.VMEM((tm + 2 * HALO, d), BF16)],
        compiler_params=_cparams(("parallel",)),
        name="ffn",
    )(x, x, x, gain, shift, gate, w_up, cw, w_down)


def _seg_sum(x, ones_bd):
    return _mm(x, ones_bd, 2, 1)


def _two_stream_specs(tm, width, col_block, n_lat_tiles):
    return [pl.BlockSpec((tm, width), lambda i: (jnp.minimum(i, n_lat_tiles - 1), col_block)),
            pl.BlockSpec((tm, width), lambda i: (jnp.maximum(i - n_lat_tiles, 0), col_block))]


def _rw_prep_kernel(ul_ref, uc_ref, wlora_ref, alora_ref, gup_ref, ones_ref, vec_ref,
                    com_ref, dir_ref, post_ref, *, n_lat_tiles):
    u = jnp.where(pl.program_id(0) < n_lat_tiles, ul_ref[...], uc_ref[...])
    w = RW_WIDTH
    r = u[:, 0:w]
    k = u[:, w:2 * w]
    v = u[:, 2 * w:3 * w]
    wd = u[:, 3 * w:3 * w + 2 * DECAY_LORA]
    ad = u[:, 3 * w + 2 * DECAY_LORA:3 * w + 2 * DECAY_LORA + 2 * ICLR_LORA]
    gd = u[:, 3 * w + 2 * DECAY_LORA + 2 * ICLR_LORA:]
    vec = vec_ref[...]
    k_k, k_a, r_k = vec[0:1], vec[1:2], vec[2:3]
    ones_bd = ones_ref[...]

    kk = k * k_k
    kk = kk * lax.rsqrt(_seg_sum(kk * kk, ones_bd) + 1e-12)
    g = _mm(_sigmoid(gd), gup_ref[...], 2, 2)
    wl = _mm(jnp.tanh(wd), wlora_ref[...], 2, 2)
    al = _mm(ad, alora_ref[...], 2, 2)
    com_ref[0] = r
    com_ref[1] = v
    com_ref[2] = -kk
    kd_sum = None
    for d in range(2):
        w0 = vec[3 + d:4 + d]
        a0 = vec[5 + d:6 + d]
        lw = -math.exp(-0.5) * _sigmoid(w0 + wl[:, d * w:(d + 1) * w])
        a = _sigmoid(a0 + al[:, d * w:(d + 1) * w])
        kd = k * (1.0 + (a - 1.0) * k_a)
        dir_ref[d, 0] = lw
        dir_ref[d, 1] = kd
        dir_ref[d, 2] = kk * a
        kd_sum = kd if kd_sum is None else kd_sum + kd
    post_ref[0] = g
    post_ref[1] = _seg_sum(r * kd_sum * r_k, ones_bd) * v


def _rw_prep(u_lat, u_ctx, col_block, wlora, alora, gup, ones_bd, vec):
    tm = math.gcd(STREAM_TILE, u_ctx.shape[0])
    n_lat_tiles = u_lat.shape[0] // tm
    n_rows = u_lat.shape[0] + u_ctx.shape[0]
    w = RW_WIDTH
    full = lambda a: pl.BlockSpec(a.shape, lambda i: (0,) * a.ndim)
    return pl.pallas_call(
        functools.partial(_rw_prep_kernel, n_lat_tiles=n_lat_tiles),
        grid=(n_rows // tm,),
        in_specs=_two_stream_specs(tm, RW_COLS, col_block, n_lat_tiles) + [
            full(wlora), full(alora), full(gup), full(ones_bd), full(vec)],
        out_specs=[pl.BlockSpec((3, tm, w), lambda i: (0, i, 0)),
                   pl.BlockSpec((2, 3, tm, w), lambda i: (0, 0, i, 0)),
                   pl.BlockSpec((2, tm, w), lambda i: (0, i, 0))],
        out_shape=[jax.ShapeDtypeStruct((3, n_rows, w), F32),
                   jax.ShapeDtypeStruct((2, 3, n_rows, w), F32),
                   jax.ShapeDtypeStruct((2, n_rows, w), F32)],
        compiler_params=_cparams(("parallel",)),
        name="rw_prep",
    )(u_lat, u_ctx, wlora, alora, gup, ones_bd, vec)


def _smm(a, b, dims=NN):
    return _mm(a, b, 1, 1, dims)


def _scan_dir(com, dr, tri, slot, reverse):
    c = CHUNK
    r, v, a = com[0], com[1], com[2]
    lw, kd, b = dr[0], dr[1], dr[2]
    cs = _mm(tri, lw, 1, 3)
    tot = cs[0:1] if reverse else cs[c - 1:c]
    e_pos = jnp.exp(cs)
    e_neg = jnp.exp(-cs)
    e_end = jnp.exp(tot - cs)
    at = a * jnp.exp(cs - lw)
    rt = r * e_pos
    bt = b * e_neg
    kt = kd * e_neg
    bh = b * e_end
    kh = kd * e_end
    p_end = jnp.exp(tot)

    row = lax.broadcasted_iota(jnp.int32, (c, c), 0)
    col = lax.broadcasted_iota(jnp.int32, (c, c), 1)
    before = (col > row) if reverse else (col < row)
    upto = (col >= row) if reverse else (col <= row)
    eye = row == col
    same_head = (row // RW_HEAD_DIM) == (col // RW_HEAD_DIM)
    both = lambda m: jnp.concatenate([m, m], axis=1)
    lane = lax.broadcasted_iota(jnp.int32, (c, LANES), 1)
    heads = [(lane // RW_HEAD_DIM) == h for h in range(2)]

    def by_head(x):
        return jnp.concatenate([jnp.where(m, x, 0.0) for m in heads], axis=0).astype(BF16)

    pairs = []
    for p in range(RW_HEADS // 2):
        sl = slice(p * LANES, (p + 1) * LANES)
        pairs.append(dict(
            slot=slot, p=p, by_head=by_head,
            atrt=jnp.concatenate([at[:, sl], rt[:, sl]], axis=0).astype(BF16),
            bk=jnp.concatenate([by_head(bt[:, sl]), by_head(kt[:, sl])], axis=0),
            v2=by_head(v[:, sl]), v_t=v[:, sl].T,
            bkh=jnp.concatenate([bh[:, sl], kh[:, sl]], axis=0).astype(BF16),
            p_end=p_end[:, sl], masks=(both(before), both(upto), both(eye), same_head)))
    return pairs


def _scan_chunk(pairs, s_ref):
    c = CHUNK
    zero = jnp.zeros((c, c), BF16)
    for pr in pairs:
        before, upto, eye, _ = pr["masks"]
        gm = _smm(pr["atrt"], pr["bk"], NT)
        a_ab = jnp.where(before, gm[:c, :2 * c], 0.0)
        pr["a_ak"] = jnp.where(before, gm[:c, 2 * c:], 0.0).astype(BF16)
        pr["a_rb"] = jnp.where(upto, gm[c:, :2 * c], 0.0).astype(BF16)
        pr["a_rk"] = jnp.where(upto, gm[c:, 2 * c:], 0.0).astype(BF16)
        pr["t_inv"] = jnp.where(eye, 1.0, a_ab)
        pr["apow"] = a_ab.astype(BF16)

    def block_diag(m):
        return jnp.concatenate([jnp.concatenate([m[:, :c], zero], axis=1),
                                jnp.concatenate([zero, m[:, c:]], axis=1)], axis=0)

    n_sq = int(math.log2(c)) - 1
    for pr in pairs:
        pr["apow"] = _smm(pr["apow"], block_diag(pr["apow"])).astype(BF16)
    for k in range(n_sq):
        for pr in pairs:
            w_bd = block_diag(pr["apow"])
            if k + 1 < n_sq:
                res = _smm(jnp.concatenate([pr["apow"], pr["t_inv"].astype(BF16)], axis=0), w_bd)
                pr["apow"] = res[:c].astype(BF16)
                pr["t_inv"] = pr["t_inv"] + res[c:]
            else:
                pr["t_inv"] = pr["t_inv"] + _smm(pr["t_inv"], w_bd)
    for pr in pairs:
        s0 = s_ref[pr["slot"], pr["p"]]
        pr["s0"] = s0
        hr = _smm(pr["atrt"], s0, NT)
        pr["ah"], pr["rh"] = hr[:c], hr[c:]
    for pr in pairs:
        x = _smm(pr["a_ak"], pr["v2"])
        pr["u"] = _smm(pr["t_inv"], pr["by_head"](pr["ah"] + x))
    outs = {}
    for pr in pairs:
        pr["y"] = pr["rh"] + _smm(pr["a_rb"], pr["by_head"](pr["u"])) + _smm(pr["a_rk"], pr["v2"])
    for pr in pairs:
        same_head = pr["masks"][3]
        uv_t = jnp.concatenate([pr["u"].T, pr["v_t"]], axis=1)
        upd = _smm(uv_t, pr["bkh"])
        s_ref[pr["slot"], pr["p"]] = pr["s0"] * pr["p_end"] + jnp.where(same_head, upd, 0.0)
        outs[(pr["slot"], pr["p"])] = pr["y"]
    n_pairs = RW_HEADS // 2
    return [jnp.concatenate([outs[(d, p)] for p in range(n_pairs)], axis=1) for d in range(2)]


def _rw_scan_kernel(comf_ref, dirf_ref, comb_ref, dirb_ref, tri_ref, yf_ref, yb_ref, s_ref):
    @pl.when(pl.program_id(0) == 0)
    def _():
        s_ref[...] = jnp.zeros_like(s_ref)

    pairs = (_scan_dir(comf_ref[...], dirf_ref[0], tri_ref[0], 0, False)
             + _scan_dir(comb_ref[...], dirb_ref[0], tri_ref[1], 1, True))
    yf, yb = _scan_chunk(pairs, s_ref)
    yf_ref[...] = yf
    yb_ref[...] = yb


def _rw_scan(com, dirs, tri, n_ctx_chunks):
    n_rows = com.shape[1]
    w = RW_WIDTH
    c = CHUNK
    assert c == LANES
    n_chunks = n_rows // c
    nc = n_ctx_chunks
    nl = n_chunks - nc

    def fwd_block(i):
        return jnp.where(i < nc, nl + i, i - nc)

    def bwd_block(i):
        return jnp.where(i < nc, nl + nc - 1 - i, nl - 1 - (i - nc))

    return pl.pallas_call(
        _rw_scan_kernel,
        grid=(n_chunks,),
        in_specs=[pl.BlockSpec((3, c, w), lambda i: (0, fwd_block(i), 0)),
                  pl.BlockSpec((1, 3, c, w), lambda i: (0, 0, fwd_block(i), 0)),
                  pl.BlockSpec((3, c, w), lambda i: (0, bwd_block(i), 0)),
                  pl.BlockSpec((1, 3, c, w), lambda i: (1, 0, bwd_block(i), 0)),
                  pl.BlockSpec((2, c, c), lambda i: (0, 0, 0))],
        out_specs=[pl.BlockSpec((c, w), lambda i: (fwd_block(i), 0)),
                   pl.BlockSpec((c, w), lambda i: (bwd_block(i), 0))],
        out_shape=[jax.ShapeDtypeStruct((n_rows, w), F32)] * 2,
        scratch_shapes=[pltpu.VMEM((2, RW_HEADS // 2, LANES, LANES), F32)],
        compiler_params=_cparams(("arbitrary",)),
        name="rw_scan",
    )(com, dirs, com, dirs, tri)


def _rw_post(y, g, bonus, avg, vec):
    mu = _mm(y, avg, 2, 1)
    yc = y - mu
    var = _mm(yc * yc, avg, 2, 1)
    yn = yc * lax.rsqrt(var + RW_LN_EPS)
    return (yn * vec[0:1] + vec[1:2] + bonus) * g


def _rms(x, n):
    return x * lax.rsqrt(jnp.sum(x * x, axis=-1, keepdims=True) * (1.0 / n) + NORM_EPS)


def _mla_prep_kernel(ul_ref, uc_ref, wq_ref, wk_ref, wv_ref, vq_ref, vkv_ref, gains_ref, ones_ref, rope_ref,
                     q_ref, k_ref, vt_ref, *, n_lat_tiles):
    u = jnp.where(pl.program_id(0) < n_lat_tiles, ul_ref[...], uc_ref[...])
    cq = _rms(u[:, :Q_LORA], Q_LORA) * vq_ref[...]
    q = jnp.dot(cq.astype(BF16), wq_ref[...], preferred_element_type=F32)
    ckv = _rms(u[:, Q_LORA:Q_LORA + KV_LORA], KV_LORA) * vkv_ref[...]
    ckv = ckv.astype(BF16)
    kn = jnp.dot(ckv, wk_ref[...], preferred_element_type=F32)
    vv = jnp.dot(ckv, wv_ref[...], preferred_element_type=F32)
    kr = pltpu.roll(u[:, Q_LORA + KV_LORA:], NOPE_DIM, 1)
    lane = lax.broadcasted_iota(jnp.int32, (1, HEAD_PAD), 1)
    ones_col = (lane == V_DIM).astype(F32)
    heads = lambda a: jnp.concatenate([a] * MLA_HEADS, axis=1)
    gains = gains_ref[...]
    gq, gk, shift_q, shift_k = (heads(gains[i:i + 1]) for i in range(4))
    tab = rope_ref[...]
    cos, sin_a, sin_b = heads(tab[0]), heads(tab[1]), heads(tab[2])
    width = MLA_HEADS * HEAD_PAD
    half = ROPE_DIM // 4

    def norm_rot(t, gain, shift):
        ms = _mm(t * t, ones_ref[...], 2, 1) * (1.0 / QK_DIM)
        t = t * lax.rsqrt(ms + NORM_EPS) * gain
        t = t * cos + pltpu.roll(t, width - half, 1) * sin_a + pltpu.roll(t, half, 1) * sin_b
        return (t + shift).astype(BF16)

    qo = norm_rot(q, gq, shift_q)
    ko = norm_rot(kn + heads(kr), gk, shift_k)
    for h in range(MLA_HEADS):
        sl = slice(h * HEAD_PAD, (h + 1) * HEAD_PAD)
        q_ref[h] = qo[:, sl]
        k_ref[h] = ko[:, sl]
        vt_ref[h] = (vv[:, sl] + ones_col).T[0:V_ROWS].astype(BF16)


def _mla_prep(u_lat, u_ctx, col_block, wq, wk, wv, vq, vkv, gains, head_ones, rope_tab):
    tm = math.gcd(STREAM_TILE, u_ctx.shape[0])
    n_lat_tiles = u_lat.shape[0] // tm
    n_rows = u_lat.shape[0] + u_ctx.shape[0]
    full = lambda a: pl.BlockSpec(a.shape, lambda i: (0,) * a.ndim)
    in_specs = _two_stream_specs(tm, MLA_PAD, col_block, n_lat_tiles) + [
        full(wq), full(wk), full(wv), full(vq), full(vkv), full(gains), full(head_ones),
        pl.BlockSpec((3, tm, HEAD_PAD), lambda i: (0, i, 0))]
    args = [u_lat, u_ctx, wq, wk, wv, vq, vkv, gains, head_ones, rope_tab]
    hs = pl.BlockSpec((MLA_HEADS, tm, HEAD_PAD), lambda i: (0, i, 0))
    shp = jax.ShapeDtypeStruct((MLA_HEADS, n_rows, HEAD_PAD), BF16)
    vs = pl.BlockSpec((MLA_HEADS, V_ROWS, tm), lambda i: (0, 0, i))
    vshp = jax.ShapeDtypeStruct((MLA_HEADS, V_ROWS, n_rows), BF16)
    return pl.pallas_call(
        functools.partial(_mla_prep_kernel, n_lat_tiles=n_lat_tiles),
        grid=(n_rows // tm,),
        in_specs=in_specs,
        out_specs=[hs, hs, vs],
        out_shape=[shp, shp, vshp],
        compiler_params=_cparams(("parallel",)),
        name="mla_prep",
    )(*args)


def _attn_kernel(q_ref, k_ref, vt_ref, o_ref, acc_scr, *rest, nk, sub, online):
    kv = pl.program_id(2)
    if online:
        m_scr, = rest

    @pl.when(kv == 0)
    def _():
        acc_scr[...] = jnp.zeros_like(acc_scr)
        if online:
            m_scr[...] = jnp.full_like(m_scr, -jnp.inf)

    k = k_ref[0]
    vt = vt_ref[0]
    n = q_ref.shape[1] // sub
    st, p, alpha = {}, {}, {}
    for t in range(n + 2):
        if t < n:
            st[t] = lax.dot_general(k, q_ref[0, t * sub:(t + 1) * sub], NT, preferred_element_type=F32)
        if 0 <= t - 1 < n:
            r = t - 1
            cols = slice(r * sub, (r + 1) * sub)
            s_r = st.pop(r)
            if online:
                m_prev = m_scr[:, cols]
                m_new = jnp.maximum(m_prev, jnp.max(s_r, axis=0, keepdims=True))
                alpha[r] = jnp.exp2(m_prev - m_new)
                m_scr[:, cols] = m_new
                s_r = s_r - m_new
            p[r] = jnp.exp2(s_r).astype(BF16)
        if 0 <= t - 2 < n:
            r = t - 2
            cols = slice(r * sub, (r + 1) * sub)
            upd = jnp.dot(vt, p.pop(r), preferred_element_type=F32)
            prev = acc_scr[0:V_ROWS, cols]
            acc_scr[0:V_ROWS, cols] = (alpha.pop(r) * prev if online else prev) + upd

    @pl.when(kv == nk - 1)
    def _():
        acc = acc_scr[...]
        o_ref[...] = (acc / acc[V_DIM:V_DIM + 1]).T


def _attention_call(q, k, vt, online, q_rows, k_rows):
    h, _, dp = q.shape
    lq = q_rows[1] - q_rows[0]
    lk = k_rows[1] - k_rows[0]
    tq = _row_tile(lq, ATT_TQ)
    sub = min(tq, ATT_SUB)
    tk = max(t for t in range(LANES, min(lk, ATT_TK) + 1, LANES) if lk % t == 0 and k_rows[0] % t == 0)
    nk = lk // tk
    assert q_rows[0] % tq == 0
    qo = q_rows[0] // tq
    ko = k_rows[0] // tk
    scratch = [pltpu.VMEM((HEAD_PAD, tq), F32)]
    if online:
        scratch.append(pltpu.VMEM((1, tq), F32))
    return pl.pallas_call(
        functools.partial(_attn_kernel, nk=nk, sub=sub, online=online),
        grid=(h, lq // tq, nk),
        in_specs=[pl.BlockSpec((1, tq, dp), lambda hh, i, j: (hh, i + qo, 0)),
                  pl.BlockSpec((1, tk, dp), lambda hh, i, j: (hh, j + ko, 0)),
                  pl.BlockSpec((1, V_ROWS, tk), lambda hh, i, j: (hh, 0, j + ko))],
        out_specs=pl.BlockSpec((tq, dp), lambda hh, i, j: (i, hh)),
        out_shape=jax.ShapeDtypeStruct((lq, h * dp), F32),
        scratch_shapes=scratch,
        compiler_params=_cparams(("parallel", "parallel", "arbitrary")),
        name="attn_online" if online else "attn",
    )(q, k, vt)


def _attention(q, k, vt, score_bound, q_rows, k_rows):
    return lax.cond(2.0 * score_bound <= MAX_SCORE_SPAN,
                    lambda: _attention_call(q, k, vt, False, q_rows, k_rows),
                    lambda: _attention_call(q, k, vt, True, q_rows, k_rows))


def _ft_chan_kernel(u_ref, m_ref, o_ref):
    o_ref[...] = _mm(u_ref[...], m_ref[...], 2, 2)


def _ft_chan(u, col_block, mat):
    n_rows = u.shape[0]
    tm = _row_tile(n_rows, 512)
    return pl.pallas_call(
        _ft_chan_kernel,
        grid=(n_rows // tm,),
        in_specs=[pl.BlockSpec((tm, FT_WIDTH), lambda i: (i, col_block)),
                  pl.BlockSpec(mat.shape, lambda i: (0, 0))],
        out_specs=pl.BlockSpec((tm, 2 * FT_WIDTH), lambda i: (i, 0)),
        out_shape=jax.ShapeDtypeStruct((n_rows, 2 * FT_WIDTH), F32),
        compiler_params=_cparams(("parallel",)),
        name="ft_chan",
    )(u, mat)


def _ft_s1_kernel(w_ref, cs_ref, tw_ref, o_ref, *, l1, g):
    z = _mm(cs_ref[...], w_ref[...], 2, 2)
    fw = FT_WIDTH
    for t in range(g):
        zr = z[:, t * 2 * fw:t * 2 * fw + fw]
        zi = z[:, t * 2 * fw + fw:(t + 1) * 2 * fw]
        a_re = zr[:l1] + zi[l1:]
        a_im = zi[:l1] - zr[l1:]
        tc = jnp.concatenate([tw_ref[0, t]] * (fw // LANES), axis=1)
        ts = jnp.concatenate([tw_ref[1, t]] * (fw // LANES), axis=1)
        o_ref[:, t * 2 * fw:t * 2 * fw + fw] = a_re * tc + a_im * ts
        o_ref[:, t * 2 * fw + fw:(t + 1) * 2 * fw] = a_im * tc - a_re * ts


def _ft_s1(w2d, cs, tw, l1, l2):
    g = min(8, l2)
    blk = g * 2 * FT_WIDTH
    return pl.pallas_call(
        functools.partial(_ft_s1_kernel, l1=l1, g=g),
        grid=(l2 // g,),
        in_specs=[pl.BlockSpec((l1, blk), lambda i: (0, i)),
                  pl.BlockSpec(cs.shape, lambda i: (0, 0)),
                  pl.BlockSpec((2, g, l1, LANES), lambda i: (0, i, 0, 0))],
        out_specs=pl.BlockSpec((l1, blk), lambda i: (0, i)),
        out_shape=jax.ShapeDtypeStruct((l1, l2 * 2 * FT_WIDTH), F32),
        compiler_params=_cparams(("parallel",)),
        name="ft_s1",
    )(w2d, cs, tw)


def _ft_s2_kernel(b_ref, cs_ref, o_ref, *, g, scale):
    fw = FT_WIDTH
    cmat = cs_ref[0]
    smat = cs_ref[1]
    for kk in range(g):
        blk = b_ref[kk]
        out = _mm(cmat, blk[:, :fw], 2, 2) + _mm(smat, blk[:, fw:], 2, 2)
        o_ref[:, kk * fw:(kk + 1) * fw] = out * scale


def _ft_s2(b3d, cs2, l1, l2, scale):
    g = min(8, l1)
    return pl.pallas_call(
        functools.partial(_ft_s2_kernel, g=g, scale=scale),
        grid=(l1 // g,),
        in_specs=[pl.BlockSpec((g, l2, 2 * FT_WIDTH), lambda i: (i, 0, 0)),
                  pl.BlockSpec(cs2.shape, lambda i: (0, 0, 0))],
        out_specs=pl.BlockSpec((l2, g * FT_WIDTH), lambda i: (0, i)),
        out_shape=jax.ShapeDtypeStruct((l2, l1 * FT_WIDTH), F32),
        compiler_params=_cparams(("parallel",)),
        name="ft_s2",
    )(b3d, cs2)


def _ft_tables(n):
    l1 = 1 << (int(math.log2(n)) // 2)
    l2 = n // l1
    assert l1 * l2 == n and l1 % 8 == 0 and l2 % 8 == 0
    k1 = np.arange(l1)
    ang1 = 2.0 * np.pi * np.outer(k1, np.arange(l1)) / l1
    cs1 = np.concatenate([np.cos(ang1), np.sin(ang1)], axis=0)
    angt = 2.0 * np.pi * np.outer(np.arange(l2), k1) / n
    tw = np.stack([np.cos(angt), np.sin(angt)], axis=0)
    tw = np.broadcast_to(tw[..., None], (2, l2, l1, LANES))
    ang2 = 2.0 * np.pi * np.outer(np.arange(l2), np.arange(l2)) / l2
    cs2 = np.stack([np.cos(ang2), np.sin(ang2)], axis=0)
    f = lambda a: jnp.asarray(np.ascontiguousarray(a), F32)
    return l1, l2, f(cs1), f(tw), f(cs2)


def _chan_dft_matrix():
    n = FT_GROUP_DIM
    ang = 2.0 * np.pi * np.outer(np.arange(n), np.arange(n)) / n
    eye = np.eye(FT_GROUPS)
    mat = np.concatenate([np.kron(eye, np.cos(ang)), -np.kron(eye, np.sin(ang))], axis=1)
    return jnp.asarray(mat, F32)


def _fourier_mix(u, col_block, chan_mat):
    n = u.shape[0]
    l1, l2, cs1, tw, cs2 = _ft_tables(n)
    wc = _ft_chan(u, col_block, chan_mat)
    b = _ft_s1(wc.reshape(l1, l2 * 2 * FT_WIDTH), cs1, tw, l1, l2)
    scale = 1.0 / math.sqrt(n * FT_GROUP_DIM)
    out = _ft_s2(b.reshape(l1, l2, 2 * FT_WIDTH), cs2, l1, l2, scale)
    return out.reshape(n, FT_WIDTH)


def _merge_kernel(yf_ref, ybw_ref, post_ref, avg_ref, ln_ref, yb_ref, yc_ref, ga_ref, gb_ref, gc_ref,
                  x_ref, gt_ref, wa_ref, wb_ref, wc_ref, wo_ref, o_ref):
    def branch(y, w_ref, g_ref):
        t = jnp.dot(y.astype(BF16), w_ref[...], preferred_element_type=F32)
        return _sigmoid(g_ref[...]) * t

    ya = _rw_post(yf_ref[...] + ybw_ref[...], post_ref[0], post_ref[1], avg_ref[...], ln_ref[...])
    m = branch(ya, wa_ref, ga_ref) + branch(yb_ref[...], wb_ref, gb_ref) + branch(yc_ref[...], wc_ref, gc_ref)
    out = jnp.dot(m.astype(BF16), wo_ref[...], preferred_element_type=F32)
    o_ref[...] = x_ref[...] + gt_ref[...] * out


def _merge(yf, ybw, post, row_off, avg_bd, ln_vec, yb, yc, u, x, gt, wa, wb, wc, wo):
    n_rows, d = x.shape
    tm = _row_tile(n_rows, 512)
    assert row_off % tm == 0
    off = row_off // tm
    w = RW_WIDTH
    full = lambda a: pl.BlockSpec(a.shape, lambda i: (0,) * a.ndim)
    rows = lambda a: pl.BlockSpec((tm, a.shape[1]), lambda i: (i, 0))
    gate = lambda b: pl.BlockSpec((tm, d), lambda i: (i, b))
    scan = pl.BlockSpec((tm, w), lambda i: (i + off, 0))
    return pl.pallas_call(
        _merge_kernel,
        grid=(n_rows // tm,),
        in_specs=[scan, scan, pl.BlockSpec((2, tm, w), lambda i: (0, i + off, 0)), full(avg_bd), full(ln_vec),
                  rows(yb), rows(yc), gate(0), gate(1), gate(2), rows(x), full(gt),
                  full(wa), full(wb), full(wc), full(wo)],
        out_specs=pl.BlockSpec((tm, d), lambda i: (i, 0)),
        out_shape=jax.ShapeDtypeStruct((n_rows, d), F32),
        compiler_params=_cparams(("parallel",)),
        name="merge",
    )(yf, ybw, post, avg_bd, ln_vec, yb, yc, u, u, u, x, gt, wa, wb, wc, wo)


def _stream_mods(mod):
    return [mod[i * D_MODEL:(i + 1) * D_MODEL][None] for i in range(N_MOD)]


def _pad_heads(w, src_cols, width):
    k = w.shape[0]
    wh = w.reshape(k, MLA_HEADS, width)[:, :, src_cols]
    wh = jnp.pad(wh, ((0, 0), (0, 0), (0, HEAD_PAD - wh.shape[2])))
    return wh.reshape(k, MLA_HEADS * HEAD_PAD)


def _block_diag2(a, b):
    z01 = jnp.zeros((a.shape[0], b.shape[1]), a.dtype)
    z10 = jnp.zeros((b.shape[0], a.shape[1]), a.dtype)
    return jnp.concatenate([jnp.concatenate([a, z01], axis=1), jnp.concatenate([z10, b], axis=1)], axis=0)


def _rope_tables(n, n_ctx):
    half = ROPE_DIM // 2
    inv = ROPE_BASE ** (-np.arange(0, half, 2, dtype=np.float32) / half)
    t = np.arange(n)
    row = (t // GRID_W).astype(np.float32)
    col = (t % GRID_W).astype(np.float32)
    q = half // 2
    cos = np.ones((n, HEAD_PAD), np.float32)
    sin_a = np.zeros((n, HEAD_PAD), np.float32)
    sin_b = np.zeros((n, HEAD_PAD), np.float32)
    for base, pos in ((NOPE_DIM, row), (NOPE_DIM + half, col)):
        ang = pos[:, None] * inv[None, :]
        c, s = np.cos(ang), np.sin(ang)
        cos[:, base:base + q] = c
        cos[:, base + q:base + 2 * q] = c
        sin_a[:, base:base + q] = -s
        sin_b[:, base + q:base + 2 * q] = s
    tab = np.stack([cos, sin_a, sin_b], axis=0)
    ident = np.stack([np.ones((n_ctx, HEAD_PAD), np.float32)] + [np.zeros((n_ctx, HEAD_PAD), np.float32)] * 2)
    return jnp.asarray(np.concatenate([tab, ident], axis=1))


def _layer_weights(l, w):
    w_in = w["w_in"][l]
    o1 = RW_COLS
    o2 = o1 + MLA_COLS
    o3 = o2 + FT_WIDTH
    w_in_p = jnp.concatenate(
        [w_in[:, o3:], w_in[:, :o1], w_in[:, o1:o2],
         jnp.zeros((D_MODEL, MLA_PAD - MLA_COLS), F32), w_in[:, o2:o3]], axis=1).astype(BF16)
    ident = jnp.tile(jnp.array([[0.0], [1.0], [0.0]], F32), (1, IN_PAD))
    cw_in = ident.at[:, RW_OFF:RW_OFF + RW_COLS].set(w["rw_conv"][l])
    pad_gain = lambda g: jnp.pad(g, (0, HEAD_PAD - QK_DIM))
    wb = w["w_branch_b"][l].reshape(MLA_HEADS, V_DIM, D_MODEL)
    wb = jnp.pad(wb, ((0, 0), (0, HEAD_PAD - V_DIM), (0, 0))).reshape(MLA_HEADS * HEAD_PAD, D_MODEL)
    q_scale = math.log2(math.e) * QK_DIM ** -0.5
    score_bound = (BOUND_MARGIN * QK_DIM * q_scale * jnp.max(jnp.abs(w["q_gain"][l]))
                   * jnp.max(jnp.abs(w["k_gain"][l])))
    shift_lane = (jnp.arange(HEAD_PAD) == QK_DIM).astype(F32)
    return dict(
        w_in=w_in_p, cw_in=cw_in,
        wlora=_block_diag2(w["rw_w_up"][l, 0], w["rw_w_up"][l, 1]),
        alora=_block_diag2(w["rw_a_up"][l, 0], w["rw_a_up"][l, 1]),
        gup=w["rw_g_up"][l],
        rw_vec=jnp.stack([w["rw_k_k"][l], w["rw_k_a"][l], w["rw_r_k"][l].reshape(-1),
                          w["rw_w0"][l, 0], w["rw_w0"][l, 1], w["rw_a0"][l, 0], w["rw_a0"][l, 1],
                          jnp.zeros((RW_WIDTH,), F32)], axis=0),
        ln_vec=jnp.stack([w["rw_ln_w"][l], w["rw_ln_b"][l]], axis=0),
        wq=_pad_heads(w["w_uq"][l], slice(0, QK_DIM), QK_DIM).astype(BF16),
        wk=_pad_heads(w["w_ukv"][l], slice(0, NOPE_DIM), NOPE_DIM + V_DIM).astype(BF16),
        wv=_pad_heads(w["w_ukv"][l], slice(NOPE_DIM, NOPE_DIM + V_DIM), NOPE_DIM + V_DIM).astype(BF16),
        vq=w["mla_q_norm"][l][None], vkv=w["mla_kv_norm"][l][None],
        gains=jnp.stack([pad_gain(w["q_gain"][l]) * q_scale, pad_gain(w["k_gain"][l]),
                             -score_bound * shift_lane, shift_lane], axis=0),
        score_bound=score_bound,
        wa=w["w_branch_a"][l].astype(BF16), wb=wb.astype(BF16), wc=w["w_branch_c"][l].astype(BF16),
        wo=w["w_out"][l].astype(BF16),
        w_up=w["w_up"][l].astype(BF16), cw_ffn=w["ffn_conv"][l], w_down=w["w_down"][l].astype(BF16),
    )


def _forward(x, c, ctx, c_ctx, w):
    depth = w["w_ada"].shape[0]
    x_lat = x[0]
    x_ctx = ctx[0]
    n_lat = x_lat.shape[0]
    n_ctx = x_ctx.shape[0]
    assert n_lat % CHUNK == 0 and n_ctx % CHUNK == 0 and n_lat % GRID_W == 0
    n_all = n_lat + n_ctx

    cc = jnp.zeros((8, D_MODEL), F32).at[0].set(c[0]).at[1].set(c_ctx)
    mods = _mods(cc, w["w_ada"], w["b_ada"])

    head = np.arange(RW_WIDTH) // RW_HEAD_DIM
    ones_bd = jnp.asarray((head[:, None] == head[None, :]).astype(np.float32), BF16)
    avg_bd = ones_bd * (1.0 / RW_HEAD_DIM)
    mla_head = np.arange(MLA_HEADS * HEAD_PAD) // HEAD_PAD
    head_ones = jnp.asarray((mla_head[:, None] == mla_head[None, :]).astype(np.float32), BF16)
    ti = np.arange(CHUNK)
    tri = jnp.asarray(np.stack([ti[None, :] <= ti[:, None], ti[None, :] >= ti[:, None]]).astype(np.float32),
                      BF16)
    chan_mat = _chan_dft_matrix()
    rope_tab = _rope_tables(n_lat, n_ctx)

    for l in range(depth):
        last = l == depth - 1
        p = _layer_weights(l, w)
        streams = [(x_lat, _stream_mods(mods[l, 0]), (0, n_lat)),
                   (x_ctx, _stream_mods(mods[l, 1]), (n_lat, n_all))]
        us = []
        for xs, (sh1, sc1, _, _, _, _), _ in streams:
            gain1 = w["g_norm1"][l][None] * (1.0 + sc1)
            us.append(_in_proj(xs, gain1, sh1, p["w_in"], p["cw_in"]))
        com, dirs, post = _rw_prep(us[0], us[1], RW_OFF // RW_COLS, p["wlora"], p["alora"], p["gup"],
                                   ones_bd, p["rw_vec"])
        q, k, vt = _mla_prep(us[0], us[1], MLA_OFF // MLA_PAD, p["wq"], p["wk"], p["wv"], p["vq"], p["vkv"],
                             p["gains"], head_ones, rope_tab)
        yf, yb = _rw_scan(com, dirs, tri, n_ctx // CHUNK)

        new = []
        for si, (xs, (_, _, gt1, sh2, sc2, gt2), rows) in enumerate(streams):
            if si == 1 and last:
                continue
            y_att = _attention(q, k, vt, p["score_bound"], rows, (0, n_all) if si == 0 else rows)
            yc = _fourier_mix(us[si], FT_OFF // FT_WIDTH, chan_mat)
            x1 = _merge(yf, yb, post, rows[0], avg_bd, p["ln_vec"], y_att, yc, us[si], xs, gt1,
                        p["wa"], p["wb"], p["wc"], p["wo"])
            gain2 = w["g_norm2"][l][None] * (1.0 + sc2)
            new.append(_ffn(x1, gain2, sh2, gt2, p["w_up"], p["cw_ffn"], p["w_down"]))
        x_lat = new[0]
        if not last:
            x_ctx = new[1]
    return x_lat[None]


def kernel(x, c, ctx, c_ctx, w_ada, b_ada, g_norm1, g_norm2, w_in, rw_conv, rw_w0, rw_w_up, rw_a0, rw_a_up,
           rw_g_up, rw_k_k, rw_k_a, rw_r_k, rw_ln_w, rw_ln_b, mla_q_norm, w_uq, mla_kv_norm, w_ukv, q_gain,
           k_gain, w_branch_a, w_branch_b, w_branch_c, w_out, ffn_conv, w_up, w_down):
    w = dict(w_ada=w_ada, b_ada=b_ada, g_norm1=g_norm1, g_norm2=g_norm2, w_in=w_in, rw_conv=rw_conv,
             rw_w0=rw_w0, rw_w_up=rw_w_up, rw_a0=rw_a0, rw_a_up=rw_a_up, rw_g_up=rw_g_up, rw_k_k=rw_k_k,
             rw_k_a=rw_k_a, rw_r_k=rw_r_k, rw_ln_w=rw_ln_w, rw_ln_b=rw_ln_b, mla_q_norm=mla_q_norm,
             w_uq=w_uq, mla_kv_norm=mla_kv_norm, w_ukv=w_ukv, q_gain=q_gain, k_gain=k_gain,
             w_branch_a=w_branch_a, w_branch_b=w_branch_b, w_branch_c=w_branch_c, w_out=w_out,
             ffn_conv=ffn_conv, w_up=w_up, w_down=w_down)
    return _forward(x, c, ctx, c_ctx, w)
```

```python
import functools
import math

import numpy as np
import jax
import jax.numpy as jnp
from jax import lax
from jax.experimental import pallas as pl
from jax.experimental.pallas import tpu as pltpu

F32 = jnp.float32
BF16 = jnp.bfloat16

D_MODEL = 1024
GRID_W = 64
NORM_EPS = 1e-6
RW_HEADS = 6
RW_HEAD_DIM = 64
RW_WIDTH = RW_HEADS * RW_HEAD_DIM
DECAY_LORA = 64
ICLR_LORA = 64
GATE_LORA = 128
RW_LN_EPS = 64e-5
MLA_HEADS = 6
Q_LORA = 384
KV_LORA = 256
NOPE_DIM = 64
ROPE_DIM = 32
V_DIM = 64
QK_DIM = NOPE_DIM + ROPE_DIM
ROPE_BASE = 10000.0
FT_GROUPS = 4
FT_GROUP_DIM = 64
FT_WIDTH = FT_GROUPS * FT_GROUP_DIM
D_FF = 2816
N_MOD = 6
RW_COLS = 3 * RW_WIDTH + 2 * DECAY_LORA + 2 * ICLR_LORA + GATE_LORA
MLA_COLS = Q_LORA + KV_LORA + ROPE_DIM

LANES = 128
HEAD_PAD = 128
MLA_PAD = 768
GATE_COLS = 3 * D_MODEL
RW_OFF = GATE_COLS
U_RW = 0
U_MLA = U_RW + RW_COLS
U_FT = U_MLA + MLA_PAD
HALO = 16
STREAM_TILE = 256
CHUNK = 128
VMEM_LIMIT = 48 * 1024 * 1024
V_ROWS = 80
BOUND_MARGIN = 1.02
MAX_SCORE_SPAN = 100.0
ATT_TQ, ATT_SUB, ATT_TK = 2048, 256, 3328

NN = (((1,), (0,)), ((), ()))
NT = (((1,), (1,)), ((), ()))


def _pieces(x, n):
    if x.dtype == BF16:
        return [x]
    out = []
    rem = x
    for i in range(n):
        p = rem.astype(BF16)
        out.append(p)
        if i + 1 < n:
            rem = rem - p.astype(F32)
    return out


def _mm(a, b, pa=1, pb=1, dims=NN):
    ap = _pieces(a, pa)
    bp = _pieces(b, pb)
    n = max(len(ap), len(bp))
    acc = None
    for i in reversed(range(len(ap))):
        for j in reversed(range(len(bp))):
            if i + j < n:
                t = lax.dot_general(ap[i], bp[j], dims, preferred_element_type=F32)
                acc = t if acc is None else acc + t
    return acc


def _sigmoid(x):
    return 0.5 * jnp.tanh(0.5 * x) + 0.5


def _cparams(sem):
    return pltpu.CompilerParams(dimension_semantics=sem, vmem_limit_bytes=VMEM_LIMIT)


def _row_tile(n, pref):
    t = min(n, pref)
    while n % t:
        t //= 2
    return t


def _mods_kernel(cc_ref, w_ref, b_ref, o_ref):
    cc = cc_ref[...]
    s = cc * _sigmoid(cc)
    o_ref[0] = _mm(s, w_ref[0], 2, 2) + b_ref[0]


def _mods(cc, w_ada, b_ada):
    depth, d, n = w_ada.shape
    tn = 1536
    return pl.pallas_call(
        _mods_kernel,
        grid=(depth, n // tn),
        in_specs=[pl.BlockSpec((8, d), lambda l, j: (0, 0)),
                  pl.BlockSpec((1, d, tn), lambda l, j: (l, 0, j)),
                  pl.BlockSpec((1, 1, tn), lambda l, j: (l, 0, j))],
        out_specs=pl.BlockSpec((1, 8, tn), lambda l, j: (l, 0, j)),
        out_shape=jax.ShapeDtypeStruct((depth, 8, n), F32),
        compiler_params=_cparams(("parallel", "parallel")),
        name="mods",
    )(cc, w_ada, b_ada.reshape(depth, 1, n))


def _modnorm(xv, g, s):
    ms = jnp.mean(xv * xv, axis=-1, keepdims=True)
    return (xv * lax.rsqrt(ms + NORM_EPS)) * g + s


def _fill_normed(h_scr, x_ref, xp_ref, xn_ref, g, s, i, nrt, tm):
    h_scr[HALO:HALO + tm] = _modnorm(x_ref[...], g, s).astype(BF16)
    hp = jnp.where(i > 0, _modnorm(xp_ref[...], g, s), 0.0)
    h_scr[0:HALO] = hp.astype(BF16)
    hn = jnp.where(i < nrt - 1, _modnorm(xn_ref[...], g, s), 0.0)
    h_scr[HALO + tm:2 * HALO + tm] = hn.astype(BF16)


def _conv3(u, cw, tm):
    rows = u.shape[0]
    up = pltpu.roll(u, 1, 0)
    un = pltpu.roll(u, rows - 1, 0)
    out = cw[0:1] * up + cw[1:2] * u + cw[2:3] * un
    return out[HALO:HALO + tm]


def _halo_specs(tm, d, n_rows):
    r = tm // HALO
    last = n_rows // HALO - 1
    return [pl.BlockSpec((tm, d), lambda i: (i, 0)),
            pl.BlockSpec((HALO, d), lambda i: (jnp.maximum(i * r - 1, 0), 0)),
            pl.BlockSpec((HALO, d), lambda i: (jnp.minimum((i + 1) * r, last), 0))]


def _resident(a):
    return pl.BlockSpec(a.shape, lambda i: (0,) * a.ndim, pipeline_mode=pl.Buffered(1))


def _in_kernel(x_ref, xp_ref, xn_ref, g_ref, s_ref, w_ref, cw_ref, gate_ref, o_ref, h_scr, *, tm, nrt, tn):
    _fill_normed(h_scr, x_ref, xp_ref, xn_ref, g_ref[...], s_ref[...], pl.program_id(0), nrt, tm)
    nj = w_ref.shape[1] // tn
    u = {}
    for t in range(nj + 1):
        if t < nj:
            conv = RW_OFF <= t * tn < RW_OFF + RW_COLS
            h = h_scr[...] if conv else h_scr[HALO:HALO + tm]
            u[t] = jnp.dot(h, w_ref[:, t * tn:(t + 1) * tn], preferred_element_type=F32)
        if t >= 1:
            lo = (t - 1) * tn
            val = u.pop(t - 1)
            if lo < GATE_COLS:
                gate_ref[:, lo:lo + tn] = _sigmoid(val).astype(BF16)
            elif lo < RW_OFF + RW_COLS:
                o_ref[:, lo - GATE_COLS:lo - GATE_COLS + tn] = _conv3(val, cw_ref[:, lo - RW_OFF:lo - RW_OFF + tn], tm)
            else:
                o_ref[:, lo - GATE_COLS:lo - GATE_COLS + tn] = val


def _in_proj(x, gain, shift, w, cw):
    n_rows, d = x.shape
    n = w.shape[1]
    tn = 512
    assert GATE_COLS % tn == 0 and RW_COLS % tn == 0 and n % tn == 0
    tm = _row_tile(n_rows, 512)
    nrt = n_rows // tm
    return pl.pallas_call(
        functools.partial(_in_kernel, tm=tm, nrt=nrt, tn=tn),
        grid=(nrt,),
        in_specs=_halo_specs(tm, d, n_rows) + [_resident(gain), _resident(shift), _resident(w), _resident(cw)],
        out_specs=[pl.BlockSpec((tm, GATE_COLS), lambda i: (i, 0)),
                   pl.BlockSpec((tm, n - GATE_COLS), lambda i: (i, 0))],
        out_shape=[jax.ShapeDtypeStruct((n_rows, GATE_COLS), BF16),
                   jax.ShapeDtypeStruct((n_rows, n - GATE_COLS), F32)],
        scratch_shapes=[pltpu.VMEM((tm + 2 * HALO, d), BF16)],
        compiler_params=_cparams(("parallel",)),
        name="in_proj",
    )(x, x, x, gain, shift, w, cw)


def _ffn_kernel(x_ref, xp_ref, xn_ref, g_ref, s_ref, gt_ref, wu_ref, cw_ref, wd_ref, o_ref, h_scr, *,
                tm, nrt, tf):
    _fill_normed(h_scr, x_ref, xp_ref, xn_ref, g_ref[...], s_ref[...], pl.program_id(0), nrt, tm)
    h = h_scr[...]
    nj = D_FF // tf
    u, gated = {}, {}
    for t in range(nj + 2):
        if t < nj:
            ca = slice(t * tf, (t + 1) * tf)
            cb = slice(D_FF + t * tf, D_FF + (t + 1) * tf)
            u[t] = (jnp.dot(h, wu_ref[:, ca], preferred_element_type=F32),
                    jnp.dot(h, wu_ref[:, cb], preferred_element_type=F32))
        if 0 <= t - 1 < nj:
            j = t - 1
            ua, ub = u.pop(j)
            a = _conv3(ua, cw_ref[:, j * tf:(j + 1) * tf], tm)
            b = _conv3(ub, cw_ref[:, D_FF + j * tf:D_FF + (j + 1) * tf], tm)
            gated[j] = (a * _sigmoid(a) * b).astype(BF16)
        if 0 <= t - 2 < nj:
            j = t - 2
            down = jnp.dot(gated.pop(j), wd_ref[j * tf:(j + 1) * tf, :], preferred_element_type=F32)
            if j == 0:
                o_ref[...] = down
            else:
                o_ref[...] += down
    o_ref[...] = x_ref[...] + gt_ref[...] * o_ref[...]


def _ffn(x, gain, shift, gate, w_up, cw, w_down):
    n_rows, d = x.shape
    tm = _row_tile(n_rows, 1024)
    nrt = n_rows // tm
    return pl.pallas_call(
        functools.partial(_ffn_kernel, tm=tm, nrt=nrt, tf=256),
        grid=(nrt,),
        in_specs=_halo_specs(tm, d, n_rows) + [_resident(gain), _resident(shift), _resident(gate),
                                               _resident(w_up), _resident(cw), _resident(w_down)],
        out_specs=pl.BlockSpec((tm, d), lambda i: (i, 0)),
        out_shape=jax.ShapeDtypeStruct((n_rows, d), F32),
        scratch_shapes=[pltpu.VMEM((tm + 2 * HALO, d), BF16)],
        compiler_params=_cparams(("parallel",)),
        name="ffn",
    )(x, x, x, gain, shift, gate, w_up, cw, w_down)


def _seg_sum(x, ones_bd):
    return _mm(x, ones_bd, 2, 1)


def _two_stream_specs(tm, width, col_block, n_lat_tiles):
    return [pl.BlockSpec((tm, width), lambda i: (jnp.minimum(i, n_lat_tiles - 1), col_block)),
            pl.BlockSpec((tm, width), lambda i: (jnp.maximum(i - n_lat_tiles, 0), col_block))]


def _rw_prep_kernel(ul_ref, uc_ref, wlora_ref, alora_ref, gup_ref, ones_ref, vec_ref,
                    com_ref, lw_ref, dkb_ref, post_ref, *, n_lat_tiles):
    u = jnp.where(pl.program_id(0) < n_lat_tiles, ul_ref[...], uc_ref[...])
    w = RW_WIDTH
    r = u[:, 0:w]
    k = u[:, w:2 * w]
    v = u[:, 2 * w:3 * w]
    wd = u[:, 3 * w:3 * w + 2 * DECAY_LORA]
    ad = u[:, 3 * w + 2 * DECAY_LORA:3 * w + 2 * DECAY_LORA + 2 * ICLR_LORA]
    gd = u[:, 3 * w + 2 * DECAY_LORA + 2 * ICLR_LORA:]
    vec = vec_ref[...]
    k_k, k_a, r_k = vec[0:1], vec[1:2], vec[2:3]
    ones_bd = ones_ref[...]

    kk = k * k_k
    kk = kk * lax.rsqrt(_seg_sum(kk * kk, ones_bd) + 1e-12)
    g = _mm(_sigmoid(gd), gup_ref[...], 2, 2)
    wl = _mm(jnp.tanh(wd), wlora_ref[...], 2, 2)
    al = _mm(ad, alora_ref[...], 2, 2)
    com_ref[0] = r.astype(BF16)
    com_ref[1] = v.astype(BF16)
    com_ref[2] = (-kk).astype(BF16)
    kd_sum = None
    for d in range(2):
        w0 = vec[3 + d:4 + d]
        a0 = vec[5 + d:6 + d]
        lw = -math.exp(-0.5) * _sigmoid(w0 + wl[:, d * w:(d + 1) * w])
        a = _sigmoid(a0 + al[:, d * w:(d + 1) * w])
        kd = k * (1.0 + (a - 1.0) * k_a)
        lw_ref[d] = lw
        dkb_ref[d, 0] = kd.astype(BF16)
        dkb_ref[d, 1] = (kk * a).astype(BF16)
        kd_sum = kd if kd_sum is None else kd_sum + kd
    post_ref[0] = g.astype(BF16)
    post_ref[1] = (_seg_sum(r * kd_sum * r_k, ones_bd) * v).astype(BF16)


def _rw_prep(u_lat, u_ctx, col_block, wlora, alora, gup, ones_bd, vec):
    tm = math.gcd(STREAM_TILE, u_ctx.shape[0])
    n_lat_tiles = u_lat.shape[0] // tm
    n_rows = u_lat.shape[0] + u_ctx.shape[0]
    w = RW_WIDTH
    full = lambda a: pl.BlockSpec(a.shape, lambda i: (0,) * a.ndim)
    return pl.pallas_call(
        functools.partial(_rw_prep_kernel, n_lat_tiles=n_lat_tiles),
        grid=(n_rows // tm,),
        in_specs=_two_stream_specs(tm, RW_COLS, col_block, n_lat_tiles) + [
            full(wlora), full(alora), full(gup), full(ones_bd), full(vec)],
        out_specs=[pl.BlockSpec((3, tm, w), lambda i: (0, i, 0)),
                   pl.BlockSpec((2, tm, w), lambda i: (0, i, 0)),
                   pl.BlockSpec((2, 2, tm, w), lambda i: (0, 0, i, 0)),
                   pl.BlockSpec((2, tm, w), lambda i: (0, i, 0))],
        out_shape=[jax.ShapeDtypeStruct((3, n_rows, w), BF16),
                   jax.ShapeDtypeStruct((2, n_rows, w), F32),
                   jax.ShapeDtypeStruct((2, 2, n_rows, w), BF16),
                   jax.ShapeDtypeStruct((2, n_rows, w), BF16)],
        compiler_params=_cparams(("parallel",)),
        name="rw_prep",
    )(u_lat, u_ctx, wlora, alora, gup, ones_bd, vec)


def _smm(a, b, dims=NN):
    return _mm(a, b, 1, 1, dims)


def _scan_dir(com, lw, dkb, tri, slot, reverse):
    c = CHUNK
    r, v, a = (com[i].astype(F32) for i in range(3))
    kd, b = dkb[0].astype(F32), dkb[1].astype(F32)
    cs = _mm(tri, lw, 1, 3)
    tot = cs[0:1] if reverse else cs[c - 1:c]
    e_pos = jnp.exp(cs)
    e_neg = jnp.exp(-cs)
    e_end = jnp.exp(tot - cs)
    at = a * jnp.exp(cs - lw)
    rt = r * e_pos
    bt = b * e_neg
    kt = kd * e_neg
    bh = b * e_end
    kh = kd * e_end
    p_end = jnp.exp(tot)

    row = lax.broadcasted_iota(jnp.int32, (c, c), 0)
    col = lax.broadcasted_iota(jnp.int32, (c, c), 1)
    before = (col > row) if reverse else (col < row)
    upto = (col >= row) if reverse else (col <= row)
    eye = row == col
    same_head = (row // RW_HEAD_DIM) == (col // RW_HEAD_DIM)
    both = lambda m: jnp.concatenate([m, m], axis=1)
    lane = lax.broadcasted_iota(jnp.int32, (c, LANES), 1)
    heads = [(lane // RW_HEAD_DIM) == h for h in range(2)]

    def by_head(x):
        return jnp.concatenate([jnp.where(m, x, 0.0) for m in heads], axis=0).astype(BF16)

    pairs = []
    for p in range(RW_HEADS // 2):
        sl = slice(p * LANES, (p + 1) * LANES)
        pairs.append(dict(
            slot=slot, p=p, by_head=by_head,
            atrt=jnp.concatenate([at[:, sl], rt[:, sl]], axis=0).astype(BF16),
            bk=jnp.concatenate([by_head(bt[:, sl]), by_head(kt[:, sl])], axis=0),
            v2=by_head(v[:, sl]), v_t=v[:, sl].T,
            bkh=jnp.concatenate([bh[:, sl], kh[:, sl]], axis=0).astype(BF16),
            p_end=p_end[:, sl], masks=(both(before), both(upto), both(eye), same_head)))
    return pairs


def _scan_chunk(pairs, s_ref):
    c = CHUNK
    zero = jnp.zeros((c, c), BF16)
    for pr in pairs:
        before, upto, eye, _ = pr["masks"]
        gm = _smm(pr["atrt"], pr["bk"], NT)
        a_ab = jnp.where(before, gm[:c, :2 * c], 0.0)
        pr["a_ak"] = jnp.where(before, gm[:c, 2 * c:], 0.0).astype(BF16)
        pr["a_rb"] = jnp.where(upto, gm[c:, :2 * c], 0.0).astype(BF16)
        pr["a_rk"] = jnp.where(upto, gm[c:, 2 * c:], 0.0).astype(BF16)
        pr["t_inv"] = jnp.where(eye, 1.0, a_ab)
        pr["apow"] = a_ab.astype(BF16)

    def block_diag(m):
        return jnp.concatenate([jnp.concatenate([m[:, :c], zero], axis=1),
                                jnp.concatenate([zero, m[:, c:]], axis=1)], axis=0)

    n_sq = int(math.log2(c)) - 1
    for pr in pairs:
        pr["apow"] = _smm(pr["apow"], block_diag(pr["apow"])).astype(BF16)
    for k in range(n_sq):
        for pr in pairs:
            w_bd = block_diag(pr["apow"])
            if k + 1 < n_sq:
                res = _smm(jnp.concatenate([pr["apow"], pr["t_inv"].astype(BF16)], axis=0), w_bd)
                pr["apow"] = res[:c].astype(BF16)
                pr["t_inv"] = pr["t_inv"] + res[c:]
            else:
                pr["t_inv"] = pr["t_inv"] + _smm(pr["t_inv"], w_bd)
    for pr in pairs:
        s0 = s_ref[pr["slot"], pr["p"]]
        pr["s0"] = s0
        hr = _smm(pr["atrt"], s0, NT)
        pr["ah"], pr["rh"] = hr[:c], hr[c:]
    for pr in pairs:
        x = _smm(pr["a_ak"], pr["v2"])
        pr["u"] = _smm(pr["t_inv"], pr["by_head"](pr["ah"] + x))
    outs = {}
    for pr in pairs:
        pr["y"] = pr["rh"] + _smm(pr["a_rb"], pr["by_head"](pr["u"])) + _smm(pr["a_rk"], pr["v2"])
    for pr in pairs:
        same_head = pr["masks"][3]
        uv_t = jnp.concatenate([pr["u"].T, pr["v_t"]], axis=1)
        upd = _smm(uv_t, pr["bkh"])
        s_ref[pr["slot"], pr["p"]] = pr["s0"] * pr["p_end"] + jnp.where(same_head, upd, 0.0)
        outs[(pr["slot"], pr["p"])] = pr["y"]
    n_pairs = RW_HEADS // 2
    return [jnp.concatenate([outs[(d, p)] for p in range(n_pairs)], axis=1) for d in range(2)]


def _rw_scan_kernel(comf_ref, lwf_ref, dkbf_ref, comb_ref, lwb_ref, dkbb_ref, tri_ref, yf_ref, yb_ref, s_ref):
    @pl.when(pl.program_id(0) == 0)
    def _():
        s_ref[...] = jnp.zeros_like(s_ref)

    pairs = (_scan_dir(comf_ref[...], lwf_ref[0], dkbf_ref[0], tri_ref[0], 0, False)
             + _scan_dir(comb_ref[...], lwb_ref[0], dkbb_ref[0], tri_ref[1], 1, True))
    yf, yb = _scan_chunk(pairs, s_ref)
    yf_ref[...] = yf
    yb_ref[...] = yb


def _rw_scan(com, lw, dkb, tri, n_ctx_chunks):
    n_rows = com.shape[1]
    w = RW_WIDTH
    c = CHUNK
    assert c == LANES
    n_chunks = n_rows // c
    nc = n_ctx_chunks
    nl = n_chunks - nc

    def fwd_block(i):
        return jnp.where(i < nc, nl + i, i - nc)

    def bwd_block(i):
        return jnp.where(i < nc, nl + nc - 1 - i, nl - 1 - (i - nc))

    return pl.pallas_call(
        _rw_scan_kernel,
        grid=(n_chunks,),
        in_specs=[pl.BlockSpec((3, c, w), lambda i: (0, fwd_block(i), 0)),
                  pl.BlockSpec((1, c, w), lambda i: (0, fwd_block(i), 0)),
                  pl.BlockSpec((1, 2, c, w), lambda i: (0, 0, fwd_block(i), 0)),
                  pl.BlockSpec((3, c, w), lambda i: (0, bwd_block(i), 0)),
                  pl.BlockSpec((1, c, w), lambda i: (1, bwd_block(i), 0)),
                  pl.BlockSpec((1, 2, c, w), lambda i: (1, 0, bwd_block(i), 0)),
                  pl.BlockSpec((2, c, c), lambda i: (0, 0, 0))],
        out_specs=[pl.BlockSpec((c, w), lambda i: (fwd_block(i), 0)),
                   pl.BlockSpec((c, w), lambda i: (bwd_block(i), 0))],
        out_shape=[jax.ShapeDtypeStruct((n_rows, w), F32)] * 2,
        scratch_shapes=[pltpu.VMEM((2, RW_HEADS // 2, LANES, LANES), F32)],
        compiler_params=_cparams(("arbitrary",)),
        name="rw_scan",
    )(com, lw, dkb, com, lw, dkb, tri)


def _rw_post(y, g, bonus, avg, vec):
    mu = _mm(y, avg, 2, 1)
    yc = y - mu
    var = _mm(yc * yc, avg, 2, 1)
    yn = yc * lax.rsqrt(var + RW_LN_EPS)
    return (yn * vec[0:1] + vec[1:2] + bonus) * g


def _rms(x, n):
    return x * lax.rsqrt(jnp.sum(x * x, axis=-1, keepdims=True) * (1.0 / n) + NORM_EPS)


def _mla_prep_kernel(ul_ref, uc_ref, wq_ref, wk_ref, wv_ref, vq_ref, vkv_ref, gains_ref, ones_ref, rope_ref,
                     q_ref, k_ref, vt_ref, *, n_lat_tiles):
    u = jnp.where(pl.program_id(0) < n_lat_tiles, ul_ref[...], uc_ref[...])
    cq = _rms(u[:, :Q_LORA], Q_LORA) * vq_ref[...]
    q = jnp.dot(cq.astype(BF16), wq_ref[...], preferred_element_type=F32)
    ckv = _rms(u[:, Q_LORA:Q_LORA + KV_LORA], KV_LORA) * vkv_ref[...]
    ckv = ckv.astype(BF16)
    kn = jnp.dot(ckv, wk_ref[...], preferred_element_type=F32)
    vv = jnp.dot(ckv, wv_ref[...], preferred_element_type=F32)
    kr = pltpu.roll(u[:, Q_LORA + KV_LORA:], NOPE_DIM, 1)
    lane = lax.broadcasted_iota(jnp.int32, (1, HEAD_PAD), 1)
    ones_col = (lane == V_DIM).astype(F32)
    heads = lambda a: jnp.concatenate([a] * MLA_HEADS, axis=1)
    gains = gains_ref[...]
    gq, gk, shift_q, shift_k = (heads(gains[i:i + 1]) for i in range(4))
    tab = rope_ref[...]
    cos, sin_a, sin_b = heads(tab[0]), heads(tab[1]), heads(tab[2])
    width = MLA_HEADS * HEAD_PAD
    half = ROPE_DIM // 4

    def norm_rot(t, gain, shift):
        ms = _mm(t * t, ones_ref[...], 2, 1) * (1.0 / QK_DIM)
        t = t * lax.rsqrt(ms + NORM_EPS) * gain
        t = t * cos + pltpu.roll(t, width - half, 1) * sin_a + pltpu.roll(t, half, 1) * sin_b
        return (t + shift).astype(BF16)

    qo = norm_rot(q, gq, shift_q)
    ko = norm_rot(kn + heads(kr), gk, shift_k)
    for h in range(MLA_HEADS):
        sl = slice(h * HEAD_PAD, (h + 1) * HEAD_PAD)
        q_ref[h] = qo[:, sl]
        k_ref[h] = ko[:, sl]
        vt_ref[h] = (vv[:, sl] + ones_col).T[0:V_ROWS].astype(BF16)


def _mla_prep(u_lat, u_ctx, col_block, wq, wk, wv, vq, vkv, gains, head_ones, rope_tab):
    tm = math.gcd(STREAM_TILE, u_ctx.shape[0])
    n_lat_tiles = u_lat.shape[0] // tm
    n_rows = u_lat.shape[0] + u_ctx.shape[0]
    full = lambda a: pl.BlockSpec(a.shape, lambda i: (0,) * a.ndim)
    in_specs = _two_stream_specs(tm, MLA_PAD, col_block, n_lat_tiles) + [
        full(wq), full(wk), full(wv), full(vq), full(vkv), full(gains), full(head_ones),
        pl.BlockSpec((3, tm, HEAD_PAD), lambda i: (0, i, 0))]
    args = [u_lat, u_ctx, wq, wk, wv, vq, vkv, gains, head_ones, rope_tab]
    hs = pl.BlockSpec((MLA_HEADS, tm, HEAD_PAD), lambda i: (0, i, 0))
    shp = jax.ShapeDtypeStruct((MLA_HEADS, n_rows, HEAD_PAD), BF16)
    vs = pl.BlockSpec((MLA_HEADS, V_ROWS, tm), lambda i: (0, 0, i))
    vshp = jax.ShapeDtypeStruct((MLA_HEADS, V_ROWS, n_rows), BF16)
    return pl.pallas_call(
        functools.partial(_mla_prep_kernel, n_lat_tiles=n_lat_tiles),
        grid=(n_rows // tm,),
        in_specs=in_specs,
        out_specs=[hs, hs, vs],
        out_shape=[shp, shp, vshp],
        compiler_params=_cparams(("parallel",)),
        name="mla_prep",
    )(*args)


def _attn_kernel(q_ref, k_ref, vt_ref, o_ref, acc_scr, *rest, nk, sub, online):
    kv = pl.program_id(2)
    if online:
        m_scr, = rest

    @pl.when(kv == 0)
    def _():
        acc_scr[...] = jnp.zeros_like(acc_scr)
        if online:
            m_scr[...] = jnp.full_like(m_scr, -jnp.inf)

    k = k_ref[0]
    vt = vt_ref[0]
    n = q_ref.shape[1] // sub
    st, p, alpha = {}, {}, {}
    for t in range(n + 2):
        if t < n:
            st[t] = lax.dot_general(k, q_ref[0, t * sub:(t + 1) * sub], NT, preferred_element_type=F32)
        if 0 <= t - 1 < n:
            r = t - 1
            cols = slice(r * sub, (r + 1) * sub)
            s_r = st.pop(r)
            if online:
                m_prev = m_scr[:, cols]
                m_new = jnp.maximum(m_prev, jnp.max(s_r, axis=0, keepdims=True))
                alpha[r] = jnp.exp2(m_prev - m_new)
                m_scr[:, cols] = m_new
                s_r = s_r - m_new
            p[r] = jnp.exp2(s_r).astype(BF16)
        if 0 <= t - 2 < n:
            r = t - 2
            cols = slice(r * sub, (r + 1) * sub)
            upd = jnp.dot(vt, p.pop(r), preferred_element_type=F32)
            prev = acc_scr[0:V_ROWS, cols]
            acc_scr[0:V_ROWS, cols] = (alpha.pop(r) * prev if online else prev) + upd

    @pl.when(kv == nk - 1)
    def _():
        acc = acc_scr[...]
        o_ref[...] = (acc / acc[V_DIM:V_DIM + 1]).T.astype(BF16)


def _attention_call(q, k, vt, online, q_rows, k_rows):
    h, _, dp = q.shape
    lq = q_rows[1] - q_rows[0]
    lk = k_rows[1] - k_rows[0]
    tq = _row_tile(lq, ATT_TQ)
    sub = min(tq, ATT_SUB)
    tk = max(t for t in range(LANES, min(lk, ATT_TK) + 1, LANES) if lk % t == 0 and k_rows[0] % t == 0)
    nk = lk // tk
    assert q_rows[0] % tq == 0
    qo = q_rows[0] // tq
    ko = k_rows[0] // tk
    scratch = [pltpu.VMEM((HEAD_PAD, tq), F32)]
    if online:
        scratch.append(pltpu.VMEM((1, tq), F32))
    return pl.pallas_call(
        functools.partial(_attn_kernel, nk=nk, sub=sub, online=online),
        grid=(h, lq // tq, nk),
        in_specs=[pl.BlockSpec((1, tq, dp), lambda hh, i, j: (hh, i + qo, 0)),
                  pl.BlockSpec((1, tk, dp), lambda hh, i, j: (hh, j + ko, 0)),
                  pl.BlockSpec((1, V_ROWS, tk), lambda hh, i, j: (hh, 0, j + ko))],
        out_specs=pl.BlockSpec((tq, dp), lambda hh, i, j: (i, hh)),
        out_shape=jax.ShapeDtypeStruct((lq, h * dp), BF16),
        scratch_shapes=scratch,
        compiler_params=_cparams(("parallel", "parallel", "arbitrary")),
        name="attn_online" if online else "attn",
    )(q, k, vt)


def _attention(q, k, vt, score_bound, q_rows, k_rows):
    return lax.cond(2.0 * score_bound <= MAX_SCORE_SPAN,
                    lambda: _attention_call(q, k, vt, False, q_rows, k_rows),
                    lambda: _attention_call(q, k, vt, True, q_rows, k_rows))


def _ft_chan_kernel(u_ref, m_ref, o_ref):
    o_ref[...] = _mm(u_ref[...], m_ref[...], 2, 2)


def _ft_chan(u, col_block, mat):
    n_rows = u.shape[0]
    tm = _row_tile(n_rows, 512)
    return pl.pallas_call(
        _ft_chan_kernel,
        grid=(n_rows // tm,),
        in_specs=[pl.BlockSpec((tm, FT_WIDTH), lambda i: (i, col_block)),
                  pl.BlockSpec(mat.shape, lambda i: (0, 0))],
        out_specs=pl.BlockSpec((tm, 2 * FT_WIDTH), lambda i: (i, 0)),
        out_shape=jax.ShapeDtypeStruct((n_rows, 2 * FT_WIDTH), F32),
        compiler_params=_cparams(("parallel",)),
        name="ft_chan",
    )(u, mat)


def _ft_s1_kernel(w_ref, cs_ref, tw_ref, o_ref, *, l1, g):
    z = _mm(cs_ref[...], w_ref[...], 2, 2)
    fw = FT_WIDTH
    for t in range(g):
        zr = z[:, t * 2 * fw:t * 2 * fw + fw]
        zi = z[:, t * 2 * fw + fw:(t + 1) * 2 * fw]
        a_re = zr[:l1] + zi[l1:]
        a_im = zi[:l1] - zr[l1:]
        tc = jnp.concatenate([tw_ref[0, t]] * (fw // LANES), axis=1)
        ts = jnp.concatenate([tw_ref[1, t]] * (fw // LANES), axis=1)
        o_ref[:, t * 2 * fw:t * 2 * fw + fw] = a_re * tc + a_im * ts
        o_ref[:, t * 2 * fw + fw:(t + 1) * 2 * fw] = a_im * tc - a_re * ts


def _ft_s1(w2d, cs, tw, l1, l2):
    g = min(8, l2)
    blk = g * 2 * FT_WIDTH
    return pl.pallas_call(
        functools.partial(_ft_s1_kernel, l1=l1, g=g),
        grid=(l2 // g,),
        in_specs=[pl.BlockSpec((l1, blk), lambda i: (0, i)),
                  pl.BlockSpec(cs.shape, lambda i: (0, 0)),
                  pl.BlockSpec((2, g, l1, LANES), lambda i: (0, i, 0, 0))],
        out_specs=pl.BlockSpec((l1, blk), lambda i: (0, i)),
        out_shape=jax.ShapeDtypeStruct((l1, l2 * 2 * FT_WIDTH), F32),
        compiler_params=_cparams(("parallel",)),
        name="ft_s1",
    )(w2d, cs, tw)


def _ft_s2_kernel(b_ref, cs_ref, o_ref, *, g, scale):
    fw = FT_WIDTH
    cmat = cs_ref[0]
    smat = cs_ref[1]
    for kk in range(g):
        blk = b_ref[kk]
        out = _mm(cmat, blk[:, :fw], 2, 2) + _mm(smat, blk[:, fw:], 2, 2)
        o_ref[:, kk * fw:(kk + 1) * fw] = out * scale


def _ft_s2(b3d, cs2, l1, l2, scale):
    g = min(8, l1)
    return pl.pallas_call(
        functools.partial(_ft_s2_kernel, g=g, scale=scale),
        grid=(l1 // g,),
        in_specs=[pl.BlockSpec((g, l2, 2 * FT_WIDTH), lambda i: (i, 0, 0)),
                  pl.BlockSpec(cs2.shape, lambda i: (0, 0, 0))],
        out_specs=pl.BlockSpec((l2, g * FT_WIDTH), lambda i: (0, i)),
        out_shape=jax.ShapeDtypeStruct((l2, l1 * FT_WIDTH), F32),
        compiler_params=_cparams(("parallel",)),
        name="ft_s2",
    )(b3d, cs2)


def _ft_tables(n):
    l1 = 1 << (int(math.log2(n)) // 2)
    l2 = n // l1
    assert l1 * l2 == n and l1 % 8 == 0 and l2 % 8 == 0
    k1 = np.arange(l1)
    ang1 = 2.0 * np.pi * np.outer(k1, np.arange(l1)) / l1
    cs1 = np.concatenate([np.cos(ang1), np.sin(ang1)], axis=0)
    angt = 2.0 * np.pi * np.outer(np.arange(l2), k1) / n
    tw = np.stack([np.cos(angt), np.sin(angt)], axis=0)
    tw = np.broadcast_to(tw[..., None], (2, l2, l1, LANES))
    ang2 = 2.0 * np.pi * np.outer(np.arange(l2), np.arange(l2)) / l2
    cs2 = np.stack([np.cos(ang2), np.sin(ang2)], axis=0)
    f = lambda a: jnp.asarray(np.ascontiguousarray(a), F32)
    return l1, l2, f(cs1), f(tw), f(cs2)


def _chan_dft_matrix():
    n = FT_GROUP_DIM
    ang = 2.0 * np.pi * np.outer(np.arange(n), np.arange(n)) / n
    eye = np.eye(FT_GROUPS)
    mat = np.concatenate([np.kron(eye, np.cos(ang)), -np.kron(eye, np.sin(ang))], axis=1)
    return jnp.asarray(mat, F32)


def _fourier_mix(u, col_block, chan_mat):
    n = u.shape[0]
    l1, l2, cs1, tw, cs2 = _ft_tables(n)
    wc = _ft_chan(u, col_block, chan_mat)
    b = _ft_s1(wc.reshape(l1, l2 * 2 * FT_WIDTH), cs1, tw, l1, l2)
    scale = 1.0 / math.sqrt(n * FT_GROUP_DIM)
    out = _ft_s2(b.reshape(l1, l2, 2 * FT_WIDTH), cs2, l1, l2, scale)
    return out.reshape(n, FT_WIDTH)


def _merge_kernel(yf_ref, ybw_ref, post_ref, avg_ref, ln_ref, yb_ref, yc_ref, ga_ref, gb_ref, gc_ref,
                  x_ref, gt_ref, wa_ref, wb_ref, wc_ref, wo_ref, o_ref):
    def branch(y, w_ref, g_ref):
        t = jnp.dot(y.astype(BF16), w_ref[...], preferred_element_type=F32)
        return g_ref[...].astype(F32) * t

    ya = _rw_post(yf_ref[...] + ybw_ref[...], post_ref[0].astype(F32), post_ref[1].astype(F32),
                  avg_ref[...], ln_ref[...])
    m = branch(ya, wa_ref, ga_ref) + branch(yb_ref[...], wb_ref, gb_ref) + branch(yc_ref[...], wc_ref, gc_ref)
    out = jnp.dot(m.astype(BF16), wo_ref[...], preferred_element_type=F32)
    o_ref[...] = x_ref[...] + gt_ref[...] * out


def _merge(yf, ybw, post, row_off, avg_bd, ln_vec, yb, yc, gates, x, gt, wa, wb, wc, wo):
    n_rows, d = x.shape
    tm = _row_tile(n_rows, 512)
    assert row_off % tm == 0
    off = row_off // tm
    w = RW_WIDTH
    full = lambda a: pl.BlockSpec(a.shape, lambda i: (0,) * a.ndim)
    rows = lambda a: pl.BlockSpec((tm, a.shape[1]), lambda i: (i, 0))
    gate = lambda b: pl.BlockSpec((tm, d), lambda i: (i, b))
    scan = pl.BlockSpec((tm, w), lambda i: (i + off, 0))
    return pl.pallas_call(
        _merge_kernel,
        grid=(n_rows // tm,),
        in_specs=[scan, scan, pl.BlockSpec((2, tm, w), lambda i: (0, i + off, 0)), full(avg_bd), full(ln_vec),
                  rows(yb), rows(yc), gate(0), gate(1), gate(2), rows(x), full(gt),
                  full(wa), full(wb), full(wc), full(wo)],
        out_specs=pl.BlockSpec((tm, d), lambda i: (i, 0)),
        out_shape=jax.ShapeDtypeStruct((n_rows, d), F32),
        compiler_params=_cparams(("parallel",)),
        name="merge",
    )(yf, ybw, post, avg_bd, ln_vec, yb, yc, gates, gates, gates, x, gt, wa, wb, wc, wo)


def _stream_mods(mod):
    return [mod[i * D_MODEL:(i + 1) * D_MODEL][None] for i in range(N_MOD)]


def _pad_heads(w, src_cols, width):
    k = w.shape[0]
    wh = w.reshape(k, MLA_HEADS, width)[:, :, src_cols]
    wh = jnp.pad(wh, ((0, 0), (0, 0), (0, HEAD_PAD - wh.shape[2])))
    return wh.reshape(k, MLA_HEADS * HEAD_PAD)


def _block_diag2(a, b):
    z01 = jnp.zeros((a.shape[0], b.shape[1]), a.dtype)
    z10 = jnp.zeros((b.shape[0], a.shape[1]), a.dtype)
    return jnp.concatenate([jnp.concatenate([a, z01], axis=1), jnp.concatenate([z10, b], axis=1)], axis=0)


def _rope_tables(n, n_ctx):
    half = ROPE_DIM // 2
    inv = ROPE_BASE ** (-np.arange(0, half, 2, dtype=np.float32) / half)
    t = np.arange(n)
    row = (t // GRID_W).astype(np.float32)
    col = (t % GRID_W).astype(np.float32)
    q = half // 2
    cos = np.ones((n, HEAD_PAD), np.float32)
    sin_a = np.zeros((n, HEAD_PAD), np.float32)
    sin_b = np.zeros((n, HEAD_PAD), np.float32)
    for base, pos in ((NOPE_DIM, row), (NOPE_DIM + half, col)):
        ang = pos[:, None] * inv[None, :]
        c, s = np.cos(ang), np.sin(ang)
        cos[:, base:base + q] = c
        cos[:, base + q:base + 2 * q] = c
        sin_a[:, base:base + q] = -s
        sin_b[:, base + q:base + 2 * q] = s
    tab = np.stack([cos, sin_a, sin_b], axis=0)
    ident = np.stack([np.ones((n_ctx, HEAD_PAD), np.float32)] + [np.zeros((n_ctx, HEAD_PAD), np.float32)] * 2)
    return jnp.asarray(np.concatenate([tab, ident], axis=1))


def _layer_weights(l, w):
    w_in = w["w_in"][l]
    o1 = RW_COLS
    o2 = o1 + MLA_COLS
    o3 = o2 + FT_WIDTH
    w_in_p = jnp.concatenate(
        [w_in[:, o3:], w_in[:, :o1], w_in[:, o1:o2],
         jnp.zeros((D_MODEL, MLA_PAD - MLA_COLS), F32), w_in[:, o2:o3]], axis=1).astype(BF16)
    pad_gain = lambda g: jnp.pad(g, (0, HEAD_PAD - QK_DIM))
    wb = w["w_branch_b"][l].reshape(MLA_HEADS, V_DIM, D_MODEL)
    wb = jnp.pad(wb, ((0, 0), (0, HEAD_PAD - V_DIM), (0, 0))).reshape(MLA_HEADS * HEAD_PAD, D_MODEL)
    q_scale = math.log2(math.e) * QK_DIM ** -0.5
    score_bound = (BOUND_MARGIN * QK_DIM * q_scale * jnp.max(jnp.abs(w["q_gain"][l]))
                   * jnp.max(jnp.abs(w["k_gain"][l])))
    shift_lane = (jnp.arange(HEAD_PAD) == QK_DIM).astype(F32)
    return dict(
        w_in=w_in_p, cw_in=w["rw_conv"][l],
        wlora=_block_diag2(w["rw_w_up"][l, 0], w["rw_w_up"][l, 1]),
        alora=_block_diag2(w["rw_a_up"][l, 0], w["rw_a_up"][l, 1]),
        gup=w["rw_g_up"][l],
        rw_vec=jnp.stack([w["rw_k_k"][l], w["rw_k_a"][l], w["rw_r_k"][l].reshape(-1),
                          w["rw_w0"][l, 0], w["rw_w0"][l, 1], w["rw_a0"][l, 0], w["rw_a0"][l, 1],
                          jnp.zeros((RW_WIDTH,), F32)], axis=0),
        ln_vec=jnp.stack([w["rw_ln_w"][l], w["rw_ln_b"][l]], axis=0),
        wq=_pad_heads(w["w_uq"][l], slice(0, QK_DIM), QK_DIM).astype(BF16),
        wk=_pad_heads(w["w_ukv"][l], slice(0, NOPE_DIM), NOPE_DIM + V_DIM).astype(BF16),
        wv=_pad_heads(w["w_ukv"][l], slice(NOPE_DIM, NOPE_DIM + V_DIM), NOPE_DIM + V_DIM).astype(BF16),
        vq=w["mla_q_norm"][l][None], vkv=w["mla_kv_norm"][l][None],
        gains=jnp.stack([pad_gain(w["q_gain"][l]) * q_scale, pad_gain(w["k_gain"][l]),
                             -score_bound * shift_lane, shift_lane], axis=0),
        score_bound=score_bound,
        wa=w["w_branch_a"][l].astype(BF16), wb=wb.astype(BF16), wc=w["w_branch_c"][l].astype(BF16),
        wo=w["w_out"][l].astype(BF16),
        w_up=w["w_up"][l].astype(BF16), cw_ffn=w["ffn_conv"][l], w_down=w["w_down"][l].astype(BF16),
    )


def _forward(x, c, ctx, c_ctx, w):
    depth = w["w_ada"].shape[0]
    x_lat = x[0]
    x_ctx = ctx[0]
    n_lat = x_lat.shape[0]
    n_ctx = x_ctx.shape[0]
    assert n_lat % CHUNK == 0 and n_ctx % CHUNK == 0 and n_lat % GRID_W == 0
    n_all = n_lat + n_ctx

    cc = jnp.zeros((8, D_MODEL), F32).at[0].set(c[0]).at[1].set(c_ctx)
    mods = _mods(cc, w["w_ada"], w["b_ada"])

    head = np.arange(RW_WIDTH) // RW_HEAD_DIM
    ones_bd = jnp.asarray((head[:, None] == head[None, :]).astype(np.float32), BF16)
    avg_bd = ones_bd * (1.0 / RW_HEAD_DIM)
    mla_head = np.arange(MLA_HEADS * HEAD_PAD) // HEAD_PAD
    head_ones = jnp.asarray((mla_head[:, None] == mla_head[None, :]).astype(np.float32), BF16)
    ti = np.arange(CHUNK)
    tri = jnp.asarray(np.stack([ti[None, :] <= ti[:, None], ti[None, :] >= ti[:, None]]).astype(np.float32),
                      BF16)
    chan_mat = _chan_dft_matrix()
    rope_tab = _rope_tables(n_lat, n_ctx)

    for l in range(depth):
        last = l == depth - 1
        p = _layer_weights(l, w)
        streams = [(x_lat, _stream_mods(mods[l, 0]), (0, n_lat)),
                   (x_ctx, _stream_mods(mods[l, 1]), (n_lat, n_all))]
        us, gates = [], []
        for xs, (sh1, sc1, _, _, _, _), _ in streams:
            gain1 = w["g_norm1"][l][None] * (1.0 + sc1)
            g_s, u_s = _in_proj(xs, gain1, sh1, p["w_in"], p["cw_in"])
            gates.append(g_s)
            us.append(u_s)
        com, lw, dkb, post = _rw_prep(us[0], us[1], U_RW // RW_COLS, p["wlora"], p["alora"], p["gup"],
                                   ones_bd, p["rw_vec"])
        q, k, vt = _mla_prep(us[0], us[1], U_MLA // MLA_PAD, p["wq"], p["wk"], p["wv"], p["vq"], p["vkv"],
                             p["gains"], head_ones, rope_tab)
        yf, yb = _rw_scan(com, lw, dkb, tri, n_ctx // CHUNK)

        new = []
        for si, (xs, (_, _, gt1, sh2, sc2, gt2), rows) in enumerate(streams):
            if si == 1 and last:
                continue
            y_att = _attention(q, k, vt, p["score_bound"], rows, (0, n_all) if si == 0 else rows)
            yc = _fourier_mix(us[si], U_FT // FT_WIDTH, chan_mat)
            x1 = _merge(yf, yb, post, rows[0], avg_bd, p["ln_vec"], y_att, yc, gates[si], xs, gt1,
                        p["wa"], p["wb"], p["wc"], p["wo"])
            gain2 = w["g_norm2"][l][None] * (1.0 + sc2)
            new.append(_ffn(x1, gain2, sh2, gt2, p["w_up"], p["cw_ffn"], p["w_down"]))
        x_lat = new[0]
        if not last:
            x_ctx = new[1]
    return x_lat[None]


def kernel(x, c, ctx, c_ctx, w_ada, b_ada, g_norm1, g_norm2, w_in, rw_conv, rw_w0, rw_w_up, rw_a0, rw_a_up,
           rw_g_up, rw_k_k, rw_k_a, rw_r_k, rw_ln_w, rw_ln_b, mla_q_norm, w_uq, mla_kv_norm, w_ukv, q_gain,
           k_gain, w_branch_a, w_branch_b, w_branch_c, w_out, ffn_conv, w_up, w_down):
    w = dict(w_ada=w_ada, b_ada=b_ada, g_norm1=g_norm1, g_norm2=g_norm2, w_in=w_in, rw_conv=rw_conv,
             rw_w0=rw_w0, rw_w_up=rw_w_up, rw_a0=rw_a0, rw_a_up=rw_a_up, rw_g_up=rw_g_up, rw_k_k=rw_k_k,
             rw_k_a=rw_k_a, rw_r_k=rw_r_k, rw_ln_w=rw_ln_w, rw_ln_b=rw_ln_b, mla_q_norm=mla_q_norm,
             w_uq=w_uq, mla_kv_norm=mla_kv_norm, w_ukv=w_ukv, q_gain=q_gain, k_gain=k_gain,
             w_branch_a=w_branch_a, w_branch_b=w_branch_b, w_branch_c=w_branch_c, w_out=w_out,
             ffn_conv=ffn_conv, w_up=w_up, w_down=w_down)
    return _forward(x, c, ctx, c_ctx, w)
```

```python
import functools
import math

import numpy as np
import jax
import jax.numpy as jnp
from jax import lax
from jax.experimental import pallas as pl
from jax.experimental.pallas import tpu as pltpu

F32 = jnp.float32
BF16 = jnp.bfloat16

D_MODEL = 1024
GRID_W = 64
NORM_EPS = 1e-6
RW_HEADS = 6
RW_HEAD_DIM = 64
RW_WIDTH = RW_HEADS * RW_HEAD_DIM
DECAY_LORA = 64
ICLR_LORA = 64
GATE_LORA = 128
RW_LN_EPS = 64e-5
MLA_HEADS = 6
Q_LORA = 384
KV_LORA = 256
NOPE_DIM = 64
ROPE_DIM = 32
V_DIM = 64
QK_DIM = NOPE_DIM + ROPE_DIM
ROPE_BASE = 10000.0
FT_GROUPS = 4
FT_GROUP_DIM = 64
FT_WIDTH = FT_GROUPS * FT_GROUP_DIM
D_FF = 2816
N_MOD = 6
RW_COLS = 3 * RW_WIDTH + 2 * DECAY_LORA + 2 * ICLR_LORA + GATE_LORA
MLA_COLS = Q_LORA + KV_LORA + ROPE_DIM

LANES = 128
HEAD_PAD = 128
MLA_PAD = 768
GATE_COLS = 3 * D_MODEL
RW_OFF = GATE_COLS
U_RW = 0
U_MLA = U_RW + RW_COLS
U_FT = U_MLA + MLA_PAD
HALO = 16
STREAM_TILE = 256
CHUNK = 128
VMEM_LIMIT = 48 * 1024 * 1024
V_ROWS = 80
BOUND_MARGIN = 1.02
MAX_SCORE_SPAN = 100.0
ATT_TQ, ATT_SUB, ATT_TK = 2048, 256, 3328

NN = (((1,), (0,)), ((), ()))
NT = (((1,), (1,)), ((), ()))


def _pieces(x, n):
    if x.dtype == BF16:
        return [x]
    out = []
    rem = x
    for i in range(n):
        p = rem.astype(BF16)
        out.append(p)
        if i + 1 < n:
            rem = rem - p.astype(F32)
    return out


def _mm(a, b, pa=1, pb=1, dims=NN):
    ap = _pieces(a, pa)
    bp = _pieces(b, pb)
    n = max(len(ap), len(bp))
    acc = None
    for i in reversed(range(len(ap))):
        for j in reversed(range(len(bp))):
            if i + j < n:
                t = lax.dot_general(ap[i], bp[j], dims, preferred_element_type=F32)
                acc = t if acc is None else acc + t
    return acc


def _sigmoid(x):
    return 0.5 * jnp.tanh(0.5 * x) + 0.5


def _cparams(sem):
    return pltpu.CompilerParams(dimension_semantics=sem, vmem_limit_bytes=VMEM_LIMIT)


def _row_tile(n, pref):
    t = min(n, pref)
    while n % t:
        t //= 2
    return t


def _mods_kernel(cc_ref, w_ref, b_ref, o_ref):
    cc = cc_ref[...]
    s = cc * _sigmoid(cc)
    o_ref[0] = _mm(s, w_ref[0], 2, 2) + b_ref[0]


def _mods(cc, w_ada, b_ada):
    depth, d, n = w_ada.shape
    tn = 1536
    return pl.pallas_call(
        _mods_kernel,
        grid=(depth, n // tn),
        in_specs=[pl.BlockSpec((8, d), lambda l, j: (0, 0)),
                  pl.BlockSpec((1, d, tn), lambda l, j: (l, 0, j)),
                  pl.BlockSpec((1, 1, tn), lambda l, j: (l, 0, j))],
        out_specs=pl.BlockSpec((1, 8, tn), lambda l, j: (l, 0, j)),
        out_shape=jax.ShapeDtypeStruct((depth, 8, n), F32),
        compiler_params=_cparams(("parallel", "parallel")),
        name="mods",
    )(cc, w_ada, b_ada.reshape(depth, 1, n))


def _modnorm(xv, g, s):
    ms = jnp.mean(xv * xv, axis=-1, keepdims=True)
    return (xv * lax.rsqrt(ms + NORM_EPS)) * g + s


def _fill_normed(h_scr, x_ref, xp_ref, xn_ref, g, s, i, nrt, tm):
    h_scr[HALO:HALO + tm] = _modnorm(x_ref[...], g, s).astype(BF16)
    hp = jnp.where(i > 0, _modnorm(xp_ref[...], g, s), 0.0)
    h_scr[0:HALO] = hp.astype(BF16)
    hn = jnp.where(i < nrt - 1, _modnorm(xn_ref[...], g, s), 0.0)
    h_scr[HALO + tm:2 * HALO + tm] = hn.astype(BF16)


def _conv3(u, cw, tm):
    rows = u.shape[0]
    up = pltpu.roll(u, 1, 0)
    un = pltpu.roll(u, rows - 1, 0)
    out = cw[0:1] * up + cw[1:2] * u + cw[2:3] * un
    return out[HALO:HALO + tm]


def _halo_specs(tm, d, n_rows):
    r = tm // HALO
    last = n_rows // HALO - 1
    return [pl.BlockSpec((tm, d), lambda i: (i, 0)),
            pl.BlockSpec((HALO, d), lambda i: (jnp.maximum(i * r - 1, 0), 0)),
            pl.BlockSpec((HALO, d), lambda i: (jnp.minimum((i + 1) * r, last), 0))]


def _resident(a):
    return pl.BlockSpec(a.shape, lambda i: (0,) * a.ndim, pipeline_mode=pl.Buffered(1))


def _in_kernel(x_ref, xp_ref, xn_ref, g_ref, s_ref, w_ref, cw_ref, gate_ref, o_ref, h_scr, *, tm, nrt, tn):
    _fill_normed(h_scr, x_ref, xp_ref, xn_ref, g_ref[...], s_ref[...], pl.program_id(0), nrt, tm)
    nj = w_ref.shape[1] // tn
    u = {}
    for t in range(nj + 1):
        if t < nj:
            conv = RW_OFF <= t * tn < RW_OFF + RW_COLS
            h = h_scr[...] if conv else h_scr[HALO:HALO + tm]
            u[t] = jnp.dot(h, w_ref[:, t * tn:(t + 1) * tn], preferred_element_type=F32)
        if t >= 1:
            lo = (t - 1) * tn
            val = u.pop(t - 1)
            if lo < GATE_COLS:
                gate_ref[:, lo:lo + tn] = _sigmoid(val).astype(BF16)
            elif lo < RW_OFF + RW_COLS:
                o_ref[:, lo - GATE_COLS:lo - GATE_COLS + tn] = _conv3(val, cw_ref[:, lo - RW_OFF:lo - RW_OFF + tn], tm)
            else:
                o_ref[:, lo - GATE_COLS:lo - GATE_COLS + tn] = val


def _in_proj(x, gain, shift, w, cw):
    n_rows, d = x.shape
    n = w.shape[1]
    tn = 512
    assert GATE_COLS % tn == 0 and RW_COLS % tn == 0 and n % tn == 0
    tm = _row_tile(n_rows, 512)
    nrt = n_rows // tm
    return pl.pallas_call(
        functools.partial(_in_kernel, tm=tm, nrt=nrt, tn=tn),
        grid=(nrt,),
        in_specs=_halo_specs(tm, d, n_rows) + [_resident(gain), _resident(shift), _resident(w), _resident(cw)],
        out_specs=[pl.BlockSpec((tm, GATE_COLS), lambda i: (i, 0)),
                   pl.BlockSpec((tm, n - GATE_COLS), lambda i: (i, 0))],
        out_shape=[jax.ShapeDtypeStruct((n_rows, GATE_COLS), BF16),
                   jax.ShapeDtypeStruct((n_rows, n - GATE_COLS), F32)],
        scratch_shapes=[pltpu.VMEM((tm + 2 * HALO, d), BF16)],
        compiler_params=_cparams(("parallel",)),
        name="in_proj",
    )(x, x, x, gain, shift, w, cw)


def _ffn_kernel(x_ref, xp_ref, xn_ref, g_ref, s_ref, gt_ref, wu_ref, cw_ref, wd_ref, o_ref, h_scr, *,
                tm, nrt, tf):
    _fill_normed(h_scr, x_ref, xp_ref, xn_ref, g_ref[...], s_ref[...], pl.program_id(0), nrt, tm)
    h = h_scr[...]
    nj = D_FF // tf
    u, gated = {}, {}
    for t in range(nj + 2):
        if t < nj:
            ca = slice(t * tf, (t + 1) * tf)
            cb = slice(D_FF + t * tf, D_FF + (t + 1) * tf)
            u[t] = (jnp.dot(h, wu_ref[:, ca], preferred_element_type=F32),
                    jnp.dot(h, wu_ref[:, cb], preferred_element_type=F32))
        if 0 <= t - 1 < nj:
            j = t - 1
            ua, ub = u.pop(j)
            a = _conv3(ua, cw_ref[:, j * tf:(j + 1) * tf], tm)
            b = _conv3(ub, cw_ref[:, D_FF + j * tf:D_FF + (j + 1) * tf], tm)
            gated[j] = (a * _sigmoid(a) * b).astype(BF16)
        if 0 <= t - 2 < nj:
            j = t - 2
            down = jnp.dot(gated.pop(j), wd_ref[j * tf:(j + 1) * tf, :], preferred_element_type=F32)
            if j == 0:
                o_ref[...] = down
            else:
                o_ref[...] += down
    o_ref[...] = x_ref[...] + gt_ref[...] * o_ref[...]


def _ffn(x, gain, shift, gate, w_up, cw, w_down):
    n_rows, d = x.shape
    tm = _row_tile(n_rows, 1024)
    nrt = n_rows // tm
    return pl.pallas_call(
        functools.partial(_ffn_kernel, tm=tm, nrt=nrt, tf=256),
        grid=(nrt,),
        in_specs=_halo_specs(tm, d, n_rows) + [_resident(gain), _resident(shift), _resident(gate),
                                               _resident(w_up), _resident(cw), _resident(w_down)],
        out_specs=pl.BlockSpec((tm, d), lambda i: (i, 0)),
        out_shape=jax.ShapeDtypeStruct((n_rows, d), F32),
        scratch_shapes=[pltpu.VMEM((tm + 2 * HALO, d), BF16)],
        compiler_params=_cparams(("parallel",)),
        name="ffn",
    )(x, x, x, gain, shift, gate, w_up, cw, w_down)


def _seg_sum(x, ones_bd):
    return _mm(x, ones_bd, 2, 1)


def _two_stream_specs(tm, width, col_block, n_lat_tiles):
    return [pl.BlockSpec((tm, width), lambda i: (jnp.minimum(i, n_lat_tiles - 1), col_block)),
            pl.BlockSpec((tm, width), lambda i: (jnp.maximum(i - n_lat_tiles, 0), col_block))]


def _rw_prep_body(u, wlora_ref, alora_ref, gup_ref, ones_ref, vec_ref, com_ref, lw_ref, dkb_ref, post_ref):
    w = RW_WIDTH
    r = u[:, 0:w]
    k = u[:, w:2 * w]
    v = u[:, 2 * w:3 * w]
    wd = u[:, 3 * w:3 * w + 2 * DECAY_LORA]
    ad = u[:, 3 * w + 2 * DECAY_LORA:3 * w + 2 * DECAY_LORA + 2 * ICLR_LORA]
    gd = u[:, 3 * w + 2 * DECAY_LORA + 2 * ICLR_LORA:]
    vec = vec_ref[...]
    k_k, k_a, r_k = vec[0:1], vec[1:2], vec[2:3]
    ones_bd = ones_ref[...]

    kk = k * k_k
    kk = kk * lax.rsqrt(_seg_sum(kk * kk, ones_bd) + 1e-12)
    g = _mm(_sigmoid(gd), gup_ref[...], 2, 2)
    wl = _mm(jnp.tanh(wd), wlora_ref[...], 2, 2)
    al = _mm(ad, alora_ref[...], 2, 2)
    com_ref[0] = r.astype(BF16)
    com_ref[1] = v.astype(BF16)
    com_ref[2] = (-kk).astype(BF16)
    kd_sum = None
    for d in range(2):
        w0 = vec[3 + d:4 + d]
        a0 = vec[5 + d:6 + d]
        lw = -math.exp(-0.5) * _sigmoid(w0 + wl[:, d * w:(d + 1) * w])
        a = _sigmoid(a0 + al[:, d * w:(d + 1) * w])
        kd = k * (1.0 + (a - 1.0) * k_a)
        lw_ref[d] = lw
        dkb_ref[d, 0] = kd.astype(BF16)
        dkb_ref[d, 1] = (kk * a).astype(BF16)
        kd_sum = kd if kd_sum is None else kd_sum + kd
    post_ref[0] = g.astype(BF16)
    post_ref[1] = (_seg_sum(r * kd_sum * r_k, ones_bd) * v).astype(BF16)


def _smm(a, b, dims=NN):
    return _mm(a, b, 1, 1, dims)


def _scan_dir(com, lw, dkb, tri, slot, reverse):
    c = CHUNK
    r, v, a = (com[i].astype(F32) for i in range(3))
    kd, b = dkb[0].astype(F32), dkb[1].astype(F32)
    cs = _mm(tri, lw, 1, 3)
    tot = cs[0:1] if reverse else cs[c - 1:c]
    e_pos = jnp.exp(cs)
    e_neg = jnp.exp(-cs)
    e_end = jnp.exp(tot - cs)
    at = a * jnp.exp(cs - lw)
    rt = r * e_pos
    bt = b * e_neg
    kt = kd * e_neg
    bh = b * e_end
    kh = kd * e_end
    p_end = jnp.exp(tot)

    row = lax.broadcasted_iota(jnp.int32, (c, c), 0)
    col = lax.broadcasted_iota(jnp.int32, (c, c), 1)
    before = (col > row) if reverse else (col < row)
    upto = (col >= row) if reverse else (col <= row)
    eye = row == col
    same_head = (row // RW_HEAD_DIM) == (col // RW_HEAD_DIM)
    both = lambda m: jnp.concatenate([m, m], axis=1)
    lane = lax.broadcasted_iota(jnp.int32, (c, LANES), 1)
    heads = [(lane // RW_HEAD_DIM) == h for h in range(2)]

    def by_head(x):
        return jnp.concatenate([jnp.where(m, x, 0.0) for m in heads], axis=0).astype(BF16)

    pairs = []
    for p in range(RW_HEADS // 2):
        sl = slice(p * LANES, (p + 1) * LANES)
        pairs.append(dict(
            slot=slot, p=p, by_head=by_head,
            atrt=jnp.concatenate([at[:, sl], rt[:, sl]], axis=0).astype(BF16),
            bk=jnp.concatenate([by_head(bt[:, sl]), by_head(kt[:, sl])], axis=0),
            v2=by_head(v[:, sl]), v_t=v[:, sl].T,
            bkh=jnp.concatenate([bh[:, sl], kh[:, sl]], axis=0).astype(BF16),
            p_end=p_end[:, sl], masks=(both(before), both(upto), both(eye), same_head)))
    return pairs


def _scan_chunk(pairs, s_ref):
    c = CHUNK
    zero = jnp.zeros((c, c), BF16)
    for pr in pairs:
        before, upto, eye, _ = pr["masks"]
        gm = _smm(pr["atrt"], pr["bk"], NT)
        a_ab = jnp.where(before, gm[:c, :2 * c], 0.0)
        pr["a_ak"] = jnp.where(before, gm[:c, 2 * c:], 0.0).astype(BF16)
        pr["a_rb"] = jnp.where(upto, gm[c:, :2 * c], 0.0).astype(BF16)
        pr["a_rk"] = jnp.where(upto, gm[c:, 2 * c:], 0.0).astype(BF16)
        pr["t_inv"] = jnp.where(eye, 1.0, a_ab)
        pr["apow"] = a_ab.astype(BF16)

    def block_diag(m):
        return jnp.concatenate([jnp.concatenate([m[:, :c], zero], axis=1),
                                jnp.concatenate([zero, m[:, c:]], axis=1)], axis=0)

    n_sq = int(math.log2(c)) - 1
    for pr in pairs:
        pr["apow"] = _smm(pr["apow"], block_diag(pr["apow"])).astype(BF16)
    for k in range(n_sq):
        for pr in pairs:
            w_bd = block_diag(pr["apow"])
            if k + 1 < n_sq:
                res = _smm(jnp.concatenate([pr["apow"], pr["t_inv"].astype(BF16)], axis=0), w_bd)
                pr["apow"] = res[:c].astype(BF16)
                pr["t_inv"] = pr["t_inv"] + res[c:]
            else:
                pr["t_inv"] = pr["t_inv"] + _smm(pr["t_inv"], w_bd)
    for pr in pairs:
        s0 = s_ref[pr["slot"], pr["p"]]
        pr["s0"] = s0
        hr = _smm(pr["atrt"], s0, NT)
        pr["ah"], pr["rh"] = hr[:c], hr[c:]
    for pr in pairs:
        x = _smm(pr["a_ak"], pr["v2"])
        pr["u"] = _smm(pr["t_inv"], pr["by_head"](pr["ah"] + x))
    outs = {}
    for pr in pairs:
        pr["y"] = pr["rh"] + _smm(pr["a_rb"], pr["by_head"](pr["u"])) + _smm(pr["a_rk"], pr["v2"])
    for pr in pairs:
        same_head = pr["masks"][3]
        uv_t = jnp.concatenate([pr["u"].T, pr["v_t"]], axis=1)
        upd = _smm(uv_t, pr["bkh"])
        s_ref[pr["slot"], pr["p"]] = pr["s0"] * pr["p_end"] + jnp.where(same_head, upd, 0.0)
        outs[(pr["slot"], pr["p"])] = pr["y"]
    n_pairs = RW_HEADS // 2
    return [jnp.concatenate([outs[(d, p)] for p in range(n_pairs)], axis=1) for d in range(2)]


def _rw_scan_kernel(comf_ref, lwf_ref, dkbf_ref, comb_ref, lwb_ref, dkbb_ref, tri_ref, yf_ref, yb_ref, s_ref):
    @pl.when(pl.program_id(0) == 0)
    def _():
        s_ref[...] = jnp.zeros_like(s_ref)

    pairs = (_scan_dir(comf_ref[...], lwf_ref[0], dkbf_ref[0], tri_ref[0], 0, False)
             + _scan_dir(comb_ref[...], lwb_ref[0], dkbb_ref[0], tri_ref[1], 1, True))
    yf, yb = _scan_chunk(pairs, s_ref)
    yf_ref[...] = yf
    yb_ref[...] = yb


def _rw_scan(com, lw, dkb, tri, n_ctx_chunks):
    n_rows = com.shape[1]
    w = RW_WIDTH
    c = CHUNK
    assert c == LANES
    n_chunks = n_rows // c
    nc = n_ctx_chunks
    nl = n_chunks - nc

    def fwd_block(i):
        return jnp.where(i < nc, nl + i, i - nc)

    def bwd_block(i):
        return jnp.where(i < nc, nl + nc - 1 - i, nl - 1 - (i - nc))

    return pl.pallas_call(
        _rw_scan_kernel,
        grid=(n_chunks,),
        in_specs=[pl.BlockSpec((3, c, w), lambda i: (0, fwd_block(i), 0)),
                  pl.BlockSpec((1, c, w), lambda i: (0, fwd_block(i), 0)),
                  pl.BlockSpec((1, 2, c, w), lambda i: (0, 0, fwd_block(i), 0)),
                  pl.BlockSpec((3, c, w), lambda i: (0, bwd_block(i), 0)),
                  pl.BlockSpec((1, c, w), lambda i: (1, bwd_block(i), 0)),
                  pl.BlockSpec((1, 2, c, w), lambda i: (1, 0, bwd_block(i), 0)),
                  pl.BlockSpec((2, c, c), lambda i: (0, 0, 0))],
        out_specs=[pl.BlockSpec((c, w), lambda i: (fwd_block(i), 0)),
                   pl.BlockSpec((c, w), lambda i: (bwd_block(i), 0))],
        out_shape=[jax.ShapeDtypeStruct((n_rows, w), F32)] * 2,
        scratch_shapes=[pltpu.VMEM((2, RW_HEADS // 2, LANES, LANES), F32)],
        compiler_params=_cparams(("arbitrary",)),
        name="rw_scan",
    )(com, lw, dkb, com, lw, dkb, tri)


def _rw_post(y, g, bonus, avg, vec):
    mu = _mm(y, avg, 2, 1)
    yc = y - mu
    var = _mm(yc * yc, avg, 2, 1)
    yn = yc * lax.rsqrt(var + RW_LN_EPS)
    return (yn * vec[0:1] + vec[1:2] + bonus) * g


def _rms(x, n):
    return x * lax.rsqrt(jnp.sum(x * x, axis=-1, keepdims=True) * (1.0 / n) + NORM_EPS)


def _mla_prep_body(u, wq_ref, wk_ref, wv_ref, vq_ref, vkv_ref, gains_ref, ones_ref, rope_ref,
                   q_ref, k_ref, vt_ref):
    cq =_rms(u[:, :Q_LORA], Q_LORA) * vq_ref[...]
    q = jnp.dot(cq.astype(BF16), wq_ref[...], preferred_element_type=F32)
    ckv = _rms(u[:, Q_LORA:Q_LORA + KV_LORA], KV_LORA) * vkv_ref[...]
    ckv = ckv.astype(BF16)
    kn = jnp.dot(ckv, wk_ref[...], preferred_element_type=F32)
    vv = jnp.dot(ckv, wv_ref[...], preferred_element_type=F32)
    kr = pltpu.roll(u[:, Q_LORA + KV_LORA:], NOPE_DIM, 1)
    lane = lax.broadcasted_iota(jnp.int32, (1, HEAD_PAD), 1)
    ones_col = (lane == V_DIM).astype(F32)
    heads = lambda a: jnp.concatenate([a] * MLA_HEADS, axis=1)
    gains = gains_ref[...]
    gq, gk, shift_q, shift_k = (heads(gains[i:i + 1]) for i in range(4))
    tab = rope_ref[...]
    cos, sin_a, sin_b = heads(tab[0]), heads(tab[1]), heads(tab[2])
    width = MLA_HEADS * HEAD_PAD
    half = ROPE_DIM // 4

    def norm_rot(t, gain, shift):
        ms = _mm(t * t, ones_ref[...], 2, 1) * (1.0 / QK_DIM)
        t = t * lax.rsqrt(ms + NORM_EPS) * gain
        t = t * cos + pltpu.roll(t, width - half, 1) * sin_a + pltpu.roll(t, half, 1) * sin_b
        return (t + shift).astype(BF16)

    qo = norm_rot(q, gq, shift_q)
    ko = norm_rot(kn + heads(kr), gk, shift_k)
    for h in range(MLA_HEADS):
        sl = slice(h * HEAD_PAD, (h + 1) * HEAD_PAD)
        q_ref[h] = qo[:, sl]
        k_ref[h] = ko[:, sl]
        vt_ref[h] = (vv[:, sl] + ones_col).T[0:V_ROWS].astype(BF16)


def _prep_kernel(ul_ref, uc_ref, wlora_ref, alora_ref, gup_ref, ones_ref, vec_ref,
                 wq_ref, wk_ref, wv_ref, vq_ref, vkv_ref, gains_ref, head_ones_ref, rope_ref, chan_ref,
                 com_ref, lw_ref, dkb_ref, post_ref, q_ref, k_ref, vt_ref, wc_ref, *, n_lat_tiles):
    u = jnp.where(pl.program_id(0) < n_lat_tiles, ul_ref[...], uc_ref[...])
    _rw_prep_body(u[:, U_RW:U_RW + RW_COLS], wlora_ref, alora_ref, gup_ref, ones_ref, vec_ref,
                  com_ref, lw_ref, dkb_ref, post_ref)
    _mla_prep_body(u[:, U_MLA:U_MLA + MLA_PAD], wq_ref, wk_ref, wv_ref, vq_ref, vkv_ref, gains_ref,
                   head_ones_ref, rope_ref, q_ref, k_ref, vt_ref)
    wc_ref[...] = _mm(u[:, U_FT:U_FT + FT_WIDTH], chan_ref[...], 2, 2)


def _prep(u_lat, u_ctx, rw_weights, mla_weights, rope_tab, chan_mat):
    tm = math.gcd(STREAM_TILE, u_ctx.shape[0])
    n_lat_tiles = u_lat.shape[0] // tm
    n_rows = u_lat.shape[0] + u_ctx.shape[0]
    w = RW_WIDTH
    full = lambda a: pl.BlockSpec(a.shape, lambda i: (0,) * a.ndim)
    consts = list(rw_weights) + list(mla_weights)
    hs = pl.BlockSpec((MLA_HEADS, tm, HEAD_PAD), lambda i: (0, i, 0))
    shp = jax.ShapeDtypeStruct((MLA_HEADS, n_rows, HEAD_PAD), BF16)
    return pl.pallas_call(
        functools.partial(_prep_kernel, n_lat_tiles=n_lat_tiles),
        grid=(n_rows // tm,),
        in_specs=_two_stream_specs(tm, u_lat.shape[1], 0, n_lat_tiles) + [full(a) for a in consts] + [
            pl.BlockSpec((3, tm, HEAD_PAD), lambda i: (0, i, 0)), full(chan_mat)],
        out_specs=[pl.BlockSpec((3, tm, w), lambda i: (0, i, 0)),
                   pl.BlockSpec((2, tm, w), lambda i: (0, i, 0)),
                   pl.BlockSpec((2, 2, tm, w), lambda i: (0, 0, i, 0)),
                   pl.BlockSpec((2, tm, w), lambda i: (0, i, 0)),
                   hs, hs,
                   pl.BlockSpec((MLA_HEADS, V_ROWS, tm), lambda i: (0, 0, i)),
                   pl.BlockSpec((tm, 2 * FT_WIDTH), lambda i: (i, 0))],
        out_shape=[jax.ShapeDtypeStruct((3, n_rows, w), BF16),
                   jax.ShapeDtypeStruct((2, n_rows, w), F32),
                   jax.ShapeDtypeStruct((2, 2, n_rows, w), BF16),
                   jax.ShapeDtypeStruct((2, n_rows, w), BF16),
                   shp, shp,
                   jax.ShapeDtypeStruct((MLA_HEADS, V_ROWS, n_rows), BF16),
                   jax.ShapeDtypeStruct((n_rows, 2 * FT_WIDTH), F32)],
        compiler_params=_cparams(("parallel",)),
        name="prep",
    )(u_lat, u_ctx, *consts, rope_tab, chan_mat)


def _attn_kernel(q_ref, k_ref, vt_ref, o_ref, acc_scr, *rest, nk, sub, online):
    kv = pl.program_id(2)
    if online:
        m_scr, = rest

    @pl.when(kv == 0)
    def _():
        acc_scr[...] = jnp.zeros_like(acc_scr)
        if online:
            m_scr[...] = jnp.full_like(m_scr, -jnp.inf)

    k = k_ref[0]
    vt = vt_ref[0]
    n = q_ref.shape[1] // sub
    st, p, alpha = {}, {}, {}
    for t in range(n + 2):
        if t < n:
            st[t] = lax.dot_general(k, q_ref[0, t * sub:(t + 1) * sub], NT, preferred_element_type=F32)
        if 0 <= t - 1 < n:
            r = t - 1
            cols = slice(r * sub, (r + 1) * sub)
            s_r = st.pop(r)
            if online:
                m_prev = m_scr[:, cols]
                m_new = jnp.maximum(m_prev, jnp.max(s_r, axis=0, keepdims=True))
                alpha[r] = jnp.exp2(m_prev - m_new)
                m_scr[:, cols] = m_new
                s_r = s_r - m_new
            p[r] = jnp.exp2(s_r).astype(BF16)
        if 0 <= t - 2 < n:
            r = t - 2
            cols = slice(r * sub, (r + 1) * sub)
            upd = jnp.dot(vt, p.pop(r), preferred_element_type=F32)
            prev = acc_scr[0:V_ROWS, cols]
            acc_scr[0:V_ROWS, cols] = (alpha.pop(r) * prev if online else prev) + upd

    @pl.when(kv == nk - 1)
    def _():
        acc = acc_scr[...]
        o_ref[...] = (acc / acc[V_DIM:V_DIM + 1]).T.astype(BF16)


def _attention_call(q, k, vt, online, q_rows, k_rows):
    h, _, dp = q.shape
    lq = q_rows[1] - q_rows[0]
    lk = k_rows[1] - k_rows[0]
    tq = _row_tile(lq, ATT_TQ)
    sub = min(tq, ATT_SUB)
    tk = max(t for t in range(LANES, min(lk, ATT_TK) + 1, LANES) if lk % t == 0 and k_rows[0] % t == 0)
    nk = lk // tk
    assert q_rows[0] % tq == 0
    qo = q_rows[0] // tq
    ko = k_rows[0] // tk
    scratch = [pltpu.VMEM((HEAD_PAD, tq), F32)]
    if online:
        scratch.append(pltpu.VMEM((1, tq), F32))
    return pl.pallas_call(
        functools.partial(_attn_kernel, nk=nk, sub=sub, online=online),
        grid=(h, lq // tq, nk),
        in_specs=[pl.BlockSpec((1, tq, dp), lambda hh, i, j: (hh, i + qo, 0)),
                  pl.BlockSpec((1, tk, dp), lambda hh, i, j: (hh, j + ko, 0)),
                  pl.BlockSpec((1, V_ROWS, tk), lambda hh, i, j: (hh, 0, j + ko))],
        out_specs=pl.BlockSpec((tq, dp), lambda hh, i, j: (i, hh)),
        out_shape=jax.ShapeDtypeStruct((lq, h * dp), BF16),
        scratch_shapes=scratch,
        compiler_params=_cparams(("parallel", "parallel", "arbitrary")),
        name="attn_online" if online else "attn",
    )(q, k, vt)


def _attention(q, k, vt, score_bound, q_rows, k_rows):
    return lax.cond(2.0 * score_bound <= MAX_SCORE_SPAN,
                    lambda: _attention_call(q, k, vt, False, q_rows, k_rows),
                    lambda: _attention_call(q, k, vt, True, q_rows, k_rows))


def _ft_s1_kernel(w_ref, cs_ref, tw_ref, o_ref, *, l1, g):
    z = _mm(cs_ref[...], w_ref[...], 2, 2)
    fw = FT_WIDTH
    for t in range(g):
        zr = z[:, t * 2 * fw:t * 2 * fw + fw]
        zi = z[:, t * 2 * fw + fw:(t + 1) * 2 * fw]
        a_re = zr[:l1] + zi[l1:]
        a_im = zi[:l1] - zr[l1:]
        tc = jnp.concatenate([tw_ref[0, t]] * (fw // LANES), axis=1)
        ts = jnp.concatenate([tw_ref[1, t]] * (fw // LANES), axis=1)
        o_ref[:, t * 2 * fw:t * 2 * fw + fw] = a_re * tc + a_im * ts
        o_ref[:, t * 2 * fw + fw:(t + 1) * 2 * fw] = a_im * tc - a_re * ts


def _ft_s1(w2d, cs, tw, l1, l2):
    g = min(8, l2)
    blk = g * 2 * FT_WIDTH
    return pl.pallas_call(
        functools.partial(_ft_s1_kernel, l1=l1, g=g),
        grid=(l2 // g,),
        in_specs=[pl.BlockSpec((l1, blk), lambda i: (0, i)),
                  pl.BlockSpec(cs.shape, lambda i: (0, 0)),
                  pl.BlockSpec((2, g, l1, LANES), lambda i: (0, i, 0, 0))],
        out_specs=pl.BlockSpec((l1, blk), lambda i: (0, i)),
        out_shape=jax.ShapeDtypeStruct((l1, l2 * 2 * FT_WIDTH), F32),
        compiler_params=_cparams(("parallel",)),
        name="ft_s1",
    )(w2d, cs, tw)


def _ft_s2_kernel(b_ref, cs_ref, o_ref, *, g, scale):
    fw = FT_WIDTH
    cmat = cs_ref[0]
    smat = cs_ref[1]
    for kk in range(g):
        blk = b_ref[kk]
        out = _mm(cmat, blk[:, :fw], 2, 2) + _mm(smat, blk[:, fw:], 2, 2)
        o_ref[:, kk * fw:(kk + 1) * fw] = out * scale


def _ft_s2(b3d, cs2, l1, l2, scale):
    g = min(8, l1)
    return pl.pallas_call(
        functools.partial(_ft_s2_kernel, g=g, scale=scale),
        grid=(l1 // g,),
        in_specs=[pl.BlockSpec((g, l2, 2 * FT_WIDTH), lambda i: (i, 0, 0)),
                  pl.BlockSpec(cs2.shape, lambda i: (0, 0, 0))],
        out_specs=pl.BlockSpec((l2, g * FT_WIDTH), lambda i: (0, i)),
        out_shape=jax.ShapeDtypeStruct((l2, l1 * FT_WIDTH), F32),
        compiler_params=_cparams(("parallel",)),
        name="ft_s2",
    )(b3d, cs2)


def _ft_tables(n):
    l1 = 1 << (int(math.log2(n)) // 2)
    l2 = n // l1
    assert l1 * l2 == n and l1 % 8 == 0 and l2 % 8 == 0
    k1 = np.arange(l1)
    ang1 = 2.0 * np.pi * np.outer(k1, np.arange(l1)) / l1
    cs1 = np.concatenate([np.cos(ang1), np.sin(ang1)], axis=0)
    angt = 2.0 * np.pi * np.outer(np.arange(l2), k1) / n
    tw = np.stack([np.cos(angt), np.sin(angt)], axis=0)
    tw = np.broadcast_to(tw[..., None], (2, l2, l1, LANES))
    ang2 = 2.0 * np.pi * np.outer(np.arange(l2), np.arange(l2)) / l2
    cs2 = np.stack([np.cos(ang2), np.sin(ang2)], axis=0)
    f = lambda a: jnp.asarray(np.ascontiguousarray(a), F32)
    return l1, l2, f(cs1), f(tw), f(cs2)


def _chan_dft_matrix():
    n = FT_GROUP_DIM
    ang = 2.0 * np.pi * np.outer(np.arange(n), np.arange(n)) / n
    eye = np.eye(FT_GROUPS)
    mat = np.concatenate([np.kron(eye, np.cos(ang)), -np.kron(eye, np.sin(ang))], axis=1)
    return jnp.asarray(mat, F32)


def _fourier_mix(wc):
    n = wc.shape[0]
    l1, l2, cs1, tw, cs2 = _ft_tables(n)
    b = _ft_s1(wc.reshape(l1, l2 * 2 * FT_WIDTH), cs1, tw, l1, l2)
    scale = 1.0 / math.sqrt(n * FT_GROUP_DIM)
    out = _ft_s2(b.reshape(l1, l2, 2 * FT_WIDTH), cs2, l1, l2, scale)
    return out.reshape(n, FT_WIDTH)


def _merge_kernel(yf_ref, ybw_ref, post_ref, avg_ref, ln_ref, yb_ref, yc_ref, ga_ref, gb_ref, gc_ref,
                  x_ref, gt_ref, wa_ref, wb_ref, wc_ref, wo_ref, o_ref):
    def branch(y, w_ref, g_ref):
        t = jnp.dot(y.astype(BF16), w_ref[...], preferred_element_type=F32)
        return g_ref[...].astype(F32) * t

    ya = _rw_post(yf_ref[...] + ybw_ref[...], post_ref[0].astype(F32), post_ref[1].astype(F32),
                  avg_ref[...], ln_ref[...])
    m = branch(ya, wa_ref, ga_ref) + branch(yb_ref[...], wb_ref, gb_ref) + branch(yc_ref[...], wc_ref, gc_ref)
    out = jnp.dot(m.astype(BF16), wo_ref[...], preferred_element_type=F32)
    o_ref[...] = x_ref[...] + gt_ref[...] * out


def _merge(yf, ybw, post, row_off, avg_bd, ln_vec, yb, yc, gates, x, gt, wa, wb, wc, wo):
    n_rows, d = x.shape
    tm = _row_tile(n_rows, 512)
    assert row_off % tm == 0
    off = row_off // tm
    w = RW_WIDTH
    full = lambda a: pl.BlockSpec(a.shape, lambda i: (0,) * a.ndim)
    rows = lambda a: pl.BlockSpec((tm, a.shape[1]), lambda i: (i, 0))
    gate = lambda b: pl.BlockSpec((tm, d), lambda i: (i, b))
    scan = pl.BlockSpec((tm, w), lambda i: (i + off, 0))
    return pl.pallas_call(
        _merge_kernel,
        grid=(n_rows // tm,),
        in_specs=[scan, scan, pl.BlockSpec((2, tm, w), lambda i: (0, i + off, 0)), full(avg_bd), full(ln_vec),
                  rows(yb), rows(yc), gate(0), gate(1), gate(2), rows(x), full(gt),
                  full(wa), full(wb), full(wc), full(wo)],
        out_specs=pl.BlockSpec((tm, d), lambda i: (i, 0)),
        out_shape=jax.ShapeDtypeStruct((n_rows, d), F32),
        compiler_params=_cparams(("parallel",)),
        name="merge",
    )(yf, ybw, post, avg_bd, ln_vec, yb, yc, gates, gates, gates, x, gt, wa, wb, wc, wo)


def _stream_mods(mod):
    return [mod[i * D_MODEL:(i + 1) * D_MODEL][None] for i in range(N_MOD)]


def _pad_heads(w, src_cols, width):
    k = w.shape[0]
    wh = w.reshape(k, MLA_HEADS, width)[:, :, src_cols]
    wh = jnp.pad(wh, ((0, 0), (0, 0), (0, HEAD_PAD - wh.shape[2])))
    return wh.reshape(k, MLA_HEADS * HEAD_PAD)


def _block_diag2(a, b):
    z01 = jnp.zeros((a.shape[0], b.shape[1]), a.dtype)
    z10 = jnp.zeros((b.shape[0], a.shape[1]), a.dtype)
    return jnp.concatenate([jnp.concatenate([a, z01], axis=1), jnp.concatenate([z10, b], axis=1)], axis=0)


def _rope_tables(n, n_ctx):
    half = ROPE_DIM // 2
    inv = ROPE_BASE ** (-np.arange(0, half, 2, dtype=np.float32) / half)
    t = np.arange(n)
    row = (t // GRID_W).astype(np.float32)
    col = (t % GRID_W).astype(np.float32)
    q = half // 2
    cos = np.ones((n, HEAD_PAD), np.float32)
    sin_a = np.zeros((n, HEAD_PAD), np.float32)
    sin_b = np.zeros((n, HEAD_PAD), np.float32)
    for base, pos in ((NOPE_DIM, row), (NOPE_DIM + half, col)):
        ang = pos[:, None] * inv[None, :]
        c, s = np.cos(ang), np.sin(ang)
        cos[:, base:base + q] = c
        cos[:, base + q:base + 2 * q] = c
        sin_a[:, base:base + q] = -s
        sin_b[:, base + q:base + 2 * q] = s
    tab = np.stack([cos, sin_a, sin_b], axis=0)
    ident = np.stack([np.ones((n_ctx, HEAD_PAD), np.float32)] + [np.zeros((n_ctx, HEAD_PAD), np.float32)] * 2)
    return jnp.asarray(np.concatenate([tab, ident], axis=1))


def _layer_weights(l, w):
    w_in = w["w_in"][l]
    o1 = RW_COLS
    o2 = o1 + MLA_COLS
    o3 = o2 + FT_WIDTH
    w_in_p = jnp.concatenate(
        [w_in[:, o3:], w_in[:, :o1], w_in[:, o1:o2],
         jnp.zeros((D_MODEL, MLA_PAD - MLA_COLS), F32), w_in[:, o2:o3]], axis=1).astype(BF16)
    pad_gain = lambda g: jnp.pad(g, (0, HEAD_PAD - QK_DIM))
    wb = w["w_branch_b"][l].reshape(MLA_HEADS, V_DIM, D_MODEL)
    wb = jnp.pad(wb, ((0, 0), (0, HEAD_PAD - V_DIM), (0, 0))).reshape(MLA_HEADS * HEAD_PAD, D_MODEL)
    q_scale = math.log2(math.e) * QK_DIM ** -0.5
    score_bound = (BOUND_MARGIN * QK_DIM * q_scale * jnp.max(jnp.abs(w["q_gain"][l]))
                   * jnp.max(jnp.abs(w["k_gain"][l])))
    shift_lane = (jnp.arange(HEAD_PAD) == QK_DIM).astype(F32)
    return dict(
        w_in=w_in_p, cw_in=w["rw_conv"][l],
        wlora=_block_diag2(w["rw_w_up"][l, 0], w["rw_w_up"][l, 1]),
        alora=_block_diag2(w["rw_a_up"][l, 0], w["rw_a_up"][l, 1]),
        gup=w["rw_g_up"][l],
        rw_vec=jnp.stack([w["rw_k_k"][l], w["rw_k_a"][l], w["rw_r_k"][l].reshape(-1),
                          w["rw_w0"][l, 0], w["rw_w0"][l, 1], w["rw_a0"][l, 0], w["rw_a0"][l, 1],
                          jnp.zeros((RW_WIDTH,), F32)], axis=0),
        ln_vec=jnp.stack([w["rw_ln_w"][l], w["rw_ln_b"][l]], axis=0),
        wq=_pad_heads(w["w_uq"][l], slice(0, QK_DIM), QK_DIM).astype(BF16),
        wk=_pad_heads(w["w_ukv"][l], slice(0, NOPE_DIM), NOPE_DIM + V_DIM).astype(BF16),
        wv=_pad_heads(w["w_ukv"][l], slice(NOPE_DIM, NOPE_DIM + V_DIM), NOPE_DIM + V_DIM).astype(BF16),
        vq=w["mla_q_norm"][l][None], vkv=w["mla_kv_norm"][l][None],
        gains=jnp.stack([pad_gain(w["q_gain"][l]) * q_scale, pad_gain(w["k_gain"][l]),
                             -score_bound * shift_lane, shift_lane], axis=0),
        score_bound=score_bound,
        wa=w["w_branch_a"][l].astype(BF16), wb=wb.astype(BF16), wc=w["w_branch_c"][l].astype(BF16),
        wo=w["w_out"][l].astype(BF16),
        w_up=w["w_up"][l].astype(BF16), cw_ffn=w["ffn_conv"][l], w_down=w["w_down"][l].astype(BF16),
    )


def _forward(x, c, ctx, c_ctx, w):
    depth = w["w_ada"].shape[0]
    x_lat = x[0]
    x_ctx = ctx[0]
    n_lat = x_lat.shape[0]
    n_ctx = x_ctx.shape[0]
    assert n_lat % CHUNK == 0 and n_ctx % CHUNK == 0 and n_lat % GRID_W == 0
    n_all = n_lat + n_ctx

    cc = jnp.zeros((8, D_MODEL), F32).at[0].set(c[0]).at[1].set(c_ctx)
    mods = _mods(cc, w["w_ada"], w["b_ada"])

    head = np.arange(RW_WIDTH) // RW_HEAD_DIM
    ones_bd = jnp.asarray((head[:, None] == head[None, :]).astype(np.float32), BF16)
    avg_bd = ones_bd * (1.0 / RW_HEAD_DIM)
    mla_head = np.arange(MLA_HEADS * HEAD_PAD) // HEAD_PAD
    head_ones = jnp.asarray((mla_head[:, None] == mla_head[None, :]).astype(np.float32), BF16)
    ti = np.arange(CHUNK)
    tri = jnp.asarray(np.stack([ti[None, :] <= ti[:, None], ti[None, :] >= ti[:, None]]).astype(np.float32),
                      BF16)
    chan_mat = _chan_dft_matrix()
    rope_tab = _rope_tables(n_lat, n_ctx)

    for l in range(depth):
        last = l == depth - 1
        p = _layer_weights(l, w)
        streams = [(x_lat, _stream_mods(mods[l, 0]), (0, n_lat)),
                   (x_ctx, _stream_mods(mods[l, 1]), (n_lat, n_all))]
        us, gates = [], []
        for xs, (sh1, sc1, _, _, _, _), _ in streams:
            gain1 = w["g_norm1"][l][None] * (1.0 + sc1)
            g_s, u_s = _in_proj(xs, gain1, sh1, p["w_in"], p["cw_in"])
            gates.append(g_s)
            us.append(u_s)
        com, lw, dkb, post, q, k, vt, wc = _prep(
            us[0], us[1], (p["wlora"], p["alora"], p["gup"], ones_bd, p["rw_vec"]),
            (p["wq"], p["wk"], p["wv"], p["vq"], p["vkv"], p["gains"], head_ones), rope_tab, chan_mat)
        yf, yb = _rw_scan(com, lw, dkb, tri, n_ctx // CHUNK)

        new = []
        for si, (xs, (_, _, gt1, sh2, sc2, gt2), rows) in enumerate(streams):
            if si == 1 and last:
                continue
            y_att = _attention(q, k, vt, p["score_bound"], rows, (0, n_all) if si == 0 else rows)
            yc = _fourier_mix(wc[rows[0]:rows[1]])
            x1 = _merge(yf, yb, post, rows[0], avg_bd, p["ln_vec"], y_att, yc, gates[si], xs, gt1,
                        p["wa"], p["wb"], p["wc"], p["wo"])
            gain2 = w["g_norm2"][l][None] * (1.0 + sc2)
            new.append(_ffn(x1, gain2, sh2, gt2, p["w_up"], p["cw_ffn"], p["w_down"]))
        x_lat = new[0]
        if not last:
            x_ctx = new[1]
    return x_lat[None]


def kernel(x, c, ctx, c_ctx, w_ada, b_ada, g_norm1, g_norm2, w_in, rw_conv, rw_w0, rw_w_up, rw_a0, rw_a_up,
           rw_g_up, rw_k_k, rw_k_a, rw_r_k, rw_ln_w, rw_ln_b, mla_q_norm, w_uq, mla_kv_norm, w_ukv, q_gain,
           k_gain, w_branch_a, w_branch_b, w_branch_c, w_out, ffn_conv, w_up, w_down):
    w = dict(w_ada=w_ada, b_ada=b_ada, g_norm1=g_norm1, g_norm2=g_norm2, w_in=w_in, rw_conv=rw_conv,
             rw_w0=rw_w0, rw_w_up=rw_w_up, rw_a0=rw_a0, rw_a_up=rw_a_up, rw_g_up=rw_g_up, rw_k_k=rw_k_k,
             rw_k_a=rw_k_a, rw_r_k=rw_r_k, rw_ln_w=rw_ln_w, rw_ln_b=rw_ln_b, mla_q_norm=mla_q_norm,
             w_uq=w_uq, mla_kv_norm=mla_kv_norm, w_ukv=w_ukv, q_gain=q_gain, k_gain=k_gain,
             w_branch_a=w_branch_a, w_branch_b=w_branch_b, w_branch_c=w_branch_c, w_out=w_out,
             ffn_conv=ffn_conv, w_up=w_up, w_down=w_down)
    return _forward(x, c, ctx, c_ctx, w)
```

```python
import functools
import math

import numpy as np
import jax
import jax.numpy as jnp
from jax import lax
from jax.experimental import pallas as pl
from jax.experimental.pallas import tpu as pltpu

F32 = jnp.float32
BF16 = jnp.bfloat16

D_MODEL = 1024
GRID_W = 64
NORM_EPS = 1e-6
RW_HEADS = 6
RW_HEAD_DIM = 64
RW_WIDTH = RW_HEADS * RW_HEAD_DIM
DECAY_LORA = 64
ICLR_LORA = 64
GATE_LORA = 128
RW_LN_EPS = 64e-5
MLA_HEADS = 6
Q_LORA = 384
KV_LORA = 256
NOPE_DIM = 64
ROPE_DIM = 32
V_DIM = 64
QK_DIM = NOPE_DIM + ROPE_DIM
ROPE_BASE = 10000.0
FT_GROUPS = 4
FT_GROUP_DIM = 64
FT_WIDTH = FT_GROUPS * FT_GROUP_DIM
D_FF = 2816
N_MOD = 6
RW_COLS = 3 * RW_WIDTH + 2 * DECAY_LORA + 2 * ICLR_LORA + GATE_LORA
MLA_COLS = Q_LORA + KV_LORA + ROPE_DIM

LANES = 128
HEAD_PAD = 128
MLA_PAD = 768
GATE_COLS = 3 * D_MODEL
RW_OFF = GATE_COLS
U_RW = 0
U_MLA = U_RW + RW_COLS
U_FT = U_MLA + MLA_PAD
HALO = 16
STREAM_TILE = 256
CHUNK = 128
VMEM_LIMIT = 48 * 1024 * 1024
V_ROWS = 80
BOUND_MARGIN = 1.02
MAX_SCORE_SPAN = 100.0
ATT_TQ, ATT_SUB, ATT_TK = 2048, 256, 3328

NN = (((1,), (0,)), ((), ()))
NT = (((1,), (1,)), ((), ()))


def _pieces(x, n):
    if x.dtype == BF16:
        return [x]
    out = []
    rem = x
    for i in range(n):
        p = rem.astype(BF16)
        out.append(p)
        if i + 1 < n:
            rem = rem - p.astype(F32)
    return out


def _mm(a, b, pa=1, pb=1, dims=NN):
    ap = _pieces(a, pa)
    bp = _pieces(b, pb)
    n = max(len(ap), len(bp))
    acc = None
    for i in reversed(range(len(ap))):
        for j in reversed(range(len(bp))):
            if i + j < n:
                t = lax.dot_general(ap[i], bp[j], dims, preferred_element_type=F32)
                acc = t if acc is None else acc + t
    return acc


def _sigmoid(x):
    return 0.5 * jnp.tanh(0.5 * x) + 0.5


def _cparams(sem):
    return pltpu.CompilerParams(dimension_semantics=sem, vmem_limit_bytes=VMEM_LIMIT)


def _row_tile(n, pref):
    t = min(n, pref)
    while n % t:
        t //= 2
    return t


def _mods_kernel(cc_ref, w_ref, b_ref, o_ref):
    cc = cc_ref[...]
    s = cc * _sigmoid(cc)
    o_ref[0] = _mm(s, w_ref[0], 2, 2) + b_ref[0]


def _mods(cc, w_ada, b_ada):
    depth, d, n = w_ada.shape
    tn = 1536
    return pl.pallas_call(
        _mods_kernel,
        grid=(depth, n // tn),
        in_specs=[pl.BlockSpec((8, d), lambda l, j: (0, 0)),
                  pl.BlockSpec((1, d, tn), lambda l, j: (l, 0, j)),
                  pl.BlockSpec((1, 1, tn), lambda l, j: (l, 0, j))],
        out_specs=pl.BlockSpec((1, 8, tn), lambda l, j: (l, 0, j)),
        out_shape=jax.ShapeDtypeStruct((depth, 8, n), F32),
        compiler_params=_cparams(("parallel", "parallel")),
        name="mods",
    )(cc, w_ada, b_ada.reshape(depth, 1, n))


def _modnorm(xv, g, s):
    ms = jnp.mean(xv * xv, axis=-1, keepdims=True)
    return (xv * lax.rsqrt(ms + NORM_EPS)) * g + s


def _fill_normed(h_scr, x_ref, xp_ref, xn_ref, g, s, i, nrt, tm):
    h_scr[HALO:HALO + tm] = _modnorm(x_ref[...], g, s).astype(BF16)
    hp = jnp.where(i > 0, _modnorm(xp_ref[...], g, s), 0.0)
    h_scr[0:HALO] = hp.astype(BF16)
    hn = jnp.where(i < nrt - 1, _modnorm(xn_ref[...], g, s), 0.0)
    h_scr[HALO + tm:2 * HALO + tm] = hn.astype(BF16)


def _conv3(u, cw, tm):
    rows = u.shape[0]
    up = pltpu.roll(u, 1, 0)
    un = pltpu.roll(u, rows - 1, 0)
    out = cw[0:1] * up + cw[1:2] * u + cw[2:3] * un
    return out[HALO:HALO + tm]


def _halo_specs(tm, d, n_rows):
    r = tm // HALO
    last = n_rows // HALO - 1
    return [pl.BlockSpec((tm, d), lambda i: (i, 0)),
            pl.BlockSpec((HALO, d), lambda i: (jnp.maximum(i * r - 1, 0), 0)),
            pl.BlockSpec((HALO, d), lambda i: (jnp.minimum((i + 1) * r, last), 0))]


def _resident(a):
    return pl.BlockSpec(a.shape, lambda i: (0,) * a.ndim, pipeline_mode=pl.Buffered(1))


def _in_kernel(x_ref, xp_ref, xn_ref, g_ref, s_ref, w_ref, cw_ref, gate_ref, o_ref, h_scr, *, tm, nrt, tn):
    _fill_normed(h_scr, x_ref, xp_ref, xn_ref, g_ref[...], s_ref[...], pl.program_id(0), nrt, tm)
    nj = w_ref.shape[1] // tn
    u = {}
    for t in range(nj + 1):
        if t < nj:
            conv = RW_OFF <= t * tn < RW_OFF + RW_COLS
            h = h_scr[...] if conv else h_scr[HALO:HALO + tm]
            u[t] = jnp.dot(h, w_ref[:, t * tn:(t + 1) * tn], preferred_element_type=F32)
        if t >= 1:
            lo = (t - 1) * tn
            val = u.pop(t - 1)
            if lo < GATE_COLS:
                gate_ref[:, lo:lo + tn] = _sigmoid(val).astype(BF16)
            elif lo < RW_OFF + RW_COLS:
                o_ref[:, lo - GATE_COLS:lo - GATE_COLS + tn] = _conv3(val, cw_ref[:, lo - RW_OFF:lo - RW_OFF + tn], tm)
            else:
                o_ref[:, lo - GATE_COLS:lo - GATE_COLS + tn] = val


def _in_proj(x, gain, shift, w, cw):
    n_rows, d = x.shape
    n = w.shape[1]
    tn = 512
    assert GATE_COLS % tn == 0 and RW_COLS % tn == 0 and n % tn == 0
    tm = _row_tile(n_rows, 512)
    nrt = n_rows // tm
    return pl.pallas_call(
        functools.partial(_in_kernel, tm=tm, nrt=nrt, tn=tn),
        grid=(nrt,),
        in_specs=_halo_specs(tm, d, n_rows) + [_resident(gain), _resident(shift), _resident(w), _resident(cw)],
        out_specs=[pl.BlockSpec((tm, GATE_COLS), lambda i: (i, 0)),
                   pl.BlockSpec((tm, n - GATE_COLS), lambda i: (i, 0))],
        out_shape=[jax.ShapeDtypeStruct((n_rows, GATE_COLS), BF16),
                   jax.ShapeDtypeStruct((n_rows, n - GATE_COLS), F32)],
        scratch_shapes=[pltpu.VMEM((tm + 2 * HALO, d), BF16)],
        compiler_params=_cparams(("parallel",)),
        name="in_proj",
    )(x, x, x, gain, shift, w, cw)


def _ffn_kernel(x_ref, xp_ref, xn_ref, g_ref, s_ref, gt_ref, wu_ref, cw_ref, wd_ref, o_ref, h_scr, *,
                tm, nrt, tf):
    _fill_normed(h_scr, x_ref, xp_ref, xn_ref, g_ref[...], s_ref[...], pl.program_id(0), nrt, tm)
    h = h_scr[...]
    nj = D_FF // tf
    u, gated = {}, {}
    for t in range(nj + 2):
        if t < nj:
            ca = slice(t * tf, (t + 1) * tf)
            cb = slice(D_FF + t * tf, D_FF + (t + 1) * tf)
            u[t] = (jnp.dot(h, wu_ref[:, ca], preferred_element_type=F32),
                    jnp.dot(h, wu_ref[:, cb], preferred_element_type=F32))
        if 0 <= t - 1 < nj:
            j = t - 1
            ua, ub = u.pop(j)
            a = _conv3(ua, cw_ref[:, j * tf:(j + 1) * tf], tm)
            b = _conv3(ub, cw_ref[:, D_FF + j * tf:D_FF + (j + 1) * tf], tm)
            gated[j] = (a * _sigmoid(a) * b).astype(BF16)
        if 0 <= t - 2 < nj:
            j = t - 2
            down = jnp.dot(gated.pop(j), wd_ref[j * tf:(j + 1) * tf, :], preferred_element_type=F32)
            if j == 0:
                o_ref[...] = down
            else:
                o_ref[...] += down
    o_ref[...] = x_ref[...] + gt_ref[...] * o_ref[...]


def _ffn(x, gain, shift, gate, w_up, cw, w_down):
    n_rows, d = x.shape
    tm = _row_tile(n_rows, 1024)
    nrt = n_rows // tm
    return pl.pallas_call(
        functools.partial(_ffn_kernel, tm=tm, nrt=nrt, tf=256),
        grid=(nrt,),
        in_specs=_halo_specs(tm, d, n_rows) + [_resident(gain), _resident(shift), _resident(gate),
                                               _resident(w_up), _resident(cw), _resident(w_down)],
        out_specs=pl.BlockSpec((tm, d), lambda i: (i, 0)),
        out_shape=jax.ShapeDtypeStruct((n_rows, d), F32),
        scratch_shapes=[pltpu.VMEM((tm + 2 * HALO, d), BF16)],
        compiler_params=_cparams(("parallel",)),
        name="ffn",
    )(x, x, x, gain, shift, gate, w_up, cw, w_down)


def _seg_sum(x, ones_bd):
    return _mm(x, ones_bd, 2, 1)


def _two_stream_specs(tm, width, col_block, n_lat_tiles):
    return [pl.BlockSpec((tm, width), lambda i: (jnp.minimum(i, n_lat_tiles - 1), col_block)),
            pl.BlockSpec((tm, width), lambda i: (jnp.maximum(i - n_lat_tiles, 0), col_block))]


def _rw_prep_body(u, wlora_ref, alora_ref, gup_ref, ones_ref, vec_ref, com_ref, lw_ref, dkb_ref, post_ref):
    w = RW_WIDTH
    r = u[:, 0:w]
    k = u[:, w:2 * w]
    v = u[:, 2 * w:3 * w]
    wd = u[:, 3 * w:3 * w + 2 * DECAY_LORA]
    ad = u[:, 3 * w + 2 * DECAY_LORA:3 * w + 2 * DECAY_LORA + 2 * ICLR_LORA]
    gd = u[:, 3 * w + 2 * DECAY_LORA + 2 * ICLR_LORA:]
    vec = vec_ref[...]
    k_k, k_a, r_k = vec[0:1], vec[1:2], vec[2:3]
    ones_bd = ones_ref[...]

    kk = k * k_k
    kk = kk * lax.rsqrt(_seg_sum(kk * kk, ones_bd) + 1e-12)
    g = _mm(_sigmoid(gd), gup_ref[...], 2, 2)
    wl = _mm(jnp.tanh(wd), wlora_ref[...], 2, 2)
    al = _mm(ad, alora_ref[...], 2, 2)
    com_ref[0] = r.astype(BF16)
    com_ref[1] = v.astype(BF16)
    com_ref[2] = (-kk).astype(BF16)
    kd_sum = None
    for d in range(2):
        w0 = vec[3 + d:4 + d]
        a0 = vec[5 + d:6 + d]
        lw = -math.exp(-0.5) * _sigmoid(w0 + wl[:, d * w:(d + 1) * w])
        a = _sigmoid(a0 + al[:, d * w:(d + 1) * w])
        kd = k * (1.0 + (a - 1.0) * k_a)
        lw_ref[d] = lw
        dkb_ref[d, 0] = kd.astype(BF16)
        dkb_ref[d, 1] = (kk * a).astype(BF16)
        kd_sum = kd if kd_sum is None else kd_sum + kd
    post_ref[0] = g.astype(BF16)
    post_ref[1] = (_seg_sum(r * kd_sum * r_k, ones_bd) * v).astype(BF16)


def _smm(a, b, dims=NN):
    return _mm(a, b, 1, 1, dims)


def _scan_dir(com, lw, dkb, tri, slot, reverse):
    c = CHUNK
    r, v, a = (com[i].astype(F32) for i in range(3))
    kd, b = dkb[0].astype(F32), dkb[1].astype(F32)
    cs = _mm(tri, lw, 1, 3)
    tot = cs[0:1] if reverse else cs[c - 1:c]
    e_pos = jnp.exp(cs)
    e_neg = jnp.exp(-cs)
    e_end = jnp.exp(tot - cs)
    at = a * jnp.exp(cs - lw)
    rt = r * e_pos
    bt = b * e_neg
    kt = kd * e_neg
    bh = b * e_end
    kh = kd * e_end
    p_end = jnp.exp(tot)

    row = lax.broadcasted_iota(jnp.int32, (c, c), 0)
    col = lax.broadcasted_iota(jnp.int32, (c, c), 1)
    before = (col > row) if reverse else (col < row)
    upto = (col >= row) if reverse else (col <= row)
    eye = row == col
    same_head = (row // RW_HEAD_DIM) == (col // RW_HEAD_DIM)
    both = lambda m: jnp.concatenate([m, m], axis=1)
    lane = lax.broadcasted_iota(jnp.int32, (c, LANES), 1)
    heads = [(lane // RW_HEAD_DIM) == h for h in range(2)]

    def by_head(x):
        return jnp.concatenate([jnp.where(m, x, 0.0) for m in heads], axis=0).astype(BF16)

    pairs = []
    for p in range(RW_HEADS // 2):
        sl = slice(p * LANES, (p + 1) * LANES)
        pairs.append(dict(
            slot=slot, p=p, by_head=by_head,
            atrt=jnp.concatenate([at[:, sl], rt[:, sl]], axis=0).astype(BF16),
            bk=jnp.concatenate([by_head(bt[:, sl]), by_head(kt[:, sl])], axis=0),
            v2=by_head(v[:, sl]), v_t=v[:, sl].T,
            bkh=jnp.concatenate([bh[:, sl], kh[:, sl]], axis=0).astype(BF16),
            p_end=p_end[:, sl], masks=(both(before), both(upto), both(eye), same_head)))
    return pairs


def _scan_chunk(pairs, s_ref):
    c = CHUNK
    zero = jnp.zeros((c, c), BF16)
    for pr in pairs:
        before, upto, eye, _ = pr["masks"]
        gm = _smm(pr["atrt"], pr["bk"], NT)
        a_ab = jnp.where(before, gm[:c, :2 * c], 0.0)
        pr["a_ak"] = jnp.where(before, gm[:c, 2 * c:], 0.0).astype(BF16)
        pr["a_rb"] = jnp.where(upto, gm[c:, :2 * c], 0.0).astype(BF16)
        pr["a_rk"] = jnp.where(upto, gm[c:, 2 * c:], 0.0).astype(BF16)
        pr["t_inv"] = jnp.where(eye, 1.0, a_ab)
        pr["apow"] = a_ab.astype(BF16)

    def block_diag(m):
        return jnp.concatenate([jnp.concatenate([m[:, :c], zero], axis=1),
                                jnp.concatenate([zero, m[:, c:]], axis=1)], axis=0)

    n_sq = int(math.log2(c)) - 1
    for pr in pairs:
        pr["apow"] = _smm(pr["apow"], block_diag(pr["apow"])).astype(BF16)
    for k in range(n_sq):
        for pr in pairs:
            w_bd = block_diag(pr["apow"])
            if k + 1 < n_sq:
                res = _smm(jnp.concatenate([pr["apow"], pr["t_inv"].astype(BF16)], axis=0), w_bd)
                pr["apow"] = res[:c].astype(BF16)
                pr["t_inv"] = pr["t_inv"] + res[c:]
            else:
                pr["t_inv"] = pr["t_inv"] + _smm(pr["t_inv"], w_bd)
    for pr in pairs:
        s0 = s_ref[pr["slot"], pr["p"]]
        pr["s0"] = s0
        hr = _smm(pr["atrt"], s0, NT)
        pr["ah"], pr["rh"] = hr[:c], hr[c:]
    for pr in pairs:
        x = _smm(pr["a_ak"], pr["v2"])
        pr["u"] = _smm(pr["t_inv"], pr["by_head"](pr["ah"] + x))
    outs = {}
    for pr in pairs:
        pr["y"] = pr["rh"] + _smm(pr["a_rb"], pr["by_head"](pr["u"])) + _smm(pr["a_rk"], pr["v2"])
    for pr in pairs:
        same_head = pr["masks"][3]
        uv_t = jnp.concatenate([pr["u"].T, pr["v_t"]], axis=1)
        upd = _smm(uv_t, pr["bkh"])
        s_ref[pr["slot"], pr["p"]] = pr["s0"] * pr["p_end"] + jnp.where(same_head, upd, 0.0)
        outs[(pr["slot"], pr["p"])] = pr["y"]
    n_pairs = RW_HEADS // 2
    return [jnp.concatenate([outs[(d, p)] for p in range(n_pairs)], axis=1) for d in range(2)]


def _rw_scan_kernel(comf_ref, lwf_ref, dkbf_ref, comb_ref, lwb_ref, dkbb_ref, tri_ref, yf_ref, yb_ref, s_ref):
    @pl.when(pl.program_id(0) == 0)
    def _():
        s_ref[...] = jnp.zeros_like(s_ref)

    pairs = (_scan_dir(comf_ref[...], lwf_ref[0], dkbf_ref[0], tri_ref[0], 0, False)
             + _scan_dir(comb_ref[...], lwb_ref[0], dkbb_ref[0], tri_ref[1], 1, True))
    yf, yb = _scan_chunk(pairs, s_ref)
    yf_ref[...] = yf.astype(BF16)
    yb_ref[...] = yb.astype(BF16)


def _rw_scan(com, lw, dkb, tri, n_ctx_chunks):
    n_rows = com.shape[1]
    w = RW_WIDTH
    c = CHUNK
    assert c == LANES
    n_chunks = n_rows // c
    nc = n_ctx_chunks
    nl = n_chunks - nc

    def fwd_block(i):
        return jnp.where(i < nc, nl + i, i - nc)

    def bwd_block(i):
        return jnp.where(i < nc, nl + nc - 1 - i, nl - 1 - (i - nc))

    return pl.pallas_call(
        _rw_scan_kernel,
        grid=(n_chunks,),
        in_specs=[pl.BlockSpec((3, c, w), lambda i: (0, fwd_block(i), 0)),
                  pl.BlockSpec((1, c, w), lambda i: (0, fwd_block(i), 0)),
                  pl.BlockSpec((1, 2, c, w), lambda i: (0, 0, fwd_block(i), 0)),
                  pl.BlockSpec((3, c, w), lambda i: (0, bwd_block(i), 0)),
                  pl.BlockSpec((1, c, w), lambda i: (1, bwd_block(i), 0)),
                  pl.BlockSpec((1, 2, c, w), lambda i: (1, 0, bwd_block(i), 0)),
                  pl.BlockSpec((2, c, c), lambda i: (0, 0, 0))],
        out_specs=[pl.BlockSpec((c, w), lambda i: (fwd_block(i), 0)),
                   pl.BlockSpec((c, w), lambda i: (bwd_block(i), 0))],
        out_shape=[jax.ShapeDtypeStruct((n_rows, w), BF16)] * 2,
        scratch_shapes=[pltpu.VMEM((2, RW_HEADS // 2, LANES, LANES), F32)],
        compiler_params=_cparams(("arbitrary",)),
        name="rw_scan",
    )(com, lw, dkb, com, lw, dkb, tri)


def _rw_post(y, g, bonus, avg, vec):
    mu = _mm(y, avg, 2, 1)
    yc = y - mu
    var = _mm(yc * yc, avg, 2, 1)
    yn = yc * lax.rsqrt(var + RW_LN_EPS)
    return (yn * vec[0:1] + vec[1:2] + bonus) * g


def _rms(x, n):
    return x * lax.rsqrt(jnp.sum(x * x, axis=-1, keepdims=True) * (1.0 / n) + NORM_EPS)


def _mla_prep_body(u, wq_ref, wk_ref, wv_ref, vq_ref, vkv_ref, gains_ref, ones_ref, rope_ref,
                   q_ref, k_ref, vt_ref):
    cq =_rms(u[:, :Q_LORA], Q_LORA) * vq_ref[...]
    q = jnp.dot(cq.astype(BF16), wq_ref[...], preferred_element_type=F32)
    ckv = _rms(u[:, Q_LORA:Q_LORA + KV_LORA], KV_LORA) * vkv_ref[...]
    ckv = ckv.astype(BF16)
    kn = jnp.dot(ckv, wk_ref[...], preferred_element_type=F32)
    vv = jnp.dot(ckv, wv_ref[...], preferred_element_type=F32)
    kr = pltpu.roll(u[:, Q_LORA + KV_LORA:], NOPE_DIM, 1)
    lane = lax.broadcasted_iota(jnp.int32, (1, HEAD_PAD), 1)
    ones_col = (lane == V_DIM).astype(F32)
    heads = lambda a: jnp.concatenate([a] * MLA_HEADS, axis=1)
    gains = gains_ref[...]
    gq, gk, shift_q, shift_k = (heads(gains[i:i + 1]) for i in range(4))
    tab = rope_ref[...]
    cos, sin_a, sin_b = heads(tab[0]), heads(tab[1]), heads(tab[2])
    width = MLA_HEADS * HEAD_PAD
    half = ROPE_DIM // 4

    def norm_rot(t, gain, shift):
        ms = _mm(t * t, ones_ref[...], 2, 1) * (1.0 / QK_DIM)
        t = t * lax.rsqrt(ms + NORM_EPS) * gain
        t = t * cos + pltpu.roll(t, width - half, 1) * sin_a + pltpu.roll(t, half, 1) * sin_b
        return (t + shift).astype(BF16)

    qo = norm_rot(q, gq, shift_q)
    ko = norm_rot(kn + heads(kr), gk, shift_k)
    for h in range(MLA_HEADS):
        sl = slice(h * HEAD_PAD, (h + 1) * HEAD_PAD)
        q_ref[h] = qo[:, sl]
        k_ref[h] = ko[:, sl]
        vt_ref[h] = (vv[:, sl] + ones_col).T[0:V_ROWS].astype(BF16)


def _prep_kernel(ul_ref, uc_ref, wlora_ref, alora_ref, gup_ref, ones_ref, vec_ref,
                 wq_ref, wk_ref, wv_ref, vq_ref, vkv_ref, gains_ref, head_ones_ref, rope_ref, chan_ref,
                 com_ref, lw_ref, dkb_ref, post_ref, q_ref, k_ref, vt_ref, wcl_ref, wcc_ref, *, n_lat_tiles):
    is_lat = pl.program_id(0) < n_lat_tiles
    u = jnp.where(is_lat, ul_ref[...], uc_ref[...])
    _rw_prep_body(u[:, U_RW:U_RW + RW_COLS], wlora_ref, alora_ref, gup_ref, ones_ref, vec_ref,
                  com_ref, lw_ref, dkb_ref, post_ref)
    _mla_prep_body(u[:, U_MLA:U_MLA + MLA_PAD], wq_ref, wk_ref, wv_ref, vq_ref, vkv_ref, gains_ref,
                   head_ones_ref, rope_ref, q_ref, k_ref, vt_ref)
    wc = _mm(u[:, U_FT:U_FT + FT_WIDTH], chan_ref[...], 2, 2)

    @pl.when(is_lat)
    def _():
        wcl_ref[...] = wc

    @pl.when(jnp.logical_not(is_lat))
    def _():
        wcc_ref[...] = wc


def _prep(u_lat, u_ctx, rw_weights, mla_weights, rope_tab, chan_mat):
    tm = math.gcd(STREAM_TILE, u_ctx.shape[0])
    n_lat_tiles = u_lat.shape[0] // tm
    n_rows = u_lat.shape[0] + u_ctx.shape[0]
    w = RW_WIDTH
    full = lambda a: pl.BlockSpec(a.shape, lambda i: (0,) * a.ndim)
    consts = list(rw_weights) + list(mla_weights)
    hs = pl.BlockSpec((MLA_HEADS, tm, HEAD_PAD), lambda i: (0, i, 0))
    shp = jax.ShapeDtypeStruct((MLA_HEADS, n_rows, HEAD_PAD), BF16)
    return pl.pallas_call(
        functools.partial(_prep_kernel, n_lat_tiles=n_lat_tiles),
        grid=(n_rows // tm,),
        in_specs=_two_stream_specs(tm, u_lat.shape[1], 0, n_lat_tiles) + [full(a) for a in consts] + [
            pl.BlockSpec((3, tm, HEAD_PAD), lambda i: (0, i, 0)), full(chan_mat)],
        out_specs=[pl.BlockSpec((3, tm, w), lambda i: (0, i, 0)),
                   pl.BlockSpec((2, tm, w), lambda i: (0, i, 0)),
                   pl.BlockSpec((2, 2, tm, w), lambda i: (0, 0, i, 0)),
                   pl.BlockSpec((2, tm, w), lambda i: (0, i, 0)),
                   hs, hs,
                   pl.BlockSpec((MLA_HEADS, V_ROWS, tm), lambda i: (0, 0, i)),
                   pl.BlockSpec((tm, 2 * FT_WIDTH), lambda i: (jnp.minimum(i, n_lat_tiles - 1), 0)),
                   pl.BlockSpec((tm, 2 * FT_WIDTH), lambda i: (jnp.maximum(i - n_lat_tiles, 0), 0))],
        out_shape=[jax.ShapeDtypeStruct((3, n_rows, w), BF16),
                   jax.ShapeDtypeStruct((2, n_rows, w), F32),
                   jax.ShapeDtypeStruct((2, 2, n_rows, w), BF16),
                   jax.ShapeDtypeStruct((2, n_rows, w), BF16),
                   shp, shp,
                   jax.ShapeDtypeStruct((MLA_HEADS, V_ROWS, n_rows), BF16),
                   jax.ShapeDtypeStruct((u_lat.shape[0], 2 * FT_WIDTH), F32),
                   jax.ShapeDtypeStruct((u_ctx.shape[0], 2 * FT_WIDTH), F32)],
        compiler_params=_cparams(("arbitrary",)),
        name="prep",
    )(u_lat, u_ctx, *consts, rope_tab, chan_mat)


def _attn_kernel(q_ref, k_ref, vt_ref, o_ref, acc_scr, *rest, nk, sub, online):
    kv = pl.program_id(2)
    if online:
        m_scr, = rest

    @pl.when(kv == 0)
    def _():
        acc_scr[...] = jnp.zeros_like(acc_scr)
        if online:
            m_scr[...] = jnp.full_like(m_scr, -jnp.inf)

    k = k_ref[0]
    vt = vt_ref[0]
    n = q_ref.shape[1] // sub
    st, p, alpha = {}, {}, {}
    for t in range(n + 2):
        if t < n:
            st[t] = lax.dot_general(k, q_ref[0, t * sub:(t + 1) * sub], NT, preferred_element_type=F32)
        if 0 <= t - 1 < n:
            r = t - 1
            cols = slice(r * sub, (r + 1) * sub)
            s_r = st.pop(r)
            if online:
                m_prev = m_scr[:, cols]
                m_new = jnp.maximum(m_prev, jnp.max(s_r, axis=0, keepdims=True))
                alpha[r] = jnp.exp2(m_prev - m_new)
                m_scr[:, cols] = m_new
                s_r = s_r - m_new
            p[r] = jnp.exp2(s_r).astype(BF16)
        if 0 <= t - 2 < n:
            r = t - 2
            cols = slice(r * sub, (r + 1) * sub)
            upd = jnp.dot(vt, p.pop(r), preferred_element_type=F32)
            prev = acc_scr[0:V_ROWS, cols]
            acc_scr[0:V_ROWS, cols] = (alpha.pop(r) * prev if online else prev) + upd

    @pl.when(kv == nk - 1)
    def _():
        acc = acc_scr[...]
        o_ref[...] = (acc / acc[V_DIM:V_DIM + 1]).T.astype(BF16)


def _attention_call(q, k, vt, online, q_rows, k_rows):
    h, _, dp = q.shape
    lq = q_rows[1] - q_rows[0]
    lk = k_rows[1] - k_rows[0]
    tq = _row_tile(lq, ATT_TQ)
    sub = min(tq, ATT_SUB)
    tk = max(t for t in range(LANES, min(lk, ATT_TK) + 1, LANES) if lk % t == 0 and k_rows[0] % t == 0)
    nk = lk // tk
    assert q_rows[0] % tq == 0
    qo = q_rows[0] // tq
    ko = k_rows[0] // tk
    scratch = [pltpu.VMEM((HEAD_PAD, tq), F32)]
    if online:
        scratch.append(pltpu.VMEM((1, tq), F32))
    return pl.pallas_call(
        functools.partial(_attn_kernel, nk=nk, sub=sub, online=online),
        grid=(h, lq // tq, nk),
        in_specs=[pl.BlockSpec((1, tq, dp), lambda hh, i, j: (hh, i + qo, 0)),
                  pl.BlockSpec((1, tk, dp), lambda hh, i, j: (hh, j + ko, 0)),
                  pl.BlockSpec((1, V_ROWS, tk), lambda hh, i, j: (hh, 0, j + ko))],
        out_specs=pl.BlockSpec((tq, dp), lambda hh, i, j: (i, hh)),
        out_shape=jax.ShapeDtypeStruct((lq, h * dp), BF16),
        scratch_shapes=scratch,
        compiler_params=_cparams(("parallel", "parallel", "arbitrary")),
        name="attn_online" if online else "attn",
    )(q, k, vt)


def _attention(q, k, vt, score_bound, q_rows, k_rows):
    return lax.cond(2.0 * score_bound <= MAX_SCORE_SPAN,
                    lambda: _attention_call(q, k, vt, False, q_rows, k_rows),
                    lambda: _attention_call(q, k, vt, True, q_rows, k_rows))


def _ft_s1_kernel(w_ref, cs_ref, tw_ref, o_ref, *, l1, g):
    z = _mm(cs_ref[...], w_ref[...], 2, 2)
    fw = FT_WIDTH
    for t in range(g):
        zr = z[:, t * 2 * fw:t * 2 * fw + fw]
        zi = z[:, t * 2 * fw + fw:(t + 1) * 2 * fw]
        a_re = zr[:l1] + zi[l1:]
        a_im = zi[:l1] - zr[l1:]
        tc = jnp.concatenate([tw_ref[0, t]] * (fw // LANES), axis=1)
        ts = jnp.concatenate([tw_ref[1, t]] * (fw // LANES), axis=1)
        o_ref[:, t * 2 * fw:t * 2 * fw + fw] = a_re * tc + a_im * ts
        o_ref[:, t * 2 * fw + fw:(t + 1) * 2 * fw] = a_im * tc - a_re * ts


def _ft_s1(w2d, cs, tw, l1, l2):
    g = min(8, l2)
    blk = g * 2 * FT_WIDTH
    return pl.pallas_call(
        functools.partial(_ft_s1_kernel, l1=l1, g=g),
        grid=(l2 // g,),
        in_specs=[pl.BlockSpec((l1, blk), lambda i: (0, i)),
                  pl.BlockSpec(cs.shape, lambda i: (0, 0)),
                  pl.BlockSpec((2, g, l1, LANES), lambda i: (0, i, 0, 0))],
        out_specs=pl.BlockSpec((l1, blk), lambda i: (0, i)),
        out_shape=jax.ShapeDtypeStruct((l1, l2 * 2 * FT_WIDTH), F32),
        compiler_params=_cparams(("parallel",)),
        name="ft_s1",
    )(w2d, cs, tw)


def _ft_s2_kernel(b_ref, cs_ref, o_ref, *, g, scale):
    fw = FT_WIDTH
    cmat = cs_ref[0]
    smat = cs_ref[1]
    for kk in range(g):
        blk = b_ref[kk]
        out = _mm(cmat, blk[:, :fw], 2, 2) + _mm(smat, blk[:, fw:], 2, 2)
        o_ref[:, kk * fw:(kk + 1) * fw] = out * scale


def _ft_s2(b3d, cs2, l1, l2, scale):
    g = min(8, l1)
    return pl.pallas_call(
        functools.partial(_ft_s2_kernel, g=g, scale=scale),
        grid=(l1 // g,),
        in_specs=[pl.BlockSpec((g, l2, 2 * FT_WIDTH), lambda i: (i, 0, 0)),
                  pl.BlockSpec(cs2.shape, lambda i: (0, 0, 0))],
        out_specs=pl.BlockSpec((l2, g * FT_WIDTH), lambda i: (0, i)),
        out_shape=jax.ShapeDtypeStruct((l2, l1 * FT_WIDTH), F32),
        compiler_params=_cparams(("parallel",)),
        name="ft_s2",
    )(b3d, cs2)


def _ft_tables(n):
    l1 = 1 << (int(math.log2(n)) // 2)
    l2 = n // l1
    assert l1 * l2 == n and l1 % 8 == 0 and l2 % 8 == 0
    k1 = np.arange(l1)
    ang1 = 2.0 * np.pi * np.outer(k1, np.arange(l1)) / l1
    cs1 = np.concatenate([np.cos(ang1), np.sin(ang1)], axis=0)
    angt = 2.0 * np.pi * np.outer(np.arange(l2), k1) / n
    tw = np.stack([np.cos(angt), np.sin(angt)], axis=0)
    tw = np.broadcast_to(tw[..., None], (2, l2, l1, LANES))
    ang2 = 2.0 * np.pi * np.outer(np.arange(l2), np.arange(l2)) / l2
    cs2 = np.stack([np.cos(ang2), np.sin(ang2)], axis=0)
    f = lambda a: jnp.asarray(np.ascontiguousarray(a), F32)
    return l1, l2, f(cs1), f(tw), f(cs2)


def _chan_dft_matrix():
    n = FT_GROUP_DIM
    ang = 2.0 * np.pi * np.outer(np.arange(n), np.arange(n)) / n
    eye = np.eye(FT_GROUPS)
    mat = np.concatenate([np.kron(eye, np.cos(ang)), -np.kron(eye, np.sin(ang))], axis=1)
    return jnp.asarray(mat, F32)


def _fourier_mix(wc):
    n = wc.shape[0]
    l1, l2, cs1, tw, cs2 = _ft_tables(n)
    b = _ft_s1(wc.reshape(l1, l2 * 2 * FT_WIDTH), cs1, tw, l1, l2)
    scale = 1.0 / math.sqrt(n * FT_GROUP_DIM)
    out = _ft_s2(b.reshape(l1, l2, 2 * FT_WIDTH), cs2, l1, l2, scale)
    return out.reshape(n, FT_WIDTH)


def _merge_kernel(yf_ref, ybw_ref, post_ref, avg_ref, ln_ref, yb_ref, yc_ref, ga_ref, gb_ref, gc_ref,
                  x_ref, gt_ref, wa_ref, wb_ref, wc_ref, wo_ref, o_ref):
    def branch(y, w_ref, g_ref):
        t = jnp.dot(y.astype(BF16), w_ref[...], preferred_element_type=F32)
        return g_ref[...].astype(F32) * t

    ya = _rw_post(yf_ref[...].astype(F32) + ybw_ref[...].astype(F32),
                  post_ref[0].astype(F32), post_ref[1].astype(F32),
                  avg_ref[...], ln_ref[...])
    m = branch(ya, wa_ref, ga_ref) + branch(yb_ref[...], wb_ref, gb_ref) + branch(yc_ref[...], wc_ref, gc_ref)
    out = jnp.dot(m.astype(BF16), wo_ref[...], preferred_element_type=F32)
    o_ref[...] = x_ref[...] + gt_ref[...] * out


def _merge(yf, ybw, post, row_off, avg_bd, ln_vec, yb, yc, gates, x, gt, wa, wb, wc, wo):
    n_rows, d = x.shape
    tm = _row_tile(n_rows, 512)
    assert row_off % tm == 0
    off = row_off // tm
    w = RW_WIDTH
    full = lambda a: pl.BlockSpec(a.shape, lambda i: (0,) * a.ndim)
    rows = lambda a: pl.BlockSpec((tm, a.shape[1]), lambda i: (i, 0))
    gate = lambda b: pl.BlockSpec((tm, d), lambda i: (i, b))
    scan = pl.BlockSpec((tm, w), lambda i: (i + off, 0))
    return pl.pallas_call(
        _merge_kernel,
        grid=(n_rows // tm,),
        in_specs=[scan, scan, pl.BlockSpec((2, tm, w), lambda i: (0, i + off, 0)), full(avg_bd), full(ln_vec),
                  rows(yb), rows(yc), gate(0), gate(1), gate(2), rows(x), full(gt),
                  full(wa), full(wb), full(wc), full(wo)],
        out_specs=pl.BlockSpec((tm, d), lambda i: (i, 0)),
        out_shape=jax.ShapeDtypeStruct((n_rows, d), F32),
        compiler_params=_cparams(("parallel",)),
        name="merge",
    )(yf, ybw, post, avg_bd, ln_vec, yb, yc, gates, gates, gates, x, gt, wa, wb, wc, wo)


def _stream_mods(mod):
    return [mod[i * D_MODEL:(i + 1) * D_MODEL][None] for i in range(N_MOD)]


def _pad_heads(w, src_cols, width):
    k = w.shape[0]
    wh = w.reshape(k, MLA_HEADS, width)[:, :, src_cols]
    wh = jnp.pad(wh, ((0, 0), (0, 0), (0, HEAD_PAD - wh.shape[2])))
    return wh.reshape(k, MLA_HEADS * HEAD_PAD)


def _block_diag2(a, b):
    z01 = jnp.zeros((a.shape[0], b.shape[1]), a.dtype)
    z10 = jnp.zeros((b.shape[0], a.shape[1]), a.dtype)
    return jnp.concatenate([jnp.concatenate([a, z01], axis=1), jnp.concatenate([z10, b], axis=1)], axis=0)


def _rope_tables(n, n_ctx):
    half = ROPE_DIM // 2
    inv = ROPE_BASE ** (-np.arange(0, half, 2, dtype=np.float32) / half)
    t = np.arange(n)
    row = (t // GRID_W).astype(np.float32)
    col = (t % GRID_W).astype(np.float32)
    q = half // 2
    cos = np.ones((n, HEAD_PAD), np.float32)
    sin_a = np.zeros((n, HEAD_PAD), np.float32)
    sin_b = np.zeros((n, HEAD_PAD), np.float32)
    for base, pos in ((NOPE_DIM, row), (NOPE_DIM + half, col)):
        ang = pos[:, None] * inv[None, :]
        c, s = np.cos(ang), np.sin(ang)
        cos[:, base:base + q] = c
        cos[:, base + q:base + 2 * q] = c
        sin_a[:, base:base + q] = -s
        sin_b[:, base + q:base + 2 * q] = s
    tab = np.stack([cos, sin_a, sin_b], axis=0)
    ident = np.stack([np.ones((n_ctx, HEAD_PAD), np.float32)] + [np.zeros((n_ctx, HEAD_PAD), np.float32)] * 2)
    return jnp.asarray(np.concatenate([tab, ident], axis=1))


def _layer_weights(l, w):
    w_in = w["w_in"][l]
    o1 = RW_COLS
    o2 = o1 + MLA_COLS
    o3 = o2 + FT_WIDTH
    w_in_p = jnp.concatenate(
        [w_in[:, o3:], w_in[:, :o1], w_in[:, o1:o2],
         jnp.zeros((D_MODEL, MLA_PAD - MLA_COLS), F32), w_in[:, o2:o3]], axis=1).astype(BF16)
    pad_gain = lambda g: jnp.pad(g, (0, HEAD_PAD - QK_DIM))
    wb = w["w_branch_b"][l].reshape(MLA_HEADS, V_DIM, D_MODEL)
    wb = jnp.pad(wb, ((0, 0), (0, HEAD_PAD - V_DIM), (0, 0))).reshape(MLA_HEADS * HEAD_PAD, D_MODEL)
    q_scale = math.log2(math.e) * QK_DIM ** -0.5
    score_bound = (BOUND_MARGIN * QK_DIM * q_scale * jnp.max(jnp.abs(w["q_gain"][l]))
                   * jnp.max(jnp.abs(w["k_gain"][l])))
    shift_lane = (jnp.arange(HEAD_PAD) == QK_DIM).astype(F32)
    return dict(
        w_in=w_in_p, cw_in=w["rw_conv"][l],
        wlora=_block_diag2(w["rw_w_up"][l, 0], w["rw_w_up"][l, 1]),
        alora=_block_diag2(w["rw_a_up"][l, 0], w["rw_a_up"][l, 1]),
        gup=w["rw_g_up"][l],
        rw_vec=jnp.stack([w["rw_k_k"][l], w["rw_k_a"][l], w["rw_r_k"][l].reshape(-1),
                          w["rw_w0"][l, 0], w["rw_w0"][l, 1], w["rw_a0"][l, 0], w["rw_a0"][l, 1],
                          jnp.zeros((RW_WIDTH,), F32)], axis=0),
        ln_vec=jnp.stack([w["rw_ln_w"][l], w["rw_ln_b"][l]], axis=0),
        wq=_pad_heads(w["w_uq"][l], slice(0, QK_DIM), QK_DIM).astype(BF16),
        wk=_pad_heads(w["w_ukv"][l], slice(0, NOPE_DIM), NOPE_DIM + V_DIM).astype(BF16),
        wv=_pad_heads(w["w_ukv"][l], slice(NOPE_DIM, NOPE_DIM + V_DIM), NOPE_DIM + V_DIM).astype(BF16),
        vq=w["mla_q_norm"][l][None], vkv=w["mla_kv_norm"][l][None],
        gains=jnp.stack([pad_gain(w["q_gain"][l]) * q_scale, pad_gain(w["k_gain"][l]),
                             -score_bound * shift_lane, shift_lane], axis=0),
        score_bound=score_bound,
        wa=w["w_branch_a"][l].astype(BF16), wb=wb.astype(BF16), wc=w["w_branch_c"][l].astype(BF16),
        wo=w["w_out"][l].astype(BF16),
        w_up=w["w_up"][l].astype(BF16), cw_ffn=w["ffn_conv"][l], w_down=w["w_down"][l].astype(BF16),
    )


def _forward(x, c, ctx, c_ctx, w):
    depth = w["w_ada"].shape[0]
    x_lat = x[0]
    x_ctx = ctx[0]
    n_lat = x_lat.shape[0]
    n_ctx = x_ctx.shape[0]
    assert n_lat % CHUNK == 0 and n_ctx % CHUNK == 0 and n_lat % GRID_W == 0
    n_all = n_lat + n_ctx

    cc = jnp.zeros((8, D_MODEL), F32).at[0].set(c[0]).at[1].set(c_ctx)
    mods = _mods(cc, w["w_ada"], w["b_ada"])

    head = np.arange(RW_WIDTH) // RW_HEAD_DIM
    ones_bd = jnp.asarray((head[:, None] == head[None, :]).astype(np.float32), BF16)
    avg_bd = ones_bd * (1.0 / RW_HEAD_DIM)
    mla_head = np.arange(MLA_HEADS * HEAD_PAD) // HEAD_PAD
    head_ones = jnp.asarray((mla_head[:, None] == mla_head[None, :]).astype(np.float32), BF16)
    ti = np.arange(CHUNK)
    tri = jnp.asarray(np.stack([ti[None, :] <= ti[:, None], ti[None, :] >= ti[:, None]]).astype(np.float32),
                      BF16)
    chan_mat = _chan_dft_matrix()
    rope_tab = _rope_tables(n_lat, n_ctx)

    for l in range(depth):
        last = l == depth - 1
        p = _layer_weights(l, w)
        streams = [(x_lat, _stream_mods(mods[l, 0]), (0, n_lat)),
                   (x_ctx, _stream_mods(mods[l, 1]), (n_lat, n_all))]
        us, gates = [], []
        for xs, (sh1, sc1, _, _, _, _), _ in streams:
            gain1 = w["g_norm1"][l][None] * (1.0 + sc1)
            g_s, u_s = _in_proj(xs, gain1, sh1, p["w_in"], p["cw_in"])
            gates.append(g_s)
            us.append(u_s)
        com, lw, dkb, post, q, k, vt, wc_lat, wc_ctx = _prep(
            us[0], us[1], (p["wlora"], p["alora"], p["gup"], ones_bd, p["rw_vec"]),
            (p["wq"], p["wk"], p["wv"], p["vq"], p["vkv"], p["gains"], head_ones), rope_tab, chan_mat)
        yf, yb = _rw_scan(com, lw, dkb, tri, n_ctx // CHUNK)

        new = []
        for si, (xs, (_, _, gt1, sh2, sc2, gt2), rows) in enumerate(streams):
            if si == 1 and last:
                continue
            y_att = _attention(q, k, vt, p["score_bound"], rows, (0, n_all) if si == 0 else rows)
            yc = _fourier_mix(wc_lat if si == 0 else wc_ctx)
            x1 = _merge(yf, yb, post, rows[0], avg_bd, p["ln_vec"], y_att, yc, gates[si], xs, gt1,
                        p["wa"], p["wb"], p["wc"], p["wo"])
            gain2 = w["g_norm2"][l][None] * (1.0 + sc2)
            new.append(_ffn(x1, gain2, sh2, gt2, p["w_up"], p["cw_ffn"], p["w_down"]))
        x_lat = new[0]
        if not last:
            x_ctx = new[1]
    return x_lat[None]


def kernel(x, c, ctx, c_ctx, w_ada, b_ada, g_norm1, g_norm2, w_in, rw_conv, rw_w0, rw_w_up, rw_a0, rw_a_up,
           rw_g_up, rw_k_k, rw_k_a, rw_r_k, rw_ln_w, rw_ln_b, mla_q_norm, w_uq, mla_kv_norm, w_ukv, q_gain,
           k_gain, w_branch_a, w_branch_b, w_branch_c, w_out, ffn_conv, w_up, w_down):
    w = dict(w_ada=w_ada, b_ada=b_ada, g_norm1=g_norm1, g_norm2=g_norm2, w_in=w_in, rw_conv=rw_conv,
             rw_w0=rw_w0, rw_w_up=rw_w_up, rw_a0=rw_a0, rw_a_up=rw_a_up, rw_g_up=rw_g_up, rw_k_k=rw_k_k,
             rw_k_a=rw_k_a, rw_r_k=rw_r_k, rw_ln_w=rw_ln_w, rw_ln_b=rw_ln_b, mla_q_norm=mla_q_norm,
             w_uq=w_uq, mla_kv_norm=mla_kv_norm, w_ukv=w_ukv, q_gain=q_gain, k_gain=k_gain,
             w_branch_a=w_branch_a, w_branch_b=w_branch_b, w_branch_c=w_branch_c, w_out=w_out,
             ffn_conv=ffn_conv, w_up=w_up, w_down=w_down)
    return _forward(x, c, ctx, c_ctx, w)
```

```python
import functools
import math

import numpy as np
import jax
import jax.numpy as jnp
from jax import lax
from jax.experimental import pallas as pl
from jax.experimental.pallas import tpu as pltpu

F32 = jnp.float32
BF16 = jnp.bfloat16

D_MODEL = 1024
GRID_W = 64
NORM_EPS = 1e-6
RW_HEADS = 6
RW_HEAD_DIM = 64
RW_WIDTH = RW_HEADS * RW_HEAD_DIM
DECAY_LORA = 64
ICLR_LORA = 64
GATE_LORA = 128
RW_LN_EPS = 64e-5
MLA_HEADS = 6
Q_LORA = 384
KV_LORA = 256
NOPE_DIM = 64
ROPE_DIM = 32
V_DIM = 64
QK_DIM = NOPE_DIM + ROPE_DIM
ROPE_BASE = 10000.0
FT_GROUPS = 4
FT_GROUP_DIM = 64
FT_WIDTH = FT_GROUPS * FT_GROUP_DIM
D_FF = 2816
N_MOD = 6
RW_COLS = 3 * RW_WIDTH + 2 * DECAY_LORA + 2 * ICLR_LORA + GATE_LORA
MLA_COLS = Q_LORA + KV_LORA + ROPE_DIM

LANES = 128
HEAD_PAD = 128
MLA_PAD = 768
GATE_COLS = 3 * D_MODEL
RW_OFF = GATE_COLS
U_RW = 0
U_MLA = U_RW + RW_COLS
U_FT = U_MLA + MLA_PAD
HALO = 16
STREAM_TILE = 256
CHUNK = 128
VMEM_LIMIT = 48 * 1024 * 1024
V_ROWS = 80
BOUND_MARGIN = 1.02
MAX_SCORE_SPAN = 100.0
ATT_TQ, ATT_SUB, ATT_TK = 4096, 256, 3328

NN = (((1,), (0,)), ((), ()))
NT = (((1,), (1,)), ((), ()))


def _pieces(x, n):
    if x.dtype == BF16:
        return [x]
    out = []
    rem = x
    for i in range(n):
        p = rem.astype(BF16)
        out.append(p)
        if i + 1 < n:
            rem = rem - p.astype(F32)
    return out


def _mm(a, b, pa=1, pb=1, dims=NN):
    ap = _pieces(a, pa)
    bp = _pieces(b, pb)
    n = max(len(ap), len(bp))
    acc = None
    for i in reversed(range(len(ap))):
        for j in reversed(range(len(bp))):
            if i + j < n:
                t = lax.dot_general(ap[i], bp[j], dims, preferred_element_type=F32)
                acc = t if acc is None else acc + t
    return acc


def _sigmoid(x):
    return 0.5 * jnp.tanh(0.5 * x) + 0.5


def _cparams(sem):
    return pltpu.CompilerParams(dimension_semantics=sem, vmem_limit_bytes=VMEM_LIMIT)


def _row_tile(n, pref):
    t = min(n, pref)
    while n % t:
        t //= 2
    return t


def _mods_kernel(cc_ref, w_ref, b_ref, o_ref):
    cc = cc_ref[...]
    s = cc * _sigmoid(cc)
    o_ref[0] = _mm(s, w_ref[0], 2, 2) + b_ref[0]


def _mods(cc, w_ada, b_ada):
    depth, d, n = w_ada.shape
    tn = 1536
    return pl.pallas_call(
        _mods_kernel,
        grid=(depth, n // tn),
        in_specs=[pl.BlockSpec((8, d), lambda l, j: (0, 0)),
                  pl.BlockSpec((1, d, tn), lambda l, j: (l, 0, j)),
                  pl.BlockSpec((1, 1, tn), lambda l, j: (l, 0, j))],
        out_specs=pl.BlockSpec((1, 8, tn), lambda l, j: (l, 0, j)),
        out_shape=jax.ShapeDtypeStruct((depth, 8, n), F32),
        compiler_params=_cparams(("parallel", "parallel")),
        name="mods",
    )(cc, w_ada, b_ada.reshape(depth, 1, n))


def _modnorm(xv, g, s):
    ms = jnp.mean(xv * xv, axis=-1, keepdims=True)
    return (xv * lax.rsqrt(ms + NORM_EPS)) * g + s


def _fill_normed(h_scr, x_ref, xp_ref, xn_ref, g, s, i, nrt, tm):
    h_scr[HALO:HALO + tm] = _modnorm(x_ref[...], g, s).astype(BF16)
    hp = jnp.where(i > 0, _modnorm(xp_ref[...], g, s), 0.0)
    h_scr[0:HALO] = hp.astype(BF16)
    hn = jnp.where(i < nrt - 1, _modnorm(xn_ref[...], g, s), 0.0)
    h_scr[HALO + tm:2 * HALO + tm] = hn.astype(BF16)


def _conv3(u, cw, tm):
    rows = u.shape[0]
    up = pltpu.roll(u, 1, 0)
    un = pltpu.roll(u, rows - 1, 0)
    out = cw[0:1] * up + cw[1:2] * u + cw[2:3] * un
    return out[HALO:HALO + tm]


def _halo_specs(tm, d, n_rows):
    r = tm // HALO
    last = n_rows // HALO - 1
    return [pl.BlockSpec((tm, d), lambda i: (i, 0)),
            pl.BlockSpec((HALO, d), lambda i: (jnp.maximum(i * r - 1, 0), 0)),
            pl.BlockSpec((HALO, d), lambda i: (jnp.minimum((i + 1) * r, last), 0))]


def _resident(a):
    return pl.BlockSpec(a.shape, lambda i: (0,) * a.ndim, pipeline_mode=pl.Buffered(1))


def _in_kernel(x_ref, xp_ref, xn_ref, g_ref, s_ref, w_ref, cw_ref, gate_ref, o_ref, h_scr, *, tm, nrt, tn):
    _fill_normed(h_scr, x_ref, xp_ref, xn_ref, g_ref[...], s_ref[...], pl.program_id(0), nrt, tm)
    nj = w_ref.shape[1] // tn
    u = {}
    for t in range(nj + 1):
        if t < nj:
            conv = RW_OFF <= t * tn < RW_OFF + RW_COLS
            h = h_scr[...] if conv else h_scr[HALO:HALO + tm]
            u[t] = jnp.dot(h, w_ref[:, t * tn:(t + 1) * tn], preferred_element_type=F32)
        if t >= 1:
            lo = (t - 1) * tn
            val = u.pop(t - 1)
            if lo < GATE_COLS:
                gate_ref[:, lo:lo + tn] = _sigmoid(val).astype(BF16)
            elif lo < RW_OFF + RW_COLS:
                o_ref[:, lo - GATE_COLS:lo - GATE_COLS + tn] = _conv3(val, cw_ref[:, lo - RW_OFF:lo - RW_OFF + tn], tm)
            else:
                o_ref[:, lo - GATE_COLS:lo - GATE_COLS + tn] = val


def _in_proj(x, gain, shift, w, cw):
    n_rows, d = x.shape
    n = w.shape[1]
    tn = 512
    assert GATE_COLS % tn == 0 and RW_COLS % tn == 0 and n % tn == 0
    tm = _row_tile(n_rows, 512)
    nrt = n_rows // tm
    return pl.pallas_call(
        functools.partial(_in_kernel, tm=tm, nrt=nrt, tn=tn),
        grid=(nrt,),
        in_specs=_halo_specs(tm, d, n_rows) + [_resident(gain), _resident(shift), _resident(w), _resident(cw)],
        out_specs=[pl.BlockSpec((tm, GATE_COLS), lambda i: (i, 0)),
                   pl.BlockSpec((tm, n - GATE_COLS), lambda i: (i, 0))],
        out_shape=[jax.ShapeDtypeStruct((n_rows, GATE_COLS), BF16),
                   jax.ShapeDtypeStruct((n_rows, n - GATE_COLS), F32)],
        scratch_shapes=[pltpu.VMEM((tm + 2 * HALO, d), BF16)],
        compiler_params=_cparams(("parallel",)),
        name="in_proj",
    )(x, x, x, gain, shift, w, cw)


def _ffn_kernel(x_ref, xp_ref, xn_ref, g_ref, s_ref, gt_ref, wu_ref, cw_ref, wd_ref, o_ref, h_scr, *,
                tm, nrt, tf):
    _fill_normed(h_scr, x_ref, xp_ref, xn_ref, g_ref[...], s_ref[...], pl.program_id(0), nrt, tm)
    h = h_scr[...]
    nj = D_FF // tf
    u, gated = {}, {}
    for t in range(nj + 2):
        if t < nj:
            ca = slice(t * tf, (t + 1) * tf)
            cb = slice(D_FF + t * tf, D_FF + (t + 1) * tf)
            u[t] = (jnp.dot(h, wu_ref[:, ca], preferred_element_type=F32),
                    jnp.dot(h, wu_ref[:, cb], preferred_element_type=F32))
        if 0 <= t - 1 < nj:
            j = t - 1
            ua, ub = u.pop(j)
            a = _conv3(ua, cw_ref[:, j * tf:(j + 1) * tf], tm)
            b = _conv3(ub, cw_ref[:, D_FF + j * tf:D_FF + (j + 1) * tf], tm)
            gated[j] = (a * _sigmoid(a) * b).astype(BF16)
        if 0 <= t - 2 < nj:
            j = t - 2
            down = jnp.dot(gated.pop(j), wd_ref[j * tf:(j + 1) * tf, :], preferred_element_type=F32)
            if j == 0:
                o_ref[...] = down
            else:
                o_ref[...] += down
    o_ref[...] = x_ref[...] + gt_ref[...] * o_ref[...]


def _ffn(x, gain, shift, gate, w_up, cw, w_down):
    n_rows, d = x.shape
    tm = _row_tile(n_rows, 1024)
    nrt = n_rows // tm
    return pl.pallas_call(
        functools.partial(_ffn_kernel, tm=tm, nrt=nrt, tf=256),
        grid=(nrt,),
        in_specs=_halo_specs(tm, d, n_rows) + [_resident(gain), _resident(shift), _resident(gate),
                                               _resident(w_up), _resident(cw), _resident(w_down)],
        out_specs=pl.BlockSpec((tm, d), lambda i: (i, 0)),
        out_shape=jax.ShapeDtypeStruct((n_rows, d), F32),
        scratch_shapes=[pltpu.VMEM((tm + 2 * HALO, d), BF16)],
        compiler_params=_cparams(("parallel",)),
        name="ffn",
    )(x, x, x, gain, shift, gate, w_up, cw, w_down)


def _seg_sum(x, ones_bd):
    return _mm(x, ones_bd, 2, 1)


def _two_stream_specs(tm, width, col_block, n_lat_tiles):
    return [pl.BlockSpec((tm, width), lambda i: (jnp.minimum(i, n_lat_tiles - 1), col_block)),
            pl.BlockSpec((tm, width), lambda i: (jnp.maximum(i - n_lat_tiles, 0), col_block))]


def _rw_prep_body(u, wlora_ref, alora_ref, gup_ref, ones_ref, vec_ref, com_ref, lw_ref, dkb_ref, post_ref):
    w = RW_WIDTH
    r = u[:, 0:w]
    k = u[:, w:2 * w]
    v = u[:, 2 * w:3 * w]
    wd = u[:, 3 * w:3 * w + 2 * DECAY_LORA]
    ad = u[:, 3 * w + 2 * DECAY_LORA:3 * w + 2 * DECAY_LORA + 2 * ICLR_LORA]
    gd = u[:, 3 * w + 2 * DECAY_LORA + 2 * ICLR_LORA:]
    vec = vec_ref[...]
    k_k, k_a, r_k = vec[0:1], vec[1:2], vec[2:3]
    ones_bd = ones_ref[...]

    kk = k * k_k
    kk = kk * lax.rsqrt(_seg_sum(kk * kk, ones_bd) + 1e-12)
    g = _mm(_sigmoid(gd), gup_ref[...], 2, 2)
    wl = _mm(jnp.tanh(wd), wlora_ref[...], 2, 2)
    al = _mm(ad, alora_ref[...], 2, 2)
    com_ref[0] = r.astype(BF16)
    com_ref[1] = v.astype(BF16)
    com_ref[2] = (-kk).astype(BF16)
    kd_sum = None
    for d in range(2):
        w0 = vec[3 + d:4 + d]
        a0 = vec[5 + d:6 + d]
        lw = -math.exp(-0.5) * _sigmoid(w0 + wl[:, d * w:(d + 1) * w])
        a = _sigmoid(a0 + al[:, d * w:(d + 1) * w])
        kd = k * (1.0 + (a - 1.0) * k_a)
        lw_ref[d] = lw
        dkb_ref[d, 0] = kd.astype(BF16)
        dkb_ref[d, 1] = (kk * a).astype(BF16)
        kd_sum = kd if kd_sum is None else kd_sum + kd
    post_ref[0] = g.astype(BF16)
    post_ref[1] = (_seg_sum(r * kd_sum * r_k, ones_bd) * v).astype(BF16)


def _smm(a, b, dims=NN):
    return _mm(a, b, 1, 1, dims)


def _scan_dir(com, lw, dkb, tri, slot, reverse):
    c = CHUNK
    r, v, a = (com[i].astype(F32) for i in range(3))
    kd, b = dkb[0].astype(F32), dkb[1].astype(F32)
    cs = _mm(tri, lw, 1, 3)
    tot = cs[0:1] if reverse else cs[c - 1:c]
    e_pos = jnp.exp(cs)
    e_neg = jnp.exp(-cs)
    e_end = jnp.exp(tot - cs)
    at = a * jnp.exp(cs - lw)
    rt = r * e_pos
    bt = b * e_neg
    kt = kd * e_neg
    bh = b * e_end
    kh = kd * e_end
    p_end = jnp.exp(tot)

    row = lax.broadcasted_iota(jnp.int32, (c, c), 0)
    col = lax.broadcasted_iota(jnp.int32, (c, c), 1)
    before = (col > row) if reverse else (col < row)
    upto = (col >= row) if reverse else (col <= row)
    eye = row == col
    same_head = (row // RW_HEAD_DIM) == (col // RW_HEAD_DIM)
    both = lambda m: jnp.concatenate([m, m], axis=1)
    lane = lax.broadcasted_iota(jnp.int32, (c, LANES), 1)
    heads = [(lane // RW_HEAD_DIM) == h for h in range(2)]

    def by_head(x):
        return jnp.concatenate([jnp.where(m, x, 0.0) for m in heads], axis=0).astype(BF16)

    pairs = []
    for p in range(RW_HEADS // 2):
        sl = slice(p * LANES, (p + 1) * LANES)
        pairs.append(dict(
            slot=slot, p=p, by_head=by_head,
            atrt=jnp.concatenate([at[:, sl], rt[:, sl]], axis=0).astype(BF16),
            bk=jnp.concatenate([by_head(bt[:, sl]), by_head(kt[:, sl])], axis=0),
            v2=by_head(v[:, sl]), v_t=v[:, sl].T,
            bkh=jnp.concatenate([bh[:, sl], kh[:, sl]], axis=0).astype(BF16),
            p_end=p_end[:, sl], masks=(both(before), both(upto), both(eye), same_head)))
    return pairs


def _scan_chunk(pairs, s_ref):
    c = CHUNK
    zero = jnp.zeros((c, c), BF16)
    for pr in pairs:
        before, upto, eye, _ = pr["masks"]
        gm = _smm(pr["atrt"], pr["bk"], NT)
        a_ab = jnp.where(before, gm[:c, :2 * c], 0.0)
        pr["a_ak"] = jnp.where(before, gm[:c, 2 * c:], 0.0).astype(BF16)
        pr["a_rb"] = jnp.where(upto, gm[c:, :2 * c], 0.0).astype(BF16)
        pr["a_rk"] = jnp.where(upto, gm[c:, 2 * c:], 0.0).astype(BF16)
        pr["t_inv"] = jnp.where(eye, 1.0, a_ab)
        pr["apow"] = a_ab.astype(BF16)

    def block_diag(m):
        return jnp.concatenate([jnp.concatenate([m[:, :c], zero], axis=1),
                                jnp.concatenate([zero, m[:, c:]], axis=1)], axis=0)

    n_sq = int(math.log2(c)) - 1
    for pr in pairs:
        pr["apow"] = _smm(pr["apow"], block_diag(pr["apow"])).astype(BF16)
    for k in range(n_sq):
        for pr in pairs:
            w_bd = block_diag(pr["apow"])
            if k + 1 < n_sq:
                res = _smm(jnp.concatenate([pr["apow"], pr["t_inv"].astype(BF16)], axis=0), w_bd)
                pr["apow"] = res[:c].astype(BF16)
                pr["t_inv"] = pr["t_inv"] + res[c:]
            else:
                pr["t_inv"] = pr["t_inv"] + _smm(pr["t_inv"], w_bd)
    for pr in pairs:
        s0 = s_ref[pr["slot"], pr["p"]]
        pr["s0"] = s0
        hr = _smm(pr["atrt"], s0, NT)
        pr["ah"], pr["rh"] = hr[:c], hr[c:]
    for pr in pairs:
        x = _smm(pr["a_ak"], pr["v2"])
        pr["u"] = _smm(pr["t_inv"], pr["by_head"](pr["ah"] + x))
    outs = {}
    for pr in pairs:
        pr["y"] = pr["rh"] + _smm(pr["a_rb"], pr["by_head"](pr["u"])) + _smm(pr["a_rk"], pr["v2"])
    for pr in pairs:
        same_head = pr["masks"][3]
        uv_t = jnp.concatenate([pr["u"].T, pr["v_t"]], axis=1)
        upd = _smm(uv_t, pr["bkh"])
        s_ref[pr["slot"], pr["p"]] = pr["s0"] * pr["p_end"] + jnp.where(same_head, upd, 0.0)
        outs[(pr["slot"], pr["p"])] = pr["y"]
    n_pairs = RW_HEADS // 2
    return [jnp.concatenate([outs[(d, p)] for p in range(n_pairs)], axis=1) for d in range(2)]


def _rw_scan_kernel(comf_ref, lwf_ref, dkbf_ref, comb_ref, lwb_ref, dkbb_ref, tri_ref, yf_ref, yb_ref, s_ref):
    @pl.when(pl.program_id(0) == 0)
    def _():
        s_ref[...] = jnp.zeros_like(s_ref)

    pairs = (_scan_dir(comf_ref[...], lwf_ref[0], dkbf_ref[0], tri_ref[0], 0, False)
             + _scan_dir(comb_ref[...], lwb_ref[0], dkbb_ref[0], tri_ref[1], 1, True))
    yf, yb = _scan_chunk(pairs, s_ref)
    yf_ref[...] = yf.astype(BF16)
    yb_ref[...] = yb.astype(BF16)


def _rw_scan(com, lw, dkb, tri, n_ctx_chunks):
    n_rows = com.shape[1]
    w = RW_WIDTH
    c = CHUNK
    assert c == LANES
    n_chunks = n_rows // c
    nc = n_ctx_chunks
    nl = n_chunks - nc

    def fwd_block(i):
        return jnp.where(i < nc, nl + i, i - nc)

    def bwd_block(i):
        return jnp.where(i < nc, nl + nc - 1 - i, nl - 1 - (i - nc))

    return pl.pallas_call(
        _rw_scan_kernel,
        grid=(n_chunks,),
        in_specs=[pl.BlockSpec((3, c, w), lambda i: (0, fwd_block(i), 0)),
                  pl.BlockSpec((1, c, w), lambda i: (0, fwd_block(i), 0)),
                  pl.BlockSpec((1, 2, c, w), lambda i: (0, 0, fwd_block(i), 0)),
                  pl.BlockSpec((3, c, w), lambda i: (0, bwd_block(i), 0)),
                  pl.BlockSpec((1, c, w), lambda i: (1, bwd_block(i), 0)),
                  pl.BlockSpec((1, 2, c, w), lambda i: (1, 0, bwd_block(i), 0)),
                  pl.BlockSpec((2, c, c), lambda i: (0, 0, 0))],
        out_specs=[pl.BlockSpec((c, w), lambda i: (fwd_block(i), 0)),
                   pl.BlockSpec((c, w), lambda i: (bwd_block(i), 0))],
        out_shape=[jax.ShapeDtypeStruct((n_rows, w), BF16)] * 2,
        scratch_shapes=[pltpu.VMEM((2, RW_HEADS // 2, LANES, LANES), F32)],
        compiler_params=_cparams(("arbitrary",)),
        name="rw_scan",
    )(com, lw, dkb, com, lw, dkb, tri)


def _rw_post(y, g, bonus, avg, vec):
    mu = _mm(y, avg, 2, 1)
    yc = y - mu
    var = _mm(yc * yc, avg, 2, 1)
    yn = yc * lax.rsqrt(var + RW_LN_EPS)
    return (yn * vec[0:1] + vec[1:2] + bonus) * g


def _rms(x, n):
    return x * lax.rsqrt(jnp.sum(x * x, axis=-1, keepdims=True) * (1.0 / n) + NORM_EPS)


def _mla_prep_body(u, wq_ref, wk_ref, wv_ref, vq_ref, vkv_ref, gains_ref, ones_ref, rope_ref,
                   q_ref, k_ref, vt_ref):
    cq =_rms(u[:, :Q_LORA], Q_LORA) * vq_ref[...]
    q = jnp.dot(cq.astype(BF16), wq_ref[...], preferred_element_type=F32)
    ckv = _rms(u[:, Q_LORA:Q_LORA + KV_LORA], KV_LORA) * vkv_ref[...]
    ckv = ckv.astype(BF16)
    kn = jnp.dot(ckv, wk_ref[...], preferred_element_type=F32)
    vv = jnp.dot(ckv, wv_ref[...], preferred_element_type=F32)
    kr = pltpu.roll(u[:, Q_LORA + KV_LORA:], NOPE_DIM, 1)
    lane = lax.broadcasted_iota(jnp.int32, (1, HEAD_PAD), 1)
    ones_col = (lane == V_DIM).astype(F32)
    heads = lambda a: jnp.concatenate([a] * MLA_HEADS, axis=1)
    gains = gains_ref[...]
    gq, gk, shift_q, shift_k = (heads(gains[i:i + 1]) for i in range(4))
    tab = rope_ref[...]
    cos, sin_a, sin_b = heads(tab[0]), heads(tab[1]), heads(tab[2])
    width = MLA_HEADS * HEAD_PAD
    half = ROPE_DIM // 4

    def norm_rot(t, gain, shift):
        ms = _mm(t * t, ones_ref[...], 2, 1) * (1.0 / QK_DIM)
        t = t * lax.rsqrt(ms + NORM_EPS) * gain
        t = t * cos + pltpu.roll(t, width - half, 1) * sin_a + pltpu.roll(t, half, 1) * sin_b
        return (t + shift).astype(BF16)

    qo = norm_rot(q, gq, shift_q)
    ko = norm_rot(kn + heads(kr), gk, shift_k)
    for h in range(MLA_HEADS):
        sl = slice(h * HEAD_PAD, (h + 1) * HEAD_PAD)
        q_ref[h] = qo[:, sl]
        k_ref[h] = ko[:, sl]
        vt_ref[h] = (vv[:, sl] + ones_col).T[0:V_ROWS].astype(BF16)


def _prep_kernel(ul_ref, uc_ref, wlora_ref, alora_ref, gup_ref, ones_ref, vec_ref,
                 wq_ref, wk_ref, wv_ref, vq_ref, vkv_ref, gains_ref, head_ones_ref, rope_ref, chan_ref,
                 com_ref, lw_ref, dkb_ref, post_ref, q_ref, k_ref, vt_ref, wcl_ref, wcc_ref, *, n_lat_tiles):
    is_lat = pl.program_id(0) < n_lat_tiles
    u = jnp.where(is_lat, ul_ref[...], uc_ref[...])
    _rw_prep_body(u[:, U_RW:U_RW + RW_COLS], wlora_ref, alora_ref, gup_ref, ones_ref, vec_ref,
                  com_ref, lw_ref, dkb_ref, post_ref)
    _mla_prep_body(u[:, U_MLA:U_MLA + MLA_PAD], wq_ref, wk_ref, wv_ref, vq_ref, vkv_ref, gains_ref,
                   head_ones_ref, rope_ref, q_ref, k_ref, vt_ref)
    wc = _mm(u[:, U_FT:U_FT + FT_WIDTH], chan_ref[...], 2, 2)

    @pl.when(is_lat)
    def _():
        wcl_ref[...] = wc

    @pl.when(jnp.logical_not(is_lat))
    def _():
        wcc_ref[...] = wc


def _prep(u_lat, u_ctx, rw_weights, mla_weights, rope_tab, chan_mat):
    tm = math.gcd(STREAM_TILE, u_ctx.shape[0])
    n_lat_tiles = u_lat.shape[0] // tm
    n_rows = u_lat.shape[0] + u_ctx.shape[0]
    w = RW_WIDTH
    full = lambda a: pl.BlockSpec(a.shape, lambda i: (0,) * a.ndim)
    consts = list(rw_weights) + list(mla_weights)
    hs = pl.BlockSpec((MLA_HEADS, tm, HEAD_PAD), lambda i: (0, i, 0))
    shp = jax.ShapeDtypeStruct((MLA_HEADS, n_rows, HEAD_PAD), BF16)
    return pl.pallas_call(
        functools.partial(_prep_kernel, n_lat_tiles=n_lat_tiles),
        grid=(n_rows // tm,),
        in_specs=_two_stream_specs(tm, u_lat.shape[1], 0, n_lat_tiles) + [full(a) for a in consts] + [
            pl.BlockSpec((3, tm, HEAD_PAD), lambda i: (0, i, 0)), full(chan_mat)],
        out_specs=[pl.BlockSpec((3, tm, w), lambda i: (0, i, 0)),
                   pl.BlockSpec((2, tm, w), lambda i: (0, i, 0)),
                   pl.BlockSpec((2, 2, tm, w), lambda i: (0, 0, i, 0)),
                   pl.BlockSpec((2, tm, w), lambda i: (0, i, 0)),
                   hs, hs,
                   pl.BlockSpec((MLA_HEADS, V_ROWS, tm), lambda i: (0, 0, i)),
                   pl.BlockSpec((tm, 2 * FT_WIDTH), lambda i: (jnp.minimum(i, n_lat_tiles - 1), 0)),
                   pl.BlockSpec((tm, 2 * FT_WIDTH), lambda i: (jnp.maximum(i - n_lat_tiles, 0), 0))],
        out_shape=[jax.ShapeDtypeStruct((3, n_rows, w), BF16),
                   jax.ShapeDtypeStruct((2, n_rows, w), F32),
                   jax.ShapeDtypeStruct((2, 2, n_rows, w), BF16),
                   jax.ShapeDtypeStruct((2, n_rows, w), BF16),
                   shp, shp,
                   jax.ShapeDtypeStruct((MLA_HEADS, V_ROWS, n_rows), BF16),
                   jax.ShapeDtypeStruct((u_lat.shape[0], 2 * FT_WIDTH), F32),
                   jax.ShapeDtypeStruct((u_ctx.shape[0], 2 * FT_WIDTH), F32)],
        compiler_params=_cparams(("arbitrary",)),
        name="prep",
    )(u_lat, u_ctx, *consts, rope_tab, chan_mat)


def _attn_kernel(q_ref, k_ref, vt_ref, o_ref, acc_scr, *rest, nk, sub, online):
    kv = pl.program_id(2)
    if online:
        m_scr, = rest

    @pl.when(kv == 0)
    def _():
        acc_scr[...] = jnp.zeros_like(acc_scr)
        if online:
            m_scr[...] = jnp.full_like(m_scr, -jnp.inf)

    k = k_ref[0]
    vt = vt_ref[0]
    n = q_ref.shape[1] // sub
    st, p, alpha = {}, {}, {}
    for t in range(n + 2):
        if t < n:
            st[t] = lax.dot_general(k, q_ref[0, t * sub:(t + 1) * sub], NT, preferred_element_type=F32)
        if 0 <= t - 1 < n:
            r = t - 1
            cols = slice(r * sub, (r + 1) * sub)
            s_r = st.pop(r)
            if online:
                m_prev = m_scr[:, cols]
                m_new = jnp.maximum(m_prev, jnp.max(s_r, axis=0, keepdims=True))
                alpha[r] = jnp.exp2(m_prev - m_new)
                m_scr[:, cols] = m_new
                s_r = s_r - m_new
            p[r] = jnp.exp2(s_r).astype(BF16)
        if 0 <= t - 2 < n:
            r = t - 2
            cols = slice(r * sub, (r + 1) * sub)
            upd = jnp.dot(vt, p.pop(r), preferred_element_type=F32)
            prev = acc_scr[0:V_ROWS, cols]
            acc_scr[0:V_ROWS, cols] = (alpha.pop(r) * prev if online else prev) + upd

    @pl.when(kv == nk - 1)
    def _():
        acc = acc_scr[...]
        o_ref[...] = (acc / acc[V_DIM:V_DIM + 1]).T.astype(BF16)


def _attention_call(q, k, vt, online, q_rows, k_rows):
    h, _, dp = q.shape
    lq = q_rows[1] - q_rows[0]
    lk = k_rows[1] - k_rows[0]
    tq = _row_tile(lq, ATT_TQ)
    sub = min(tq, ATT_SUB)
    tk = max(t for t in range(LANES, min(lk, ATT_TK) + 1, LANES) if lk % t == 0 and k_rows[0] % t == 0)
    nk = lk // tk
    assert q_rows[0] % tq == 0
    qo = q_rows[0] // tq
    ko = k_rows[0] // tk
    scratch = [pltpu.VMEM((HEAD_PAD, tq), F32)]
    if online:
        scratch.append(pltpu.VMEM((1, tq), F32))
    return pl.pallas_call(
        functools.partial(_attn_kernel, nk=nk, sub=sub, online=online),
        grid=(h, lq // tq, nk),
        in_specs=[pl.BlockSpec((1, tq, dp), lambda hh, i, j: (hh, i + qo, 0)),
                  pl.BlockSpec((1, tk, dp), lambda hh, i, j: (hh, j + ko, 0)),
                  pl.BlockSpec((1, V_ROWS, tk), lambda hh, i, j: (hh, 0, j + ko))],
        out_specs=pl.BlockSpec((tq, dp), lambda hh, i, j: (i, hh)),
        out_shape=jax.ShapeDtypeStruct((lq, h * dp), BF16),
        scratch_shapes=scratch,
        compiler_params=_cparams(("parallel", "parallel", "arbitrary")),
        name="attn_online" if online else "attn",
    )(q, k, vt)


def _attention(q, k, vt, score_bound, q_rows, k_rows):
    return lax.cond(2.0 * score_bound <= MAX_SCORE_SPAN,
                    lambda: _attention_call(q, k, vt, False, q_rows, k_rows),
                    lambda: _attention_call(q, k, vt, True, q_rows, k_rows))


def _ft_s1_kernel(w_ref, cs_ref, tw_ref, o_ref, *, l1, g):
    z = _mm(cs_ref[...], w_ref[...], 2, 2)
    fw = FT_WIDTH
    for t in range(g):
        zr = z[:, t * 2 * fw:t * 2 * fw + fw]
        zi = z[:, t * 2 * fw + fw:(t + 1) * 2 * fw]
        a_re = zr[:l1] + zi[l1:]
        a_im = zi[:l1] - zr[l1:]
        tc = jnp.concatenate([tw_ref[0, t]] * (fw // LANES), axis=1)
        ts = jnp.concatenate([tw_ref[1, t]] * (fw // LANES), axis=1)
        o_ref[:, t * 2 * fw:t * 2 * fw + fw] = a_re * tc + a_im * ts
        o_ref[:, t * 2 * fw + fw:(t + 1) * 2 * fw] = a_im * tc - a_re * ts


def _ft_s1(w2d, cs, tw, l1, l2):
    g = min(8, l2)
    blk = g * 2 * FT_WIDTH
    return pl.pallas_call(
        functools.partial(_ft_s1_kernel, l1=l1, g=g),
        grid=(l2 // g,),
        in_specs=[pl.BlockSpec((l1, blk), lambda i: (0, i)),
                  pl.BlockSpec(cs.shape, lambda i: (0, 0)),
                  pl.BlockSpec((2, g, l1, LANES), lambda i: (0, i, 0, 0))],
        out_specs=pl.BlockSpec((l1, blk), lambda i: (0, i)),
        out_shape=jax.ShapeDtypeStruct((l1, l2 * 2 * FT_WIDTH), F32),
        compiler_params=_cparams(("parallel",)),
        name="ft_s1",
    )(w2d, cs, tw)


def _ft_s2_kernel(b_ref, cs_ref, o_ref, *, g, scale):
    fw = FT_WIDTH
    cmat = cs_ref[0]
    smat = cs_ref[1]
    for kk in range(g):
        blk = b_ref[kk]
        out = _mm(cmat, blk[:, :fw], 2, 2) + _mm(smat, blk[:, fw:], 2, 2)
        o_ref[:, kk * fw:(kk + 1) * fw] = out * scale


def _ft_s2(b3d, cs2, l1, l2, scale):
    g = min(8, l1)
    return pl.pallas_call(
        functools.partial(_ft_s2_kernel, g=g, scale=scale),
        grid=(l1 // g,),
        in_specs=[pl.BlockSpec((g, l2, 2 * FT_WIDTH), lambda i: (i, 0, 0)),
                  pl.BlockSpec(cs2.shape, lambda i: (0, 0, 0))],
        out_specs=pl.BlockSpec((l2, g * FT_WIDTH), lambda i: (0, i)),
        out_shape=jax.ShapeDtypeStruct((l2, l1 * FT_WIDTH), F32),
        compiler_params=_cparams(("parallel",)),
        name="ft_s2",
    )(b3d, cs2)


def _ft_tables(n):
    l1 = 1 << (int(math.log2(n)) // 2)
    l2 = n // l1
    assert l1 * l2 == n and l1 % 8 == 0 and l2 % 8 == 0
    k1 = np.arange(l1)
    ang1 = 2.0 * np.pi * np.outer(k1, np.arange(l1)) / l1
    cs1 = np.concatenate([np.cos(ang1), np.sin(ang1)], axis=0)
    angt = 2.0 * np.pi * np.outer(np.arange(l2), k1) / n
    tw = np.stack([np.cos(angt), np.sin(angt)], axis=0)
    tw = np.broadcast_to(tw[..., None], (2, l2, l1, LANES))
    ang2 = 2.0 * np.pi * np.outer(np.arange(l2), np.arange(l2)) / l2
    cs2 = np.stack([np.cos(ang2), np.sin(ang2)], axis=0)
    f = lambda a: jnp.asarray(np.ascontiguousarray(a), F32)
    return l1, l2, f(cs1), f(tw), f(cs2)


def _chan_dft_matrix():
    n = FT_GROUP_DIM
    ang = 2.0 * np.pi * np.outer(np.arange(n), np.arange(n)) / n
    eye = np.eye(FT_GROUPS)
    mat = np.concatenate([np.kron(eye, np.cos(ang)), -np.kron(eye, np.sin(ang))], axis=1)
    return jnp.asarray(mat, F32)


def _fourier_mix(wc):
    n = wc.shape[0]
    l1, l2, cs1, tw, cs2 = _ft_tables(n)
    b = _ft_s1(wc.reshape(l1, l2 * 2 * FT_WIDTH), cs1, tw, l1, l2)
    scale = 1.0 / math.sqrt(n * FT_GROUP_DIM)
    out = _ft_s2(b.reshape(l1, l2, 2 * FT_WIDTH), cs2, l1, l2, scale)
    return out.reshape(n, FT_WIDTH)


def _merge_kernel(yf_ref, ybw_ref, post_ref, avg_ref, ln_ref, yb_ref, yc_ref, ga_ref, gb_ref, gc_ref,
                  x_ref, gt_ref, wa_ref, wb_ref, wc_ref, wo_ref, o_ref):
    def branch(y, w_ref, g_ref):
        t = jnp.dot(y.astype(BF16), w_ref[...], preferred_element_type=F32)
        return g_ref[...].astype(F32) * t

    ya = _rw_post(yf_ref[...].astype(F32) + ybw_ref[...].astype(F32),
                  post_ref[0].astype(F32), post_ref[1].astype(F32),
                  avg_ref[...], ln_ref[...])
    m = branch(ya, wa_ref, ga_ref) + branch(yb_ref[...], wb_ref, gb_ref) + branch(yc_ref[...], wc_ref, gc_ref)
    out = jnp.dot(m.astype(BF16), wo_ref[...], preferred_element_type=F32)
    o_ref[...] = x_ref[...] + gt_ref[...] * out


def _merge(yf, ybw, post, row_off, avg_bd, ln_vec, yb, yc, gates, x, gt, wa, wb, wc, wo):
    n_rows, d = x.shape
    tm = _row_tile(n_rows, 512)
    assert row_off % tm == 0
    off = row_off // tm
    w = RW_WIDTH
    full = lambda a: pl.BlockSpec(a.shape, lambda i: (0,) * a.ndim)
    rows = lambda a: pl.BlockSpec((tm, a.shape[1]), lambda i: (i, 0))
    gate = lambda b: pl.BlockSpec((tm, d), lambda i: (i, b))
    scan = pl.BlockSpec((tm, w), lambda i: (i + off, 0))
    return pl.pallas_call(
        _merge_kernel,
        grid=(n_rows // tm,),
        in_specs=[scan, scan, pl.BlockSpec((2, tm, w), lambda i: (0, i + off, 0)), full(avg_bd), full(ln_vec),
                  rows(yb), rows(yc), gate(0), gate(1), gate(2), rows(x), full(gt),
                  full(wa), full(wb), full(wc), full(wo)],
        out_specs=pl.BlockSpec((tm, d), lambda i: (i, 0)),
        out_shape=jax.ShapeDtypeStruct((n_rows, d), F32),
        compiler_params=_cparams(("parallel",)),
        name="merge",
    )(yf, ybw, post, avg_bd, ln_vec, yb, yc, gates, gates, gates, x, gt, wa, wb, wc, wo)


def _stream_mods(mod):
    return [mod[i * D_MODEL:(i + 1) * D_MODEL][None] for i in range(N_MOD)]


def _pad_heads(w, src_cols, width):
    k = w.shape[0]
    wh = w.reshape(k, MLA_HEADS, width)[:, :, src_cols]
    wh = jnp.pad(wh, ((0, 0), (0, 0), (0, HEAD_PAD - wh.shape[2])))
    return wh.reshape(k, MLA_HEADS * HEAD_PAD)


def _block_diag2(a, b):
    z01 = jnp.zeros((a.shape[0], b.shape[1]), a.dtype)
    z10 = jnp.zeros((b.shape[0], a.shape[1]), a.dtype)
    return jnp.concatenate([jnp.concatenate([a, z01], axis=1), jnp.concatenate([z10, b], axis=1)], axis=0)


def _rope_tables(n, n_ctx):
    half = ROPE_DIM // 2
    inv = ROPE_BASE ** (-np.arange(0, half, 2, dtype=np.float32) / half)
    t = np.arange(n)
    row = (t // GRID_W).astype(np.float32)
    col = (t % GRID_W).astype(np.float32)
    q = half // 2
    cos = np.ones((n, HEAD_PAD), np.float32)
    sin_a = np.zeros((n, HEAD_PAD), np.float32)
    sin_b = np.zeros((n, HEAD_PAD), np.float32)
    for base, pos in ((NOPE_DIM, row), (NOPE_DIM + half, col)):
        ang = pos[:, None] * inv[None, :]
        c, s = np.cos(ang), np.sin(ang)
        cos[:, base:base + q] = c
        cos[:, base + q:base + 2 * q] = c
        sin_a[:, base:base + q] = -s
        sin_b[:, base + q:base + 2 * q] = s
    tab = np.stack([cos, sin_a, sin_b], axis=0)
    ident = np.stack([np.ones((n_ctx, HEAD_PAD), np.float32)] + [np.zeros((n_ctx, HEAD_PAD), np.float32)] * 2)
    return jnp.asarray(np.concatenate([tab, ident], axis=1))


def _layer_weights(l, w):
    w_in = w["w_in"][l]
    o1 = RW_COLS
    o2 = o1 + MLA_COLS
    o3 = o2 + FT_WIDTH
    w_in_p = jnp.concatenate(
        [w_in[:, o3:], w_in[:, :o1], w_in[:, o1:o2],
         jnp.zeros((D_MODEL, MLA_PAD - MLA_COLS), F32), w_in[:, o2:o3]], axis=1).astype(BF16)
    pad_gain = lambda g: jnp.pad(g, (0, HEAD_PAD - QK_DIM))
    wb = w["w_branch_b"][l].reshape(MLA_HEADS, V_DIM, D_MODEL)
    wb = jnp.pad(wb, ((0, 0), (0, HEAD_PAD - V_DIM), (0, 0))).reshape(MLA_HEADS * HEAD_PAD, D_MODEL)
    q_scale = math.log2(math.e) * QK_DIM ** -0.5
    score_bound = (BOUND_MARGIN * QK_DIM * q_scale * jnp.max(jnp.abs(w["q_gain"][l]))
                   * jnp.max(jnp.abs(w["k_gain"][l])))
    shift_lane = (jnp.arange(HEAD_PAD) == QK_DIM).astype(F32)
    return dict(
        w_in=w_in_p, cw_in=w["rw_conv"][l],
        wlora=_block_diag2(w["rw_w_up"][l, 0], w["rw_w_up"][l, 1]),
        alora=_block_diag2(w["rw_a_up"][l, 0], w["rw_a_up"][l, 1]),
        gup=w["rw_g_up"][l],
        rw_vec=jnp.stack([w["rw_k_k"][l], w["rw_k_a"][l], w["rw_r_k"][l].reshape(-1),
                          w["rw_w0"][l, 0], w["rw_w0"][l, 1], w["rw_a0"][l, 0], w["rw_a0"][l, 1],
                          jnp.zeros((RW_WIDTH,), F32)], axis=0),
        ln_vec=jnp.stack([w["rw_ln_w"][l], w["rw_ln_b"][l]], axis=0),
        wq=_pad_heads(w["w_uq"][l], slice(0, QK_DIM), QK_DIM).astype(BF16),
        wk=_pad_heads(w["w_ukv"][l], slice(0, NOPE_DIM), NOPE_DIM + V_DIM).astype(BF16),
        wv=_pad_heads(w["w_ukv"][l], slice(NOPE_DIM, NOPE_DIM + V_DIM), NOPE_DIM + V_DIM).astype(BF16),
        vq=w["mla_q_norm"][l][None], vkv=w["mla_kv_norm"][l][None],
        gains=jnp.stack([pad_gain(w["q_gain"][l]) * q_scale, pad_gain(w["k_gain"][l]),
                             -score_bound * shift_lane, shift_lane], axis=0),
        score_bound=score_bound,
        wa=w["w_branch_a"][l].astype(BF16), wb=wb.astype(BF16), wc=w["w_branch_c"][l].astype(BF16),
        wo=w["w_out"][l].astype(BF16),
        w_up=w["w_up"][l].astype(BF16), cw_ffn=w["ffn_conv"][l], w_down=w["w_down"][l].astype(BF16),
    )


def _forward(x, c, ctx, c_ctx, w):
    depth = w["w_ada"].shape[0]
    x_lat = x[0]
    x_ctx = ctx[0]
    n_lat = x_lat.shape[0]
    n_ctx = x_ctx.shape[0]
    assert n_lat % CHUNK == 0 and n_ctx % CHUNK == 0 and n_lat % GRID_W == 0
    n_all = n_lat + n_ctx

    cc = jnp.zeros((8, D_MODEL), F32).at[0].set(c[0]).at[1].set(c_ctx)
    mods = _mods(cc, w["w_ada"], w["b_ada"])

    head = np.arange(RW_WIDTH) // RW_HEAD_DIM
    ones_bd = jnp.asarray((head[:, None] == head[None, :]).astype(np.float32), BF16)
    avg_bd = ones_bd * (1.0 / RW_HEAD_DIM)
    mla_head = np.arange(MLA_HEADS * HEAD_PAD) // HEAD_PAD
    head_ones = jnp.asarray((mla_head[:, None] == mla_head[None, :]).astype(np.float32), BF16)
    ti = np.arange(CHUNK)
    tri = jnp.asarray(np.stack([ti[None, :] <= ti[:, None], ti[None, :] >= ti[:, None]]).astype(np.float32),
                      BF16)
    chan_mat = _chan_dft_matrix()
    rope_tab = _rope_tables(n_lat, n_ctx)

    for l in range(depth):
        last = l == depth - 1
        p = _layer_weights(l, w)
        streams = [(x_lat, _stream_mods(mods[l, 0]), (0, n_lat)),
                   (x_ctx, _stream_mods(mods[l, 1]), (n_lat, n_all))]
        us, gates = [], []
        for xs, (sh1, sc1, _, _, _, _), _ in streams:
            gain1 = w["g_norm1"][l][None] * (1.0 + sc1)
            g_s, u_s = _in_proj(xs, gain1, sh1, p["w_in"], p["cw_in"])
            gates.append(g_s)
            us.append(u_s)
        com, lw, dkb, post, q, k, vt, wc_lat, wc_ctx = _prep(
            us[0], us[1], (p["wlora"], p["alora"], p["gup"], ones_bd, p["rw_vec"]),
            (p["wq"], p["wk"], p["wv"], p["vq"], p["vkv"], p["gains"], head_ones), rope_tab, chan_mat)
        yf, yb = _rw_scan(com, lw, dkb, tri, n_ctx // CHUNK)

        new = []
        for si, (xs, (_, _, gt1, sh2, sc2, gt2), rows) in enumerate(streams):
            if si == 1 and last:
                continue
            y_att = _attention(q, k, vt, p["score_bound"], rows, (0, n_all) if si == 0 else rows)
            yc = _fourier_mix(wc_lat if si == 0 else wc_ctx)
            x1 = _merge(yf, yb, post, rows[0], avg_bd, p["ln_vec"], y_att, yc, gates[si], xs, gt1,
                        p["wa"], p["wb"], p["wc"], p["wo"])
            gain2 = w["g_norm2"][l][None] * (1.0 + sc2)
            new.append(_ffn(x1, gain2, sh2, gt2, p["w_up"], p["cw_ffn"], p["w_down"]))
        x_lat = new[0]
        if not last:
            x_ctx = new[1]
    return x_lat[None]


def kernel(x, c, ctx, c_ctx, w_ada, b_ada, g_norm1, g_norm2, w_in, rw_conv, rw_w0, rw_w_up, rw_a0, rw_a_up,
           rw_g_up, rw_k_k, rw_k_a, rw_r_k, rw_ln_w, rw_ln_b, mla_q_norm, w_uq, mla_kv_norm, w_ukv, q_gain,
           k_gain, w_branch_a, w_branch_b, w_branch_c, w_out, ffn_conv, w_up, w_down):
    w = dict(w_ada=w_ada, b_ada=b_ada, g_norm1=g_norm1, g_norm2=g_norm2, w_in=w_in, rw_conv=rw_conv,
             rw_w0=rw_w0, rw_w_up=rw_w_up, rw_a0=rw_a0, rw_a_up=rw_a_up, rw_g_up=rw_g_up, rw_k_k=rw_k_k,
             rw_k_a=rw_k_a, rw_r_k=rw_r_k, rw_ln_w=rw_ln_w, rw_ln_b=rw_ln_b, mla_q_norm=mla_q_norm,
             w_uq=w_uq, mla_kv_norm=mla_kv_norm, w_ukv=w_ukv, q_gain=q_gain, k_gain=k_gain,
             w_branch_a=w_branch_a, w_branch_b=w_branch_b, w_branch_c=w_branch_c, w_out=w_out,
             ffn_conv=ffn_conv, w_up=w_up, w_down=w_down)
    return _forward(x, c, ctx, c_ctx, w)
```
